```python
import math
import jax, jax.numpy as jnp
from jax import lax
import numpy as np

D_MODEL = 1024
BATCH = 32
SEQ = 2048
DEPTH = 4
DEC_BATCH = 8
DEC_SEQ = 4096
PAST_LEN = 128

HEAD_DIM = 64
N_HEADS = D_MODEL // HEAD_DIM
A_HEADS = N_HEADS // 2
A_KV_HEADS = 2
A_GROUP = A_HEADS // A_KV_HEADS
A_RADIUS = 128
B_HEADS = N_HEADS - A_HEADS
B_DILATIONS = ((128, 1), (512, 4), (2048, 16))
A_WIDTH = A_HEADS * HEAD_DIM
A_KV_WIDTH = A_KV_HEADS * HEAD_DIM
B_WIDTH = B_HEADS * HEAD_DIM
MIX_WIDTH = A_WIDTH + B_WIDTH
IN_WIDTH = A_WIDTH + 2 * A_KV_WIDTH + 3 * B_WIDTH
N_EXPERTS = 16
N_GROUPS = 4
EXPERTS_PER_GROUP = N_EXPERTS // N_GROUPS
TOP_K = 2
D_EXPERT = 256
D_SHARED = 256
DEEPNORM_ALPHA = (2 * DEPTH) ** 0.25
DEEPNORM_BETA = (8 * DEPTH) ** -0.25
LN_EPS = 1e-5
RMS_EPS = 1e-6

kernel_name = "hybrid_bidir_window_dilated_moe_encoder"


def alibi_slopes():
    h = jnp.arange(1, N_HEADS + 1, dtype=jnp.float32)
    return jnp.exp2(-8.0 * h / N_HEADS)


def layer_norm(x, g, b):
    xf = x.astype(jnp.float32)
    mu = xf.mean(-1, keepdims=True)
    var = jnp.square(xf - mu).mean(-1, keepdims=True)
    return ((xf - mu) * lax.rsqrt(var + LN_EPS) * g.astype(jnp.float32) + b.astype(jnp.float32)).astype(x.dtype)


def rms_norm(x, g):
    xf = x.astype(jnp.float32)
    ms = jnp.square(xf).mean(-1, keepdims=True)
    return (xf * lax.rsqrt(ms + RMS_EPS) * g.astype(jnp.float32)).astype(x.dtype)


def banded_attention(q, k, v, radius, slopes, dist_scale, sink=None):
    bsz, L, G, R, hd = q.shape
    blk = radius
    nb = -(-L // blk)
    Lp = nb * blk
    qb = jnp.pad(q, ((0, 0), (0, Lp - L), (0, 0), (0, 0), (0, 0))).reshape(bsz, nb, blk, G, R, hd).astype(jnp.float32)
    padk = ((0, 0), (blk, Lp - L + blk), (0, 0), (0, 0))
    kp = jnp.pad(k, padk).reshape(bsz, nb + 2, blk, G, hd).astype(jnp.float32)
    vp = jnp.pad(v, padk).reshape(bsz, nb + 2, blk, G, hd).astype(jnp.float32)

    def band(t):
        return jnp.concatenate([t[:, :-2], t[:, 1:-1], t[:, 2:]], axis=2)

    kb, vb = band(kp), band(vp)
    s = jnp.einsum('bnqgrd,bnkgd->bngrqk', qb, kb) * (hd ** -0.5)
    qpos = jnp.arange(nb)[:, None] * blk + jnp.arange(blk)[None, :]
    kpos = (jnp.arange(nb)[:, None] - 1) * blk + jnp.arange(3 * blk)[None, :]
    rel = kpos[:, None, :] - qpos[:, :, None]
    valid = (jnp.abs(rel) <= radius) & (kpos[:, None, :] >= 0) & (kpos[:, None, :] < L)
    dist = jnp.abs(rel).astype(jnp.float32) * dist_scale
    s = s - slopes.astype(jnp.float32)[None, None, :, :, None, None] * dist[None, :, None, None]
    s = jnp.where(valid[None, :, None, None], s, -jnp.inf)
    m = s.max(-1)
    if sink is not None:
        sk = sink.astype(jnp.float32)[None, None, :, :, None]
        m = jnp.maximum(m, sk)
    p = jnp.exp(s - m[..., None])
    denom = p.sum(-1)
    if sink is not None:
        denom = denom + jnp.exp(sk - m)
    o = jnp.einsum('bngrqk,bnkgd->bnqgrd', p, vb) / jnp.transpose(denom, (0, 1, 4, 2, 3))[..., None]
    o = o.reshape(bsz, Lp, G, R, hd)[:, :L]
    lse = jnp.transpose(m + jnp.log(denom), (0, 1, 4, 2, 3)).reshape(bsz, Lp, G, R)[:, :L]
    return o, lse


def dilated_attention(q, k, v, slopes):
    bsz, L, H, hd = q.shape
    outs, lses = [], []
    for window, d in B_DILATIONS:
        Ld = L // d

        def strided(t):
            return t.reshape(bsz, Ld, d, H, hd).transpose(0, 2, 1, 3, 4).reshape(bsz * d, Ld, H, hd)

        o, lse = banded_attention(strided(q)[:, :, :, None], strided(k), strided(v),
                                  (window // 2) // d, slopes[:, None], float(d))
        outs.append(o.reshape(bsz, d, Ld, H, hd).transpose(0, 2, 1, 3, 4).reshape(bsz, L, H, hd))
        lses.append(lse.reshape(bsz, d, Ld, H).transpose(0, 2, 1, 3).reshape(bsz, L, H))
    w = jax.nn.softmax(jnp.stack(lses, 0), axis=0)
    return jnp.einsum('cblh,cblhd->blhd', w, jnp.stack(outs, 0))


def mixer(x, w_in, sink, norm_a, norm_b, w_out):
    bsz, L, _ = x.shape
    proj = x @ w_in
    cuts = [A_WIDTH, A_WIDTH + A_KV_WIDTH, A_WIDTH + 2 * A_KV_WIDTH,
            A_WIDTH + 2 * A_KV_WIDTH + B_WIDTH, A_WIDTH + 2 * A_KV_WIDTH + 2 * B_WIDTH]
    qa, ka, va, qb, kb, vb = jnp.split(proj, cuts, axis=-1)
    slopes = alibi_slopes()
    oa, _ = banded_attention(qa.reshape(bsz, L, A_KV_HEADS, A_GROUP, HEAD_DIM),
                             ka.reshape(bsz, L, A_KV_HEADS, HEAD_DIM),
                             va.reshape(bsz, L, A_KV_HEADS, HEAD_DIM),
                             A_RADIUS, slopes[0::2].reshape(A_KV_HEADS, A_GROUP), 1.0,
                             sink.reshape(A_KV_HEADS, A_GROUP))
    ob = dilated_attention(qb.reshape(bsz, L, B_HEADS, HEAD_DIM), kb.reshape(bsz, L, B_HEADS, HEAD_DIM),
                           vb.reshape(bsz, L, B_HEADS, HEAD_DIM), slopes[1::2])
    oa = rms_norm(oa.reshape(bsz, L, A_WIDTH).astype(x.dtype), norm_a)
    ob = rms_norm(ob.reshape(bsz, L, B_WIDTH).astype(x.dtype), norm_b)
    return jnp.concatenate([oa, ob], axis=-1) @ w_out


def moe(x, w_router, router_bias, w_gate, w_up, w_down, ws_gate, ws_up, ws_down):
    shp = x.shape
    xt = x.reshape(-1, shp[-1])
    scores = jax.nn.sigmoid((xt @ w_router).astype(jnp.float32))
    biased = scores + router_bias.astype(jnp.float32)
    group_score = lax.top_k(biased.reshape(-1, N_GROUPS, EXPERTS_PER_GROUP), TOP_K)[0].sum(-1)
    sel_group = jnp.argmax(group_score, axis=-1)
    in_group = (jnp.arange(N_EXPERTS) // EXPERTS_PER_GROUP)[None, :] == sel_group[:, None]
    _, idx = lax.top_k(jnp.where(in_group, biased, -jnp.inf), TOP_K)
    sel = jnp.take_along_axis(scores, idx, axis=-1)
    sel = sel / sel.sum(-1, keepdims=True)
    gates = (jax.nn.one_hot(idx, N_EXPERTS, dtype=jnp.float32) * sel[..., None]).sum(1)
    h = jax.nn.silu(jnp.einsum('nd,edf->nef', xt, w_gate)) * jnp.einsum('nd,edf->nef', xt, w_up)
    h = h * gates[..., None].astype(h.dtype)
    y = jnp.einsum('nef,efd->nd', h, w_down)
    shared = (jax.nn.silu(xt @ ws_gate) * (xt @ ws_up)) @ ws_down
    return (y + shared).reshape(shp)


def trunk(x, w_in, sink_a, norm_a, norm_b, w_out, ln1_g, ln1_b, w_router, router_bias,
          w_gate, w_up, w_down, ws_gate, ws_up, ws_down, ln2_g, ln2_b):
    for l in range(DEPTH):
        x = layer_norm(DEEPNORM_ALPHA * x + mixer(x, w_in[l], sink_a[l], norm_a[l], norm_b[l], w_out[l]),
                       ln1_g[l], ln1_b[l])
        x = layer_norm(DEEPNORM_ALPHA * x + moe(x, w_router, router_bias, w_gate[l], w_up[l], w_down[l],
                                                 ws_gate[l], ws_up[l], ws_down[l]),
                       ln2_g[l], ln2_b[l])
    return x


def setup_inputs(seed: int = 0) -> dict:
    key = jax.random.key(seed)
    ks = jax.random.split(key, 22)
    nrm = jax.random.normal
    f32 = jnp.float32
    col_scale = jnp.concatenate([
        jnp.ones((A_WIDTH + A_KV_WIDTH,), f32),
        jnp.full((A_KV_WIDTH,), DEEPNORM_BETA, f32),
        jnp.ones((2 * B_WIDTH,), f32),
        jnp.full((B_WIDTH,), DEEPNORM_BETA, f32)])
    return {
        "x_prompt": nrm(ks[0], (BATCH, SEQ, D_MODEL), f32),
        "x_sample": nrm(ks[1], (DEC_BATCH, DEC_SEQ, D_MODEL), f32),
        "w_in": nrm(ks[2], (DEPTH, D_MODEL, IN_WIDTH), f32) * D_MODEL ** -0.5 * col_scale,
        "sink_a": nrm(ks[3], (DEPTH, A_HEADS), f32),
        "norm_a": 1.0 + 0.02 * nrm(ks[4], (DEPTH, A_WIDTH), f32),
        "norm_b": 1.0 + 0.02 * nrm(ks[5], (DEPTH, B_WIDTH), f32),
        "w_out": nrm(ks[6], (DEPTH, MIX_WIDTH, D_MODEL), f32) * MIX_WIDTH ** -0.5 * DEEPNORM_BETA,
        "ln1_g": 1.0 + 0.02 * nrm(ks[7], (DEPTH, D_MODEL), f32),
        "ln1_b": 0.02 * nrm(ks[8], (DEPTH, D_MODEL), f32),
        "w_router": nrm(ks[9], (D_MODEL, N_EXPERTS), f32) * D_MODEL ** -0.5,
        "router_bias": 0.01 * nrm(ks[10], (N_EXPERTS,), f32),
        "w_gate": nrm(ks[11], (DEPTH, N_EXPERTS, D_MODEL, D_EXPERT), f32) * D_MODEL ** -0.5,
        "w_up": nrm(ks[12], (DEPTH, N_EXPERTS, D_MODEL, D_EXPERT), f32) * D_MODEL ** -0.5,
        "w_down": nrm(ks[13], (DEPTH, N_EXPERTS, D_EXPERT, D_MODEL), f32) * D_EXPERT ** -0.5 * DEEPNORM_BETA,
        "ws_gate": nrm(ks[14], (DEPTH, D_MODEL, D_SHARED), f32) * D_MODEL ** -0.5,
        "ws_up": nrm(ks[15], (DEPTH, D_MODEL, D_SHARED), f32) * D_MODEL ** -0.5,
        "ws_down": nrm(ks[16], (DEPTH, D_SHARED, D_MODEL), f32) * D_SHARED ** -0.5 * DEEPNORM_BETA,
        "ln2_g": 1.0 + 0.02 * nrm(ks[17], (DEPTH, D_MODEL), f32),
        "ln2_b": 0.02 * nrm(ks[18], (DEPTH, D_MODEL), f32),
    }


def reference(x_prompt, x_sample, w_in, sink_a, norm_a, norm_b, w_out, ln1_g, ln1_b, w_router, router_bias,
              w_gate, w_up, w_down, ws_gate, ws_up, ws_down, ln2_g, ln2_b):
    y_prompt = trunk(x_prompt, w_in, sink_a, norm_a, norm_b, w_out, ln1_g, ln1_b, w_router, router_bias,
                     w_gate, w_up, w_down, ws_gate, ws_up, ws_down, ln2_g, ln2_b)
    y_sample = trunk(x_sample, w_in, sink_a, norm_a, norm_b, w_out, ln1_g, ln1_b, w_router, router_bias,
                     w_gate, w_up, w_down, ws_gate, ws_up, ws_down, ln2_g, ln2_b)
    return (y_prompt, y_sample)
```

```python
import functools

import jax
import jax.numpy as jnp
from jax import lax
from jax.experimental import pallas as pl
from jax.experimental.pallas import tpu as pltpu

F32 = jnp.float32
BF16 = jnp.bfloat16

D_MODEL = 1024
DEPTH = 4
HEAD_DIM = 64
A_HEADS = 8
A_KV_HEADS = 2
A_GROUP = A_HEADS // A_KV_HEADS
A_RADIUS = 128
B_HEADS = 8
B_DILATIONS = ((128, 1), (512, 4), (2048, 16))
A_WIDTH = A_HEADS * HEAD_DIM
A_KV_WIDTH = A_KV_HEADS * HEAD_DIM
B_WIDTH = B_HEADS * HEAD_DIM
IN_WIDTH = A_WIDTH + 2 * A_KV_WIDTH + 3 * B_WIDTH
N_EXPERTS = 16
N_GROUPS = 4
EXPERTS_PER_GROUP = N_EXPERTS // N_GROUPS
D_EXPERT = 256
DEEPNORM_ALPHA = (2 * DEPTH) ** 0.25
LN_EPS = 1e-5
RMS_EPS = 1e-6

LANES = 128
MASK_PENALTY = 1e30
VMEM_LIMIT = 48 * 1024 * 1024

Q_TILE = 128
Q_BLOCK = 512
PROJ_TILE = 512
OUT_TILE = 512
MOE_TILE = 1024


def _params(*sem):
    return pltpu.CompilerParams(dimension_semantics=sem, vmem_limit_bytes=VMEM_LIMIT)


def _inproj_kernel(x_ref, w_ref, qa_ref, ka_ref, va_ref, qb_ref, kb_ref, vb_ref):
    x = x_ref[...].astype(BF16)

    def proj(lo, width):
        return jnp.dot(x, w_ref[:, lo:lo + width], preferred_element_type=F32)

    scale = HEAD_DIM ** -0.5
    lo = 0
    qa_ref[...] = (proj(lo, A_WIDTH) * scale).astype(BF16)
    lo += A_WIDTH
    ka_ref[...] = proj(lo, A_KV_WIDTH).astype(BF16)
    lo += A_KV_WIDTH
    va_ref[...] = proj(lo, A_KV_WIDTH).astype(BF16)
    lo += A_KV_WIDTH
    qb_ref[...] = (proj(lo, B_WIDTH) * scale).astype(BF16)
    lo += B_WIDTH
    kb_ref[...] = proj(lo, B_WIDTH).astype(BF16)
    lo += B_WIDTH
    vb_ref[...] = proj(lo, B_WIDTH).astype(BF16)


def _inproj(x, w_in):
    n = x.shape[0]
    widths = (A_WIDTH, A_KV_WIDTH, A_KV_WIDTH, B_WIDTH, B_WIDTH, B_WIDTH)
    return pl.pallas_call(
        _inproj_kernel,
        grid=(n // PROJ_TILE,),
        in_specs=[pl.BlockSpec((PROJ_TILE, D_MODEL), lambda i: (i, 0)),
                  pl.BlockSpec((D_MODEL, IN_WIDTH), lambda i: (0, 0))],
        out_specs=[pl.BlockSpec((PROJ_TILE, w), lambda i: (i, 0)) for w in widths],
        out_shape=[jax.ShapeDtypeStruct((n, w), BF16) for w in widths],
        compiler_params=_params("parallel"),
        name="inproj",
    )(x, w_in)


def _band_penalty(q0, ks, rows, width, radius, dist_scale):
    qi = lax.broadcasted_iota(jnp.int32, (rows, width), 0)
    kj = lax.broadcasted_iota(jnp.int32, (rows, width), 1)
    dist = jnp.abs(kj - qi + (ks - q0))
    return jnp.where(dist <= radius, dist.astype(F32) * dist_scale, MASK_PENALTY)


def _attn_a_kernel(hp_ref, norm_ref, q_ref, k_ref, v_ref, o_ref, acc_ref, *, seq_len):
    width = Q_TILE + 2 * A_RADIUS
    blk = pl.program_id(1)

    def tile(t, carry):
        r0 = pl.multiple_of(t * Q_TILE, Q_TILE)
        q0 = blk * Q_BLOCK + r0
        ks = pl.multiple_of(jnp.clip(q0 - A_RADIUS, 0, seq_len - width), Q_TILE)
        pen = _band_penalty(q0, ks, Q_TILE, width, A_RADIUS, 1.0)
        q = q_ref[0, pl.ds(r0, Q_TILE), :]
        kwin = k_ref[0, pl.ds(ks, width), :]
        vwin = v_ref[0, pl.ds(ks, width), :]
        for h in range(A_HEADS):
            g = h // A_GROUP
            qh = q[:, h * HEAD_DIM:(h + 1) * HEAD_DIM]
            kh = kwin[:, g * HEAD_DIM:(g + 1) * HEAD_DIM]
            vh = vwin[:, g * HEAD_DIM:(g + 1) * HEAD_DIM]
            s = lax.dot_general(qh, kh, (((1,), (1,)), ((), ())), preferred_element_type=F32)
            s = s - hp_ref[0, h] * pen
            sink = hp_ref[1, h]
            m = jnp.maximum(jnp.max(s, axis=-1, keepdims=True), sink)
            p = jnp.exp(s - m)
            denom = jnp.sum(p, axis=-1, keepdims=True) + jnp.exp(sink - m)
            o = jnp.dot(p.astype(BF16), vh, preferred_element_type=F32) / denom
            acc_ref[:, h * HEAD_DIM:(h + 1) * HEAD_DIM] = o
        o = acc_ref[...]
        ms = jnp.mean(o * o, axis=-1, keepdims=True)
        o_ref[0, pl.ds(r0, Q_TILE), :] = (o * lax.rsqrt(ms + RMS_EPS) * norm_ref[...]).astype(BF16)
        return carry

    lax.fori_loop(0, Q_BLOCK // Q_TILE, tile, 0)


def _attn_a(qa, ka, va, head_params, norm_a):
    bsz, seq_len, _ = qa.shape
    return pl.pallas_call(
        functools.partial(_attn_a_kernel, seq_len=seq_len),
        grid=(bsz, seq_len // Q_BLOCK),
        in_specs=[pl.BlockSpec(memory_space=pltpu.SMEM),
                  pl.BlockSpec((1, A_WIDTH), lambda b, i: (0, 0)),
                  pl.BlockSpec((1, Q_BLOCK, A_WIDTH), lambda b, i: (b, i, 0)),
                  pl.BlockSpec((1, seq_len, A_KV_WIDTH), lambda b, i: (b, 0, 0)),
                  pl.BlockSpec((1, seq_len, A_KV_WIDTH), lambda b, i: (b, 0, 0))],
        out_specs=pl.BlockSpec((1, Q_BLOCK, A_WIDTH), lambda b, i: (b, i, 0)),
        out_shape=jax.ShapeDtypeStruct((bsz, seq_len, A_WIDTH), BF16),
        scratch_shapes=[pltpu.VMEM((Q_TILE, A_WIDTH), F32)],
        compiler_params=_params("parallel", "arbitrary"),
        name="attn_window",
    )(head_params, norm_a, qa, ka, va)


def _attn_b_kernel(hp_ref, q_ref, k_ref, v_ref, o_ref, lse_ref, *, seq_len, q_block, radius, dist_scale):
    width = min(Q_TILE + 2 * radius, seq_len)
    blk = pl.program_id(2)
    lane = lax.broadcasted_iota(jnp.int32, (Q_TILE, LANES), 1)

    def tile(t, carry):
        r0 = pl.multiple_of(t * Q_TILE, Q_TILE)
        q0 = blk * q_block + r0
        ks = pl.multiple_of(jnp.clip(q0 - radius, 0, seq_len - width), radius)
        pen = _band_penalty(q0, ks, Q_TILE, width, radius, dist_scale)
        q = q_ref[0, pl.ds(r0, Q_TILE), :]
        kwin = k_ref[0, pl.ds(ks, width), :]
        vwin = v_ref[0, pl.ds(ks, width), :]
        lse_tile = jnp.zeros((Q_TILE, LANES), F32)
        for h in range(B_HEADS):
            sl = slice(h * HEAD_DIM, (h + 1) * HEAD_DIM)
            s = lax.dot_general(q[:, sl], kwin[:, sl], (((1,), (1,)), ((), ())),
                                preferred_element_type=F32)
            s = s - hp_ref[0, h] * pen
            m = jnp.max(s, axis=-1, keepdims=True)
            p = jnp.exp(s - m)
            denom = jnp.sum(p, axis=-1, keepdims=True)
            o = jnp.dot(p.astype(BF16), vwin[:, sl], preferred_element_type=F32) / denom
            o_ref[0, pl.ds(r0, Q_TILE), sl] = o
            lse_tile = jnp.where(lane == h, m + jnp.log(denom), lse_tile)
        lse_ref[0, pl.ds(r0, Q_TILE), :] = lse_tile
        return carry

    lax.fori_loop(0, q_block // Q_TILE, tile, 0)


def _attn_b_branch(qb, kb, vb, head_params, window, dilation):
    bsz, full_len, _ = qb.shape
    seq_len = full_len // dilation
    radius = (window // 2) // dilation
    q_block = min(Q_BLOCK, seq_len)
    view = lambda t: t.reshape(bsz, seq_len, dilation * B_WIDTH)
    kv_spec = pl.BlockSpec((1, seq_len, B_WIDTH), lambda b, r, i: (b, 0, r))
    o, lse = pl.pallas_call(
        functools.partial(_attn_b_kernel, seq_len=seq_len, q_block=q_block, radius=radius,
                          dist_scale=float(dilation)),
        grid=(bsz, dilation, seq_len // q_block),
        in_specs=[pl.BlockSpec(memory_space=pltpu.SMEM),
                  pl.BlockSpec((1, q_block, B_WIDTH), lambda b, r, i: (b, i, r)),
                  kv_spec, kv_spec],
        out_specs=[pl.BlockSpec((1, q_block, B_WIDTH), lambda b, r, i: (b, i, r)),
                   pl.BlockSpec((1, q_block, LANES), lambda b, r, i: (b, i, r))],
        out_shape=[jax.ShapeDtypeStruct((bsz, seq_len, dilation * B_WIDTH), F32),
                   jax.ShapeDtypeStruct((bsz, seq_len, dilation * LANES), F32)],
        compiler_params=_params("parallel", "parallel", "arbitrary"),
        name=f"attn_dilated_{dilation}",
    )(head_params, view(qb), view(kb), view(vb))
    return o.reshape(bsz, full_len, B_WIDTH), lse.reshape(bsz, full_len, LANES)


def _layer_norm(v, g, b):
    mu = jnp.mean(v, axis=-1, keepdims=True)
    c = v - mu
    var = jnp.mean(c * c, axis=-1, keepdims=True)
    return c * lax.rsqrt(var + LN_EPS) * g + b


def _mix_out_kernel(x_ref, oa_ref, ob1_ref, ob2_ref, ob3_ref, l1_ref, l2_ref, l3_ref,
                    expand_ref, norm_b_ref, w_ref, g_ref, b_ref, y_ref):
    lses = (l1_ref[...], l2_ref[...], l3_ref[...])
    top = jnp.maximum(jnp.maximum(lses[0], lses[1]), lses[2])
    es = [jnp.exp(l - top) for l in lses]
    inv = 1.0 / (es[0] + es[1] + es[2])
    ob = jnp.zeros(ob1_ref.shape, F32)
    for e, o_ref in zip(es, (ob1_ref, ob2_ref, ob3_ref)):
        wgt = e * inv
        hi = wgt.astype(BF16)
        lo = (wgt - hi.astype(F32)).astype(BF16)
        wide = (jnp.dot(hi, expand_ref[...], preferred_element_type=F32)
                + jnp.dot(lo, expand_ref[...], preferred_element_type=F32))
        ob = ob + wide * o_ref[...]
    ms = jnp.mean(ob * ob, axis=-1, keepdims=True)
    ob = (ob * lax.rsqrt(ms + RMS_EPS) * norm_b_ref[...]).astype(BF16)
    mix = (jnp.dot(oa_ref[...], w_ref[:A_WIDTH, :], preferred_element_type=F32)
           + jnp.dot(ob, w_ref[A_WIDTH:, :], preferred_element_type=F32))
    y_ref[...] = _layer_norm(DEEPNORM_ALPHA * x_ref[...] + mix, g_ref[...], b_ref[...])


def _mix_out(x, oa, obs, lses, expand, norm_b, w_out, ln_g, ln_b):
    n = x.shape[0]
    row = lambda w: pl.BlockSpec((OUT_TILE, w), lambda i: (i, 0))
    full = lambda r, c: pl.BlockSpec((r, c), lambda i: (0, 0))
    return pl.pallas_call(
        _mix_out_kernel,
        grid=(n // OUT_TILE,),
        in_specs=[row(D_MODEL), row(A_WIDTH), row(B_WIDTH), row(B_WIDTH), row(B_WIDTH),
                  row(LANES), row(LANES), row(LANES),
                  full(LANES, B_WIDTH), full(1, B_WIDTH), full(D_MODEL, D_MODEL),
                  full(1, D_MODEL), full(1, D_MODEL)],
        out_specs=row(D_MODEL),
        out_shape=jax.ShapeDtypeStruct((n, D_MODEL), F32),
        compiler_params=_params("parallel"),
        name="mix_out",
    )(x, oa, *obs, *lses, expand, norm_b, w_out, ln_g, ln_b)


def _silu(v):
    return v / (1.0 + jnp.exp(-v))


def _route(scores, biased):
    def top2_sum(c):
        hi01, lo01 = jnp.maximum(c[0], c[1]), jnp.minimum(c[0], c[1])
        hi23, lo23 = jnp.maximum(c[2], c[3]), jnp.minimum(c[2], c[3])
        first = jnp.maximum(hi01, hi23)
        second = jnp.maximum(jnp.minimum(hi01, hi23), jnp.where(hi01 >= hi23, lo01, lo23))
        return first + second

    groups = [biased[g * EXPERTS_PER_GROUP:(g + 1) * EXPERTS_PER_GROUP] for g in range(N_GROUPS)]
    gscore = [top2_sum(c) for c in groups]
    sel = jnp.zeros_like(gscore[0], dtype=jnp.int32)
    best = gscore[0]
    for g in range(1, N_GROUPS):
        better = gscore[g] > best
        sel = jnp.where(better, g, sel)
        best = jnp.where(better, gscore[g], best)

    def pick(cols):
        out = []
        for j in range(EXPERTS_PER_GROUP):
            v = cols[j]
            for g in range(1, N_GROUPS):
                v = jnp.where(sel == g, cols[g * EXPERTS_PER_GROUP + j], v)
            out.append(v)
        return out

    cb, cs = pick(biased), pick(scores)
    i1 = jnp.zeros_like(sel)
    v1 = cb[0]
    for j in range(1, EXPERTS_PER_GROUP):
        better = cb[j] > v1
        i1 = jnp.where(better, j, i1)
        v1 = jnp.where(better, cb[j], v1)
    i2 = jnp.where(i1 == 0, 1, 0)
    v2 = jnp.where(i1 == 0, cb[1], cb[0])
    for j in range(1, EXPERTS_PER_GROUP):
        better = (cb[j] > v2) & (i1 != j) & (i2 != j)
        i2 = jnp.where(better, j, i2)
        v2 = jnp.where(better, cb[j], v2)
    s1 = sum(jnp.where(i1 == j, cs[j], 0.0) for j in range(EXPERTS_PER_GROUP))
    s2 = sum(jnp.where(i2 == j, cs[j], 0.0) for j in range(EXPERTS_PER_GROUP))
    total = s1 + s2
    gates = []
    for e in range(N_EXPERTS):
        g, j = divmod(e, EXPERTS_PER_GROUP)
        in_group = sel == g
        gate = jnp.where(in_group & (i1 == j), s1 / total, 0.0) + jnp.where(in_group & (i2 == j), s2 / total, 0.0)
        gates.append(gate)
    return gates


def _moe_kernel(x_ref, wr_ref, rb_ref, wg_ref, wu_ref, wd_ref, sg_ref, su_ref, sd_ref,
                g_ref, b_ref, y_ref, xb_ref, gate_ref, acc_ref):
    e = pl.program_id(1)

    @pl.when(e == 0)
    def _():
        x = x_ref[...]
        xb = x.astype(BF16)
        xb_ref[...] = xb
        x_lo = (x - xb.astype(F32)).astype(BF16)
        w = wr_ref[...]
        w_hi = w.astype(BF16)
        w_lo = (w - w_hi.astype(F32)).astype(BF16)
        logits = (jnp.dot(xb, w_hi, preferred_element_type=F32)
                  + jnp.dot(xb, w_lo, preferred_element_type=F32)
                  + jnp.dot(x_lo, w_hi, preferred_element_type=F32))
        scores = 1.0 / (1.0 + jnp.exp(-logits))
        biased = scores + rb_ref[...]
        cols = lambda t: [t[:, j:j + 1] for j in range(N_EXPERTS)]
        gates = _route(cols(scores), cols(biased))
        lane = lax.broadcasted_iota(jnp.int32, (x.shape[0], LANES), 1)
        gate_tile = jnp.zeros((x.shape[0], LANES), F32)
        for j, gcol in enumerate(gates):
            gate_tile = jnp.where(lane == j, gcol, gate_tile)
        gate_ref[...] = gate_tile
        hs = _silu(jnp.dot(xb, sg_ref[...], preferred_element_type=F32)) * jnp.dot(
            xb, su_ref[...], preferred_element_type=F32)
        acc_ref[...] = jnp.dot(hs.astype(BF16), sd_ref[...], preferred_element_type=F32)

    xb = xb_ref[...]
    lane = lax.broadcasted_iota(jnp.int32, gate_ref.shape, 1)
    gate = jnp.sum(jnp.where(lane == e, gate_ref[...], 0.0), axis=-1, keepdims=True)
    h = _silu(jnp.dot(xb, wg_ref[0], preferred_element_type=F32)) * jnp.dot(
        xb, wu_ref[0], preferred_element_type=F32)
    h = (h * gate).astype(BF16)
    acc_ref[...] += jnp.dot(h, wd_ref[0], preferred_element_type=F32)

    @pl.when(e == N_EXPERTS - 1)
    def _():
        y_ref[...] = _layer_norm(DEEPNORM_ALPHA * x_ref[...] + acc_ref[...], g_ref[...], b_ref[...])


def _moe(x, w_router, router_bias, w_gate, w_up, w_down, ws_gate, ws_up, ws_down, ln_g, ln_b):
    n = x.shape[0]
    row = pl.BlockSpec((MOE_TILE, D_MODEL), lambda i, e: (i, 0))
    full = lambda r, c: pl.BlockSpec((r, c), lambda i, e: (0, 0))
    return pl.pallas_call(
        _moe_kernel,
        grid=(n // MOE_TILE, N_EXPERTS),
        in_specs=[row, full(D_MODEL, N_EXPERTS), full(1, N_EXPERTS),
                  pl.BlockSpec((1, D_MODEL, D_EXPERT), lambda i, e: (e, 0, 0)),
                  pl.BlockSpec((1, D_MODEL, D_EXPERT), lambda i, e: (e, 0, 0)),
                  pl.BlockSpec((1, D_EXPERT, D_MODEL), lambda i, e: (e, 0, 0)),
                  full(D_MODEL, D_EXPERT), full(D_MODEL, D_EXPERT), full(D_EXPERT, D_MODEL),
                  full(1, D_MODEL), full(1, D_MODEL)],
        out_specs=row,
        out_shape=jax.ShapeDtypeStruct((n, D_MODEL), F32),
        scratch_shapes=[pltpu.VMEM((MOE_TILE, D_MODEL), BF16),
                        pltpu.VMEM((MOE_TILE, LANES), F32),
                        pltpu.VMEM((MOE_TILE, D_MODEL), F32)],
        compiler_params=_params("parallel", "arbitrary"),
        name="moe",
    )(x, w_router, router_bias, w_gate, w_up, w_down, ws_gate, ws_up, ws_down, ln_g, ln_b)


def _alibi_slopes():
    n_heads = A_HEADS + B_HEADS
    h = jnp.arange(1, n_heads + 1, dtype=F32)
    return jnp.exp2(-8.0 * h / n_heads)


def _layer(x, p, l, shape):
    bsz, seq_len = shape
    tokens = bsz * seq_len
    qa, ka, va, qb, kb, vb = _inproj(x, p["w_in"][l])
    seq = lambda t: t.reshape(bsz, seq_len, t.shape[-1])
    oa = _attn_a(seq(qa), seq(ka), seq(va), p["hp_a"][l], p["norm_a"][l])
    obs, lses = [], []
    for window, dilation in B_DILATIONS:
        o, lse = _attn_b_branch(seq(qb), seq(kb), seq(vb), p["hp_b"], window, dilation)
        obs.append(o.reshape(tokens, B_WIDTH))
        lses.append(lse.reshape(tokens, LANES))
    x = _mix_out(x, oa.reshape(tokens, A_WIDTH), obs, lses, p["expand"], p["norm_b"][l],
                 p["w_out"][l], p["ln1_g"][l], p["ln1_b"][l])
    return _moe(x, p["w_router"], p["router_bias"], p["w_gate"][l], p["w_up"][l], p["w_down"][l],
                p["ws_gate"][l], p["ws_up"][l], p["ws_down"][l], p["ln2_g"][l], p["ln2_b"][l])


def _trunk(x, p):
    bsz, seq_len, _ = x.shape
    h = x.reshape(bsz * seq_len, D_MODEL)
    for l in range(DEPTH):
        h = _layer(h, p, l, (bsz, seq_len))
    return h.reshape(bsz, seq_len, D_MODEL)


def _prepare(w_in, sink_a, norm_a, norm_b, w_out, ln1_g, ln1_b, w_router, router_bias,
             w_gate, w_up, w_down, ws_gate, ws_up, ws_down, ln2_g, ln2_b):
    slopes = _alibi_slopes()
    slopes_a = jnp.broadcast_to(slopes[0::2], (DEPTH, A_HEADS))
    head_lane = jnp.arange(B_WIDTH)[None, :] // HEAD_DIM
    p = {
        "w_in": w_in.astype(BF16),
        "hp_a": jnp.stack([slopes_a, sink_a.astype(F32)], axis=1),
        "hp_b": jnp.stack([slopes[1::2], jnp.zeros((B_HEADS,), F32)], axis=0),
        "expand": (jnp.arange(LANES)[:, None] == head_lane).astype(BF16),
        "norm_a": norm_a.reshape(DEPTH, 1, A_WIDTH),
        "norm_b": norm_b.reshape(DEPTH, 1, B_WIDTH),
        "w_out": w_out.astype(BF16),
        "ln1_g": ln1_g.reshape(DEPTH, 1, D_MODEL), "ln1_b": ln1_b.reshape(DEPTH, 1, D_MODEL),
        "w_router": w_router, "router_bias": router_bias.reshape(1, N_EXPERTS),
        "w_gate": w_gate.astype(BF16), "w_up": w_up.astype(BF16), "w_down": w_down.astype(BF16),
        "ws_gate": ws_gate.astype(BF16), "ws_up": ws_up.astype(BF16), "ws_down": ws_down.astype(BF16),
        "ln2_g": ln2_g.reshape(DEPTH, 1, D_MODEL), "ln2_b": ln2_b.reshape(DEPTH, 1, D_MODEL),
    }
    return p


def kernel(x_prompt, x_sample, w_in, sink_a, norm_a, norm_b, w_out, ln1_g, ln1_b, w_router, router_bias,
           w_gate, w_up, w_down, ws_gate, ws_up, ws_down, ln2_g, ln2_b):
    p = _prepare(w_in, sink_a, norm_a, norm_b, w_out, ln1_g, ln1_b, w_router, router_bias,
                 w_gate, w_up, w_down, ws_gate, ws_up, ws_down, ln2_g, ln2_b)
    return _trunk(x_prompt, p), _trunk(x_sample, p)
```

```python
import functools

import jax
import jax.numpy as jnp
from jax import lax
from jax.experimental import pallas as pl
from jax.experimental.pallas import tpu as pltpu

F32 = jnp.float32
BF16 = jnp.bfloat16

D_MODEL = 1024
DEPTH = 4
HEAD_DIM = 64
A_HEADS = 8
A_KV_HEADS = 2
A_RADIUS = 128
B_HEADS = 8
B_DILATIONS = ((128, 1), (512, 4), (2048, 16))
A_WIDTH = A_HEADS * HEAD_DIM
A_KV_WIDTH = A_KV_HEADS * HEAD_DIM
B_WIDTH = B_HEADS * HEAD_DIM
N_EXPERTS = 16
N_GROUPS = 4
EXPERTS_PER_GROUP = N_EXPERTS // N_GROUPS
D_EXPERT = 256
DEEPNORM_ALPHA = (2 * DEPTH) ** 0.25
LN_EPS = 1e-5
RMS_EPS = 1e-6
LOG2E = 1.4426950408889634

LANES = 128
MASK_PENALTY = 1e30
VMEM_LIMIT = 48 * 1024 * 1024

Q_TILE = 128
Q_BLOCK = 512
TILE_UNROLL = 4
PROJ_TILE = 512
OUT_TILE = 512
MOE_TILE = 1024
STRIDED_DILATIONS = tuple(d for _, d in B_DILATIONS if d > 1)
A_KV2_WIDTH = 2 * A_KV_WIDTH
PROJ_WIDTH = A_WIDTH + 2 * A_KV2_WIDTH + 3 * B_WIDTH


def _params(*sem):
    return pltpu.CompilerParams(dimension_semantics=sem, vmem_limit_bytes=VMEM_LIMIT)


def _dot_nt(a, b):
    return lax.dot_general(a, b, (((1,), (1,)), ((), ())), preferred_element_type=F32)


def _inproj_kernel(x_ref, w_ref, qa_ref, ka_ref, va_ref, *rest):
    b_refs, stage_ref = rest[:-1], rest[-1]
    x = x_ref[...].astype(BF16)
    q_scale = HEAD_DIM ** -0.5 * LOG2E

    def proj(lo, width):
        return jnp.dot(x, w_ref[:, lo:lo + width], preferred_element_type=F32)

    qa_ref[...] = (proj(0, A_WIDTH) * q_scale).astype(BF16)
    ka_ref[...] = proj(A_WIDTH, A_KV2_WIDTH).astype(BF16)
    va_ref[...] = proj(A_WIDTH + A_KV2_WIDTH, A_KV2_WIDTH).astype(BF16)
    lo = A_WIDTH + 2 * A_KV2_WIDTH
    n_layouts = 1 + len(STRIDED_DILATIONS)
    for j in range(3):
        r = proj(lo + j * B_WIDTH, B_WIDTH)
        if j == 0:
            r = r * q_scale
        outs = b_refs[j * n_layouts:(j + 1) * n_layouts]
        outs[0][...] = r.astype(BF16)
        for c in range(B_WIDTH // LANES):
            stage_ref[c] = r[:, c * LANES:(c + 1) * LANES]
        for d, out in zip(STRIDED_DILATIONS, outs[1:]):
            rows = PROJ_TILE // d
            for res in range(d):
                for c in range(B_WIDTH // LANES):
                    lo_c = res * B_WIDTH + c * LANES
                    out[:, lo_c:lo_c + LANES] = stage_ref[c, pl.ds(res, rows, stride=d), :].astype(BF16)


def _inproj(x, w_in):
    n = x.shape[0]
    shapes = [(n, A_WIDTH), (n, A_KV2_WIDTH), (n, A_KV2_WIDTH)]
    blocks = [(PROJ_TILE, A_WIDTH), (PROJ_TILE, A_KV2_WIDTH), (PROJ_TILE, A_KV2_WIDTH)]
    for _ in range(3):
        for d in (1,) + STRIDED_DILATIONS:
            shapes.append((n // d, d * B_WIDTH))
            blocks.append((PROJ_TILE // d, d * B_WIDTH))
    outs = pl.pallas_call(
        _inproj_kernel,
        grid=(n // PROJ_TILE,),
        in_specs=[pl.BlockSpec((PROJ_TILE, D_MODEL), lambda i: (i, 0)),
                  pl.BlockSpec((D_MODEL, PROJ_WIDTH), lambda i: (0, 0))],
        out_specs=[pl.BlockSpec(b, lambda i: (i, 0)) for b in blocks],
        out_shape=[jax.ShapeDtypeStruct(s, BF16) for s in shapes],
        scratch_shapes=[pltpu.VMEM((B_WIDTH // LANES, PROJ_TILE, LANES), F32)],
        compiler_params=_params("parallel"),
        name="inproj",
    )(x, w_in)
    n_layouts = 1 + len(STRIDED_DILATIONS)
    qa, ka, va = outs[:3]
    qb, kb, vb = (outs[3 + j * n_layouts:3 + (j + 1) * n_layouts] for j in range(3))
    return qa, ka, va, qb, kb, vb


def _band_penalty(q0, ks, rows, width, radius, dist_scale):
    qi = lax.broadcasted_iota(jnp.int32, (rows, width), 0)
    kj = lax.broadcasted_iota(jnp.int32, (rows, width), 1)
    dist = jnp.abs(kj - qi + (ks - q0))
    return jnp.where(dist <= radius, dist.astype(F32) * dist_scale, MASK_PENALTY)


def _attn_a_kernel(hp_ref, mask_ref, norm_ref, q_ref, k_ref, v_ref, o_ref, acc_ref, *, seq_len):
    width = Q_TILE + 2 * A_RADIUS
    blk = pl.program_id(1)
    lane = lax.broadcasted_iota(jnp.int32, (Q_TILE, LANES), 1)

    def tile(t, carry):
        r0 = pl.multiple_of(t * Q_TILE, Q_TILE)
        q0 = blk * Q_BLOCK + r0
        ks = pl.multiple_of(jnp.clip(q0 - A_RADIUS, 0, seq_len - width), Q_TILE)
        pen = _band_penalty(q0, ks, Q_TILE, width, A_RADIUS, LOG2E)
        for g in range(A_KV_HEADS):
            heads = [(c, half) for c in (2 * g, 2 * g + 1) for half in range(2)]
            q4 = jnp.concatenate([q_ref[0, pl.ds(r0, Q_TILE), c * LANES:(c + 1) * LANES] * mask_ref[half:half + 1, :]
                                  for c, half in heads], axis=0)
            kg = k_ref[0, pl.ds(ks, width), g * LANES:(g + 1) * LANES]
            vg = v_ref[0, pl.ds(ks, width), g * LANES:(g + 1) * LANES]
            s4 = _dot_nt(q4, kg)
            probs, inv = [], []
            for j, (c, half) in enumerate(heads):
                h = 2 * c + half
                sb = s4[j * Q_TILE:(j + 1) * Q_TILE] - hp_ref[0, h] * pen
                sink = hp_ref[1, h]
                m = jnp.maximum(jnp.max(sb, axis=-1, keepdims=True), sink)
                p = jnp.exp2(sb - m)
                denom = jnp.sum(p, axis=-1, keepdims=True) + jnp.exp2(sink - m)
                probs.append(p.astype(BF16))
                inv.append(1.0 / denom)
            o4 = jnp.dot(jnp.concatenate(probs, axis=0), vg, preferred_element_type=F32)
            for jc, c in enumerate((2 * g, 2 * g + 1)):
                lo = o4[(2 * jc) * Q_TILE:(2 * jc + 1) * Q_TILE] * inv[2 * jc]
                hi = o4[(2 * jc + 1) * Q_TILE:(2 * jc + 2) * Q_TILE] * inv[2 * jc + 1]
                acc_ref[:, c * LANES:(c + 1) * LANES] = jnp.where(lane < HEAD_DIM, lo, hi)
        o = acc_ref[...]
        ms = jnp.mean(o * o, axis=-1, keepdims=True)
        o_ref[0, pl.ds(r0, Q_TILE), :] = (o * lax.rsqrt(ms + RMS_EPS) * norm_ref[...]).astype(BF16)
        return carry

    lax.fori_loop(0, Q_BLOCK // Q_TILE, tile, 0, unroll=TILE_UNROLL)


def _attn_a(qa, ka, va, head_params, lane_masks, norm_a):
    bsz, seq_len, _ = qa.shape
    kv_spec = pl.BlockSpec((1, seq_len, A_KV2_WIDTH), lambda b, i: (b, 0, 0))
    return pl.pallas_call(
        functools.partial(_attn_a_kernel, seq_len=seq_len),
        grid=(bsz, seq_len // Q_BLOCK),
        in_specs=[pl.BlockSpec(memory_space=pltpu.SMEM),
                  pl.BlockSpec((2, LANES), lambda b, i: (0, 0)),
                  pl.BlockSpec((1, A_WIDTH), lambda b, i: (0, 0)),
                  pl.BlockSpec((1, Q_BLOCK, A_WIDTH), lambda b, i: (b, i, 0)),
                  kv_spec, kv_spec],
        out_specs=pl.BlockSpec((1, Q_BLOCK, A_WIDTH), lambda b, i: (b, i, 0)),
        out_shape=jax.ShapeDtypeStruct((bsz, seq_len, A_WIDTH), BF16),
        scratch_shapes=[pltpu.VMEM((Q_TILE, A_WIDTH), F32)],
        compiler_params=_params("parallel", "arbitrary"),
        name="attn_window",
    )(head_params, lane_masks, norm_a, qa, ka, va)


def _attn_b_kernel(hp_ref, mask_ref, q_ref, k_ref, v_ref, o_ref, lse_ref, *,
                   seq_len, q_block, res_block, radius, dist_scale):
    width = min(Q_TILE + 2 * radius, seq_len)
    blk = pl.program_id(2)
    lane = lax.broadcasted_iota(jnp.int32, (Q_TILE, LANES), 1)

    for rr in range(res_block):
        base = rr * B_WIDTH

        def tile(t, carry, base=base, rr=rr):
            r0 = pl.multiple_of(t * Q_TILE, Q_TILE)
            q0 = blk * q_block + r0
            ks = pl.multiple_of(jnp.clip(q0 - radius, 0, seq_len - width), radius)
            pen = _band_penalty(q0, ks, Q_TILE, width, radius, dist_scale * LOG2E)
            lse_tile = jnp.zeros((Q_TILE, LANES), F32)
            for c in range(B_WIDTH // LANES):
                cs = slice(base + c * LANES, base + (c + 1) * LANES)
                qc = q_ref[0, pl.ds(r0, Q_TILE), cs]
                kc = k_ref[0, pl.ds(ks, width), cs]
                vc = v_ref[0, pl.ds(ks, width), cs]
                q2 = jnp.concatenate([qc * mask_ref[0:1, :], qc * mask_ref[1:2, :]], axis=0)
                s2 = _dot_nt(q2, kc)
                probs, inv = [], []
                for half in range(2):
                    h = 2 * c + half
                    s = s2[half * Q_TILE:(half + 1) * Q_TILE] - hp_ref[0, h] * pen
                    m = jnp.max(s, axis=-1, keepdims=True)
                    p = jnp.exp2(s - m)
                    denom = jnp.sum(p, axis=-1, keepdims=True)
                    probs.append(p.astype(BF16))
                    inv.append(1.0 / denom)
                    lse_tile = jnp.where(lane == h, m + jnp.log2(denom), lse_tile)
                o2 = jnp.dot(jnp.concatenate(probs, axis=0), vc, preferred_element_type=F32)
                acc = jnp.where(lane < HEAD_DIM, o2[:Q_TILE] * inv[0], o2[Q_TILE:] * inv[1])
                o_ref[0, pl.ds(r0, Q_TILE), cs] = acc.astype(BF16)
            lse_ref[0, pl.ds(r0, Q_TILE), rr * LANES:(rr + 1) * LANES] = lse_tile
            return carry

        lax.fori_loop(0, q_block // Q_TILE, tile, 0, unroll=min(TILE_UNROLL, q_block // Q_TILE))


def _attn_b_branch(qv, kv, vv, head_params, lane_masks, bsz, window, dilation):
    seq_len = qv.shape[0] // bsz
    radius = (window // 2) // dilation
    q_block = min(Q_BLOCK, seq_len)
    res_block = min(dilation, max(1, Q_BLOCK // q_block))
    view = lambda t: t.reshape(bsz, seq_len, t.shape[-1])
    kv_spec = pl.BlockSpec((1, seq_len, res_block * B_WIDTH), lambda b, r, i: (b, 0, r))
    o, lse = pl.pallas_call(
        functools.partial(_attn_b_kernel, seq_len=seq_len, q_block=q_block, res_block=res_block,
                          radius=radius, dist_scale=float(dilation)),
        grid=(bsz, dilation // res_block, seq_len // q_block),
        in_specs=[pl.BlockSpec(memory_space=pltpu.SMEM),
                  pl.BlockSpec((2, LANES), lambda b, r, i: (0, 0)),
                  pl.BlockSpec((1, q_block, res_block * B_WIDTH), lambda b, r, i: (b, i, r)),
                  kv_spec, kv_spec],
        out_specs=[pl.BlockSpec((1, q_block, res_block * B_WIDTH), lambda b, r, i: (b, i, r)),
                   pl.BlockSpec((1, q_block, res_block * LANES), lambda b, r, i: (b, i, r))],
        out_shape=[jax.ShapeDtypeStruct((bsz, seq_len, dilation * B_WIDTH), BF16),
                   jax.ShapeDtypeStruct((bsz, seq_len, dilation * LANES), F32)],
        compiler_params=_params("parallel", "parallel", "arbitrary"),
        name=f"attn_dilated_{dilation}",
    )(head_params, lane_masks, view(qv), view(kv), view(vv))
    return o.reshape(bsz * seq_len, dilation * B_WIDTH), lse.reshape(bsz * seq_len, dilation * LANES)


def _layer_norm(v, g, b):
    mu = jnp.mean(v, axis=-1, keepdims=True)
    c = v - mu
    var = jnp.mean(c * c, axis=-1, keepdims=True)
    return c * lax.rsqrt(var + LN_EPS) * g + b


def _mix_out_kernel(x_ref, oa_ref, *rest):
    n_br = len(B_DILATIONS)
    ob_refs, lse_refs = rest[:n_br], rest[n_br:2 * n_br]
    expand_ref, norm_b_ref, w_ref, g_ref, b_ref, y_ref, ob_stage, lse_stage = rest[2 * n_br:]
    obs, lses = [], []
    for idx, ((_, d), o_ref, l_ref) in enumerate(zip(B_DILATIONS, ob_refs, lse_refs)):
        if d == 1:
            obs.append(o_ref[...].astype(F32))
            lses.append(l_ref[...])
            continue
        rows = OUT_TILE // d
        n_cols = B_WIDTH // LANES
        for res in range(d):
            for c in range(n_cols):
                lo_c = res * B_WIDTH + c * LANES
                ob_stage[idx * n_cols + c, pl.ds(res, rows, stride=d), :] = o_ref[:, lo_c:lo_c + LANES].astype(F32)
            lse_stage[idx, pl.ds(res, rows, stride=d), :] = l_ref[:, res * LANES:(res + 1) * LANES]
        obs.append(jnp.concatenate([ob_stage[idx * n_cols + c] for c in range(n_cols)], axis=-1))
        lses.append(lse_stage[idx])
    top = functools.reduce(jnp.maximum, lses)
    es = [jnp.exp2(l - top) for l in lses]
    inv = 1.0 / functools.reduce(lambda a, b: a + b, es)
    ob = None
    for e, o in zip(es, obs):
        wgt = e * inv
        hi = wgt.astype(BF16)
        lo = (wgt - hi.astype(F32)).astype(BF16)
        wide = (jnp.dot(hi, expand_ref[...], preferred_element_type=F32)
                + jnp.dot(lo, expand_ref[...], preferred_element_type=F32))
        ob = wide * o if ob is None else ob + wide * o
    ms = jnp.mean(ob * ob, axis=-1, keepdims=True)
    ob = (ob * lax.rsqrt(ms + RMS_EPS) * norm_b_ref[...]).astype(BF16)
    mix = (jnp.dot(oa_ref[...], w_ref[:A_WIDTH, :], preferred_element_type=F32)
           + jnp.dot(ob, w_ref[A_WIDTH:, :], preferred_element_type=F32))
    y_ref[...] = _layer_norm(DEEPNORM_ALPHA * x_ref[...] + mix, g_ref[...], b_ref[...])


def _mix_out(x, oa, obs, lses, expand, norm_b, w_out, ln_g, ln_b):
    n = x.shape[0]
    row = lambda w: pl.BlockSpec((OUT_TILE, w), lambda i: (i, 0))
    full = lambda r, c: pl.BlockSpec((r, c), lambda i: (0, 0))
    strided = lambda w: [pl.BlockSpec((OUT_TILE // d, d * w), lambda i: (i, 0)) for _, d in B_DILATIONS]
    n_br = len(B_DILATIONS)
    return pl.pallas_call(
        _mix_out_kernel,
        grid=(n // OUT_TILE,),
        in_specs=[row(D_MODEL), row(A_WIDTH), *strided(B_WIDTH), *strided(LANES),
                  full(LANES, B_WIDTH), full(1, B_WIDTH), full(D_MODEL, D_MODEL),
                  full(1, D_MODEL), full(1, D_MODEL)],
        out_specs=row(D_MODEL),
        out_shape=jax.ShapeDtypeStruct((n, D_MODEL), F32),
        scratch_shapes=[pltpu.VMEM((n_br * B_WIDTH // LANES, OUT_TILE, LANES), F32),
                        pltpu.VMEM((n_br, OUT_TILE, LANES), F32)],
        compiler_params=_params("parallel"),
        name="mix_out",
    )(x, oa, *obs, *lses, expand, norm_b, w_out, ln_g, ln_b)


def _silu(v):
    return v / (1.0 + jnp.exp(-v))


def _route(scores, biased):
    def top2_sum(c):
        hi01, lo01 = jnp.maximum(c[0], c[1]), jnp.minimum(c[0], c[1])
        hi23, lo23 = jnp.maximum(c[2], c[3]), jnp.minimum(c[2], c[3])
        first = jnp.maximum(hi01, hi23)
        second = jnp.maximum(jnp.minimum(hi01, hi23), jnp.where(hi01 >= hi23, lo01, lo23))
        return first + second

    groups = [biased[g * EXPERTS_PER_GROUP:(g + 1) * EXPERTS_PER_GROUP] for g in range(N_GROUPS)]
    gscore = [top2_sum(c) for c in groups]
    sel = jnp.zeros_like(gscore[0], dtype=jnp.int32)
    best = gscore[0]
    for g in range(1, N_GROUPS):
        better = gscore[g] > best
        sel = jnp.where(better, g, sel)
        best = jnp.where(better, gscore[g], best)

    def pick(cols):
        out = []
        for j in range(EXPERTS_PER_GROUP):
            v = cols[j]
            for g in range(1, N_GROUPS):
                v = jnp.where(sel == g, cols[g * EXPERTS_PER_GROUP + j], v)
            out.append(v)
        return out

    cb, cs = pick(biased), pick(scores)
    i1 = jnp.zeros_like(sel)
    v1 = cb[0]
    for j in range(1, EXPERTS_PER_GROUP):
        better = cb[j] > v1
        i1 = jnp.where(better, j, i1)
        v1 = jnp.where(better, cb[j], v1)
    i2 = jnp.where(i1 == 0, 1, 0)
    v2 = jnp.where(i1 == 0, cb[1], cb[0])
    for j in range(1, EXPERTS_PER_GROUP):
        better = (cb[j] > v2) & (i1 != j) & (i2 != j)
        i2 = jnp.where(better, j, i2)
        v2 = jnp.where(better, cb[j], v2)
    s1 = sum(jnp.where(i1 == j, cs[j], 0.0) for j in range(EXPERTS_PER_GROUP))
    s2 = sum(jnp.where(i2 == j, cs[j], 0.0) for j in range(EXPERTS_PER_GROUP))
    total = s1 + s2
    gates = []
    for e in range(N_EXPERTS):
        g, j = divmod(e, EXPERTS_PER_GROUP)
        in_group = sel == g
        gate = jnp.where(in_group & (i1 == j), s1 / total, 0.0) + jnp.where(in_group & (i2 == j), s2 / total, 0.0)
        gates.append(gate)
    return gates


def _moe_kernel(x_ref, wr_ref, rb_ref, wg_ref, wu_ref, wd_ref, sg_ref, su_ref, sd_ref,
                g_ref, b_ref, y_ref, xb_ref, gate_ref, acc_ref):
    e = pl.program_id(1)

    @pl.when(e == 0)
    def _():
        x = x_ref[...]
        xb = x.astype(BF16)
        xb_ref[...] = xb
        x_lo = (x - xb.astype(F32)).astype(BF16)
        w = wr_ref[...]
        w_hi = w.astype(BF16)
        w_lo = (w - w_hi.astype(F32)).astype(BF16)
        logits = (jnp.dot(xb, w_hi, preferred_element_type=F32)
                  + jnp.dot(xb, w_lo, preferred_element_type=F32)
                  + jnp.dot(x_lo, w_hi, preferred_element_type=F32))
        scores = 1.0 / (1.0 + jnp.exp(-logits))
        biased = scores + rb_ref[...]
        cols = lambda t: [t[:, j:j + 1] for j in range(N_EXPERTS)]
        gates = _route(cols(scores), cols(biased))
        lane = lax.broadcasted_iota(jnp.int32, (x.shape[0], LANES), 1)
        gate_tile = jnp.zeros((x.shape[0], LANES), F32)
        for j, gcol in enumerate(gates):
            gate_tile = jnp.where(lane == j, gcol, gate_tile)
        gate_ref[...] = gate_tile
        hs = _silu(jnp.dot(xb, sg_ref[...], preferred_element_type=F32)) * jnp.dot(
            xb, su_ref[...], preferred_element_type=F32)
        acc_ref[...] = jnp.dot(hs.astype(BF16), sd_ref[...], preferred_element_type=F32)

    xb = xb_ref[...]
    lane = lax.broadcasted_iota(jnp.int32, gate_ref.shape, 1)
    gate = jnp.sum(jnp.where(lane == e, gate_ref[...], 0.0), axis=-1, keepdims=True)
    h = _silu(jnp.dot(xb, wg_ref[0], preferred_element_type=F32)) * jnp.dot(
        xb, wu_ref[0], preferred_element_type=F32)
    h = (h * gate).astype(BF16)
    acc_ref[...] += jnp.dot(h, wd_ref[0], preferred_element_type=F32)

    @pl.when(e == N_EXPERTS - 1)
    def _():
        y_ref[...] = _layer_norm(DEEPNORM_ALPHA * x_ref[...] + acc_ref[...], g_ref[...], b_ref[...])


def _moe(x, w_router, router_bias, w_gate, w_up, w_down, ws_gate, ws_up, ws_down, ln_g, ln_b):
    n = x.shape[0]
    row = pl.BlockSpec((MOE_TILE, D_MODEL), lambda i, e: (i, 0))
    full = lambda r, c: pl.BlockSpec((r, c), lambda i, e: (0, 0))
    return pl.pallas_call(
        _moe_kernel,
        grid=(n // MOE_TILE, N_EXPERTS),
        in_specs=[row, full(D_MODEL, N_EXPERTS), full(1, N_EXPERTS),
                  pl.BlockSpec((1, D_MODEL, D_EXPERT), lambda i, e: (e, 0, 0)),
                  pl.BlockSpec((1, D_MODEL, D_EXPERT), lambda i, e: (e, 0, 0)),
                  pl.BlockSpec((1, D_EXPERT, D_MODEL), lambda i, e: (e, 0, 0)),
                  full(D_MODEL, D_EXPERT), full(D_MODEL, D_EXPERT), full(D_EXPERT, D_MODEL),
                  full(1, D_MODEL), full(1, D_MODEL)],
        out_specs=row,
        out_shape=jax.ShapeDtypeStruct((n, D_MODEL), F32),
        scratch_shapes=[pltpu.VMEM((MOE_TILE, D_MODEL), BF16),
                        pltpu.VMEM((MOE_TILE, LANES), F32),
                        pltpu.VMEM((MOE_TILE, D_MODEL), F32)],
        compiler_params=_params("parallel", "arbitrary"),
        name="moe",
    )(x, w_router, router_bias, w_gate, w_up, w_down, ws_gate, ws_up, ws_down, ln_g, ln_b)


def _alibi_slopes():
    n_heads = A_HEADS + B_HEADS
    h = jnp.arange(1, n_heads + 1, dtype=F32)
    return jnp.exp2(-8.0 * h / n_heads)


def _layer(x, p, l, shape):
    bsz, seq_len = shape
    tokens = bsz * seq_len
    qa, ka, va, qb, kb, vb = _inproj(x, p["w_in"][l])
    seq = lambda t: t.reshape(bsz, seq_len, t.shape[-1])
    oa = _attn_a(seq(qa), seq(ka), seq(va), p["hp_a"][l], p["lane_masks"], p["norm_a"][l])
    obs, lses = [], []
    for idx, (window, dilation) in enumerate(B_DILATIONS):
        o, lse = _attn_b_branch(qb[idx], kb[idx], vb[idx], p["hp_b"], p["lane_masks"], bsz, window, dilation)
        obs.append(o)
        lses.append(lse)
    x = _mix_out(x, oa.reshape(tokens, A_WIDTH), obs, lses, p["expand"], p["norm_b"][l],
                 p["w_out"][l], p["ln1_g"][l], p["ln1_b"][l])
    return _moe(x, p["w_router"], p["router_bias"], p["w_gate"][l], p["w_up"][l], p["w_down"][l],
                p["ws_gate"][l], p["ws_up"][l], p["ws_down"][l], p["ln2_g"][l], p["ln2_b"][l])


def _trunk(x, p):
    bsz, seq_len, _ = x.shape
    h = x.reshape(bsz * seq_len, D_MODEL)
    for l in range(DEPTH):
        h = _layer(h, p, l, (bsz, seq_len))
    return h.reshape(bsz, seq_len, D_MODEL)


def _prepare(w_in, sink_a, norm_a, norm_b, w_out, ln1_g, ln1_b, w_router, router_bias,
             w_gate, w_up, w_down, ws_gate, ws_up, ws_down, ln2_g, ln2_b):
    slopes = _alibi_slopes()
    slopes_a = jnp.broadcast_to(slopes[0::2], (DEPTH, A_HEADS))
    head_lane = jnp.arange(B_WIDTH)[None, :] // HEAD_DIM
    lane_half = jnp.arange(LANES)[None, :] // HEAD_DIM

    def dup_kv(w):
        h0, h1 = w[..., :HEAD_DIM], w[..., HEAD_DIM:]
        return jnp.concatenate([h0, h0, h1, h1], axis=-1)

    lo = A_WIDTH
    w_proj = jnp.concatenate([w_in[..., :lo], dup_kv(w_in[..., lo:lo + A_KV_WIDTH]),
                              dup_kv(w_in[..., lo + A_KV_WIDTH:lo + 2 * A_KV_WIDTH]),
                              w_in[..., lo + 2 * A_KV_WIDTH:]], axis=-1)
    return {
        "w_in": w_proj.astype(BF16),
        "hp_a": jnp.stack([slopes_a, sink_a.astype(F32) * LOG2E], axis=1),
        "hp_b": jnp.stack([slopes[1::2], jnp.zeros((B_HEADS,), F32)], axis=0),
        "lane_masks": (lane_half == jnp.arange(2)[:, None]).astype(BF16),
        "expand": (jnp.arange(LANES)[:, None] == head_lane).astype(BF16),
        "norm_a": norm_a.reshape(DEPTH, 1, A_WIDTH),
        "norm_b": norm_b.reshape(DEPTH, 1, B_WIDTH),
        "w_out": w_out.astype(BF16),
        "ln1_g": ln1_g.reshape(DEPTH, 1, D_MODEL), "ln1_b": ln1_b.reshape(DEPTH, 1, D_MODEL),
        "w_router": w_router, "router_bias": router_bias.reshape(1, N_EXPERTS),
        "w_gate": w_gate.astype(BF16), "w_up": w_up.astype(BF16), "w_down": w_down.astype(BF16),
        "ws_gate": ws_gate.astype(BF16), "ws_up": ws_up.astype(BF16), "ws_down": ws_down.astype(BF16),
        "ln2_g": ln2_g.reshape(DEPTH, 1, D_MODEL), "ln2_b": ln2_b.reshape(DEPTH, 1, D_MODEL),
    }


def kernel(x_prompt, x_sample, w_in, sink_a, norm_a, norm_b, w_out, ln1_g, ln1_b, w_router, router_bias,
           w_gate, w_up, w_down, ws_gate, ws_up, ws_down, ln2_g, ln2_b):
    p = _prepare(w_in, sink_a, norm_a, norm_b, w_out, ln1_g, ln1_b, w_router, router_bias,
                 w_gate, w_up, w_down, ws_gate, ws_up, ws_down, ln2_g, ln2_b)
    return _trunk(x_prompt, p), _trunk(x_sample, p)
```

```python
import functools

import jax
import jax.numpy as jnp
from jax import lax
from jax.experimental import pallas as pl
from jax.experimental.pallas import tpu as pltpu

F32 = jnp.float32
BF16 = jnp.bfloat16

D_MODEL = 1024
DEPTH = 4
HEAD_DIM = 64
A_HEADS = 8
A_KV_HEADS = 2
A_RADIUS = 128
B_HEADS = 8
B_DILATIONS = ((128, 1), (512, 4), (2048, 16))
A_WIDTH = A_HEADS * HEAD_DIM
A_KV_WIDTH = A_KV_HEADS * HEAD_DIM
B_WIDTH = B_HEADS * HEAD_DIM
N_EXPERTS = 16
N_GROUPS = 4
EXPERTS_PER_GROUP = N_EXPERTS // N_GROUPS
D_EXPERT = 256
DEEPNORM_ALPHA = (2 * DEPTH) ** 0.25
LN_EPS = 1e-5
RMS_EPS = 1e-6
LOG2E = 1.4426950408889634

LANES = 128
MASK_PENALTY = 1e30
VMEM_LIMIT = 48 * 1024 * 1024

Q_TILE = 128
Q_BLOCK = 512
TILE_UNROLL = 4
PROJ_TILE = 512
OUT_TILE = 512
MOE_TILE = 1024
MOE_CHUNK_LOG2 = 8
MOE_CHUNK = 1 << MOE_CHUNK_LOG2
MOE_VMEM_LIMIT = 56 * 1024 * 1024
STRIDED_DILATIONS = tuple(d for _, d in B_DILATIONS if d > 1)
A_KV2_WIDTH = 2 * A_KV_WIDTH
PROJ_WIDTH = A_WIDTH + 2 * A_KV2_WIDTH + 3 * B_WIDTH


def _params(*sem):
    return pltpu.CompilerParams(dimension_semantics=sem, vmem_limit_bytes=VMEM_LIMIT)


def _dot_nt(a, b):
    return lax.dot_general(a, b, (((1,), (1,)), ((), ())), preferred_element_type=F32)


def _inproj_kernel(x_ref, w_ref, qa_ref, ka_ref, va_ref, *rest):
    b_refs, stage_ref = rest[:-1], rest[-1]
    x = x_ref[...].astype(BF16)
    q_scale = HEAD_DIM ** -0.5 * LOG2E

    def proj(lo, width):
        return jnp.dot(x, w_ref[:, lo:lo + width], preferred_element_type=F32)

    qa_ref[...] = (proj(0, A_WIDTH) * q_scale).astype(BF16)
    ka_ref[...] = proj(A_WIDTH, A_KV2_WIDTH).astype(BF16)
    va_ref[...] = proj(A_WIDTH + A_KV2_WIDTH, A_KV2_WIDTH).astype(BF16)
    lo = A_WIDTH + 2 * A_KV2_WIDTH
    n_layouts = 1 + len(STRIDED_DILATIONS)
    for j in range(3):
        r = proj(lo + j * B_WIDTH, B_WIDTH)
        if j == 0:
            r = r * q_scale
        outs = b_refs[j * n_layouts:(j + 1) * n_layouts]
        outs[0][...] = r.astype(BF16)
        for c in range(B_WIDTH // LANES):
            stage_ref[c] = r[:, c * LANES:(c + 1) * LANES]
        for d, out in zip(STRIDED_DILATIONS, outs[1:]):
            rows = PROJ_TILE // d
            for res in range(d):
                for c in range(B_WIDTH // LANES):
                    lo_c = res * B_WIDTH + c * LANES
                    out[:, lo_c:lo_c + LANES] = stage_ref[c, pl.ds(res, rows, stride=d), :].astype(BF16)


def _inproj(x, w_in):
    n = x.shape[0]
    shapes = [(n, A_WIDTH), (n, A_KV2_WIDTH), (n, A_KV2_WIDTH)]
    blocks = [(PROJ_TILE, A_WIDTH), (PROJ_TILE, A_KV2_WIDTH), (PROJ_TILE, A_KV2_WIDTH)]
    for _ in range(3):
        for d in (1,) + STRIDED_DILATIONS:
            shapes.append((n // d, d * B_WIDTH))
            blocks.append((PROJ_TILE // d, d * B_WIDTH))
    outs = pl.pallas_call(
        _inproj_kernel,
        grid=(n // PROJ_TILE,),
        in_specs=[pl.BlockSpec((PROJ_TILE, D_MODEL), lambda i: (i, 0)),
                  pl.BlockSpec((D_MODEL, PROJ_WIDTH), lambda i: (0, 0))],
        out_specs=[pl.BlockSpec(b, lambda i: (i, 0)) for b in blocks],
        out_shape=[jax.ShapeDtypeStruct(s, BF16) for s in shapes],
        scratch_shapes=[pltpu.VMEM((B_WIDTH // LANES, PROJ_TILE, LANES), F32)],
        compiler_params=_params("parallel"),
        name="inproj",
    )(x, w_in)
    n_layouts = 1 + len(STRIDED_DILATIONS)
    qa, ka, va = outs[:3]
    qb, kb, vb = (outs[3 + j * n_layouts:3 + (j + 1) * n_layouts] for j in range(3))
    return qa, ka, va, qb, kb, vb


def _band_penalty(q0, ks, rows, width, radius, dist_scale):
    qi = lax.broadcasted_iota(jnp.int32, (rows, width), 0)
    kj = lax.broadcasted_iota(jnp.int32, (rows, width), 1)
    dist = jnp.abs(kj - qi + (ks - q0))
    return jnp.where(dist <= radius, dist.astype(F32) * dist_scale, MASK_PENALTY)


def _attn_a_kernel(hp_ref, mask_ref, norm_ref, q_ref, k_ref, v_ref, o_ref, acc_ref, *, seq_len):
    width = Q_TILE + 2 * A_RADIUS
    blk = pl.program_id(1)
    lane = lax.broadcasted_iota(jnp.int32, (Q_TILE, LANES), 1)

    def tile(t, carry):
        r0 = pl.multiple_of(t * Q_TILE, Q_TILE)
        q0 = blk * Q_BLOCK + r0
        ks = pl.multiple_of(jnp.clip(q0 - A_RADIUS, 0, seq_len - width), Q_TILE)
        pen = _band_penalty(q0, ks, Q_TILE, width, A_RADIUS, LOG2E)
        for g in range(A_KV_HEADS):
            heads = [(c, half) for c in (2 * g, 2 * g + 1) for half in range(2)]
            q4 = jnp.concatenate([q_ref[0, pl.ds(r0, Q_TILE), c * LANES:(c + 1) * LANES] * mask_ref[half:half + 1, :]
                                  for c, half in heads], axis=0)
            kg = k_ref[0, pl.ds(ks, width), g * LANES:(g + 1) * LANES]
            vg = v_ref[0, pl.ds(ks, width), g * LANES:(g + 1) * LANES]
            s4 = _dot_nt(q4, kg)
            probs, inv = [], []
            for j, (c, half) in enumerate(heads):
                h = 2 * c + half
                sb = s4[j * Q_TILE:(j + 1) * Q_TILE] - hp_ref[0, h] * pen
                sink = hp_ref[1, h]
                m = jnp.maximum(jnp.max(sb, axis=-1, keepdims=True), sink)
                p = jnp.exp2(sb - m)
                denom = jnp.sum(p, axis=-1, keepdims=True) + jnp.exp2(sink - m)
                probs.append(p.astype(BF16))
                inv.append(1.0 / denom)
            o4 = jnp.dot(jnp.concatenate(probs, axis=0), vg, preferred_element_type=F32)
            for jc, c in enumerate((2 * g, 2 * g + 1)):
                lo = o4[(2 * jc) * Q_TILE:(2 * jc + 1) * Q_TILE] * inv[2 * jc]
                hi = o4[(2 * jc + 1) * Q_TILE:(2 * jc + 2) * Q_TILE] * inv[2 * jc + 1]
                acc_ref[:, c * LANES:(c + 1) * LANES] = jnp.where(lane < HEAD_DIM, lo, hi)
        o = acc_ref[...]
        ms = jnp.mean(o * o, axis=-1, keepdims=True)
        o_ref[0, pl.ds(r0, Q_TILE), :] = (o * lax.rsqrt(ms + RMS_EPS) * norm_ref[...]).astype(BF16)
        return carry

    lax.fori_loop(0, Q_BLOCK // Q_TILE, tile, 0, unroll=TILE_UNROLL)


def _attn_a(qa, ka, va, head_params, lane_masks, norm_a):
    bsz, seq_len, _ = qa.shape
    kv_spec = pl.BlockSpec((1, seq_len, A_KV2_WIDTH), lambda b, i: (b, 0, 0))
    return pl.pallas_call(
        functools.partial(_attn_a_kernel, seq_len=seq_len),
        grid=(bsz, seq_len // Q_BLOCK),
        in_specs=[pl.BlockSpec(memory_space=pltpu.SMEM),
                  pl.BlockSpec((2, LANES), lambda b, i: (0, 0)),
                  pl.BlockSpec((1, A_WIDTH), lambda b, i: (0, 0)),
                  pl.BlockSpec((1, Q_BLOCK, A_WIDTH), lambda b, i: (b, i, 0)),
                  kv_spec, kv_spec],
        out_specs=pl.BlockSpec((1, Q_BLOCK, A_WIDTH), lambda b, i: (b, i, 0)),
        out_shape=jax.ShapeDtypeStruct((bsz, seq_len, A_WIDTH), BF16),
        scratch_shapes=[pltpu.VMEM((Q_TILE, A_WIDTH), F32)],
        compiler_params=_params("parallel", "arbitrary"),
        name="attn_window",
    )(head_params, lane_masks, norm_a, qa, ka, va)


def _attn_b_kernel(hp_ref, mask_ref, q_ref, k_ref, v_ref, o_ref, lse_ref, *,
                   seq_len, q_block, res_block, radius, dist_scale):
    width = min(Q_TILE + 2 * radius, seq_len)
    blk = pl.program_id(2)
    lane = lax.broadcasted_iota(jnp.int32, (Q_TILE, LANES), 1)

    for rr in range(res_block):
        base = rr * B_WIDTH

        def tile(t, carry, base=base, rr=rr):
            r0 = pl.multiple_of(t * Q_TILE, Q_TILE)
            q0 = blk * q_block + r0
            ks = pl.multiple_of(jnp.clip(q0 - radius, 0, seq_len - width), radius)
            pen = _band_penalty(q0, ks, Q_TILE, width, radius, dist_scale * LOG2E)
            lse_tile = jnp.zeros((Q_TILE, LANES), F32)
            for c in range(B_WIDTH // LANES):
                cs = slice(base + c * LANES, base + (c + 1) * LANES)
                qc = q_ref[0, pl.ds(r0, Q_TILE), cs]
                kc = k_ref[0, pl.ds(ks, width), cs]
                vc = v_ref[0, pl.ds(ks, width), cs]
                q2 = jnp.concatenate([qc * mask_ref[0:1, :], qc * mask_ref[1:2, :]], axis=0)
                s2 = _dot_nt(q2, kc)
                probs, inv = [], []
                for half in range(2):
                    h = 2 * c + half
                    s = s2[half * Q_TILE:(half + 1) * Q_TILE] - hp_ref[0, h] * pen
                    m = jnp.max(s, axis=-1, keepdims=True)
                    p = jnp.exp2(s - m)
                    denom = jnp.sum(p, axis=-1, keepdims=True)
                    probs.append(p.astype(BF16))
                    inv.append(1.0 / denom)
                    lse_tile = jnp.where(lane == h, m + jnp.log2(denom), lse_tile)
                o2 = jnp.dot(jnp.concatenate(probs, axis=0), vc, preferred_element_type=F32)
                acc = jnp.where(lane < HEAD_DIM, o2[:Q_TILE] * inv[0], o2[Q_TILE:] * inv[1])
                o_ref[0, pl.ds(r0, Q_TILE), cs] = acc.astype(BF16)
            lse_ref[0, pl.ds(r0, Q_TILE), rr * LANES:(rr + 1) * LANES] = lse_tile
            return carry

        lax.fori_loop(0, q_block // Q_TILE, tile, 0, unroll=min(TILE_UNROLL, q_block // Q_TILE))


def _attn_b_branch(qv, kv, vv, head_params, lane_masks, bsz, window, dilation):
    seq_len = qv.shape[0] // bsz
    radius = (window // 2) // dilation
    q_block = min(Q_BLOCK, seq_len)
    res_block = min(dilation, max(1, Q_BLOCK // q_block))
    view = lambda t: t.reshape(bsz, seq_len, t.shape[-1])
    kv_spec = pl.BlockSpec((1, seq_len, res_block * B_WIDTH), lambda b, r, i: (b, 0, r))
    o, lse = pl.pallas_call(
        functools.partial(_attn_b_kernel, seq_len=seq_len, q_block=q_block, res_block=res_block,
                          radius=radius, dist_scale=float(dilation)),
        grid=(bsz, dilation // res_block, seq_len // q_block),
        in_specs=[pl.BlockSpec(memory_space=pltpu.SMEM),
                  pl.BlockSpec((2, LANES), lambda b, r, i: (0, 0)),
                  pl.BlockSpec((1, q_block, res_block * B_WIDTH), lambda b, r, i: (b, i, r)),
                  kv_spec, kv_spec],
        out_specs=[pl.BlockSpec((1, q_block, res_block * B_WIDTH), lambda b, r, i: (b, i, r)),
                   pl.BlockSpec((1, q_block, res_block * LANES), lambda b, r, i: (b, i, r))],
        out_shape=[jax.ShapeDtypeStruct((bsz, seq_len, dilation * B_WIDTH), BF16),
                   jax.ShapeDtypeStruct((bsz, seq_len, dilation * LANES), F32)],
        compiler_params=_params("parallel", "parallel", "arbitrary"),
        name=f"attn_dilated_{dilation}",
    )(head_params, lane_masks, view(qv), view(kv), view(vv))
    return o.reshape(bsz * seq_len, dilation * B_WIDTH), lse.reshape(bsz * seq_len, dilation * LANES)


def _layer_norm(v, g, b):
    mu = jnp.mean(v, axis=-1, keepdims=True)
    c = v - mu
    var = jnp.mean(c * c, axis=-1, keepdims=True)
    return c * lax.rsqrt(var + LN_EPS) * g + b


def _mix_out_kernel(x_ref, oa_ref, *rest):
    n_br = len(B_DILATIONS)
    ob_refs, lse_refs = rest[:n_br], rest[n_br:2 * n_br]
    expand_ref, norm_b_ref, w_ref, g_ref, b_ref, y_ref, ob_stage, lse_stage = rest[2 * n_br:]
    obs, lses = [], []
    for idx, ((_, d), o_ref, l_ref) in enumerate(zip(B_DILATIONS, ob_refs, lse_refs)):
        if d == 1:
            obs.append(o_ref[...].astype(F32))
            lses.append(l_ref[...])
            continue
        rows = OUT_TILE // d
        n_cols = B_WIDTH // LANES
        for res in range(d):
            for c in range(n_cols):
                lo_c = res * B_WIDTH + c * LANES
                ob_stage[idx * n_cols + c, pl.ds(res, rows, stride=d), :] = o_ref[:, lo_c:lo_c + LANES].astype(F32)
            lse_stage[idx, pl.ds(res, rows, stride=d), :] = l_ref[:, res * LANES:(res + 1) * LANES]
        obs.append(jnp.concatenate([ob_stage[idx * n_cols + c] for c in range(n_cols)], axis=-1))
        lses.append(lse_stage[idx])
    top = functools.reduce(jnp.maximum, lses)
    es = [jnp.exp2(l - top) for l in lses]
    inv = 1.0 / functools.reduce(lambda a, b: a + b, es)
    ob = None
    for e, o in zip(es, obs):
        wgt = e * inv
        hi = wgt.astype(BF16)
        lo = (wgt - hi.astype(F32)).astype(BF16)
        wide = (jnp.dot(hi, expand_ref[...], preferred_element_type=F32)
                + jnp.dot(lo, expand_ref[...], preferred_element_type=F32))
        ob = wide * o if ob is None else ob + wide * o
    ms = jnp.mean(ob * ob, axis=-1, keepdims=True)
    ob = (ob * lax.rsqrt(ms + RMS_EPS) * norm_b_ref[...]).astype(BF16)
    mix = (jnp.dot(oa_ref[...], w_ref[:A_WIDTH, :], preferred_element_type=F32)
           + jnp.dot(ob, w_ref[A_WIDTH:, :], preferred_element_type=F32))
    y_ref[...] = _layer_norm(DEEPNORM_ALPHA * x_ref[...] + mix, g_ref[...], b_ref[...])


def _mix_out(x, oa, obs, lses, expand, norm_b, w_out, ln_g, ln_b):
    n = x.shape[0]
    row = lambda w: pl.BlockSpec((OUT_TILE, w), lambda i: (i, 0))
    full = lambda r, c: pl.BlockSpec((r, c), lambda i: (0, 0))
    strided = lambda w: [pl.BlockSpec((OUT_TILE // d, d * w), lambda i: (i, 0)) for _, d in B_DILATIONS]
    n_br = len(B_DILATIONS)
    return pl.pallas_call(
        _mix_out_kernel,
        grid=(n // OUT_TILE,),
        in_specs=[row(D_MODEL), row(A_WIDTH), *strided(B_WIDTH), *strided(LANES),
                  full(LANES, B_WIDTH), full(1, B_WIDTH), full(D_MODEL, D_MODEL),
                  full(1, D_MODEL), full(1, D_MODEL)],
        out_specs=row(D_MODEL),
        out_shape=jax.ShapeDtypeStruct((n, D_MODEL), F32),
        scratch_shapes=[pltpu.VMEM((n_br * B_WIDTH // LANES, OUT_TILE, LANES), F32),
                        pltpu.VMEM((n_br, OUT_TILE, LANES), F32)],
        compiler_params=_params("parallel"),
        name="mix_out",
    )(x, oa, *obs, *lses, expand, norm_b, w_out, ln_g, ln_b)


def _silu(v):
    return v / (1.0 + jnp.exp(-v))


def _route(scores, biased):
    def top2_sum(c):
        hi01, lo01 = jnp.maximum(c[0], c[1]), jnp.minimum(c[0], c[1])
        hi23, lo23 = jnp.maximum(c[2], c[3]), jnp.minimum(c[2], c[3])
        first = jnp.maximum(hi01, hi23)
        second = jnp.maximum(jnp.minimum(hi01, hi23), jnp.where(hi01 >= hi23, lo01, lo23))
        return first + second

    groups = [biased[g * EXPERTS_PER_GROUP:(g + 1) * EXPERTS_PER_GROUP] for g in range(N_GROUPS)]
    gscore = [top2_sum(c) for c in groups]
    sel = jnp.zeros_like(gscore[0], dtype=jnp.int32)
    best = gscore[0]
    for g in range(1, N_GROUPS):
        better = gscore[g] > best
        sel = jnp.where(better, g, sel)
        best = jnp.where(better, gscore[g], best)

    def pick(cols):
        out = []
        for j in range(EXPERTS_PER_GROUP):
            v = cols[j]
            for g in range(1, N_GROUPS):
                v = jnp.where(sel == g, cols[g * EXPERTS_PER_GROUP + j], v)
            out.append(v)
        return out

    cb, cs = pick(biased), pick(scores)
    i1 = jnp.zeros_like(sel)
    v1 = cb[0]
    for j in range(1, EXPERTS_PER_GROUP):
        better = cb[j] > v1
        i1 = jnp.where(better, j, i1)
        v1 = jnp.where(better, cb[j], v1)
    i2 = jnp.where(i1 == 0, 1, 0)
    v2 = jnp.where(i1 == 0, cb[1], cb[0])
    for j in range(1, EXPERTS_PER_GROUP):
        better = (cb[j] > v2) & (i1 != j) & (i2 != j)
        i2 = jnp.where(better, j, i2)
        v2 = jnp.where(better, cb[j], v2)
    s1 = sum(jnp.where(i1 == j, cs[j], 0.0) for j in range(EXPERTS_PER_GROUP))
    s2 = sum(jnp.where(i2 == j, cs[j], 0.0) for j in range(EXPERTS_PER_GROUP))
    total = s1 + s2
    gates = []
    for e in range(N_EXPERTS):
        g, j = divmod(e, EXPERTS_PER_GROUP)
        in_group = sel == g
        gate = jnp.where(in_group & (i1 == j), s1 / total, 0.0) + jnp.where(in_group & (i2 == j), s2 / total, 0.0)
        gates.append(gate)
    return gates, sel


def _split_bf16(v):
    hi = v.astype(BF16)
    return hi, (v - hi.astype(F32)).astype(BF16)


def _moe_kernel(x_ref, tri_ref, wr_ref, rb_ref, wg_ref, wu_ref, wd_ref, sg_ref, su_ref, sd_ref,
                g_ref, b_ref, y_ref, xb_ref, xs_ref, gs_ref, ys_ref, pos_ref, off_ref):
    grp = pl.program_id(1)
    tile = MOE_TILE
    n_chunks = tile // MOE_CHUNK

    def perm_rows(c):
        slot = (lax.broadcasted_iota(jnp.int32, (MOE_CHUNK, tile), 0) + c * MOE_CHUNK).astype(F32)
        return jnp.where(pos_ref[...] == slot, 1.0, 0.0).astype(BF16)

    @pl.when(grp == 0)
    def _route_and_sort():
        x = x_ref[...]
        xb = x.astype(BF16)
        xb_ref[...] = xb
        x_lo = (x - xb.astype(F32)).astype(BF16)
        w_hi, w_lo = _split_bf16(wr_ref[...])
        logits = (_dot_nt(jnp.concatenate([w_hi, w_lo], axis=0), xb)
                  + jnp.concatenate([_dot_nt(w_hi, x_lo), jnp.zeros((N_EXPERTS, tile), F32)], axis=0))
        logits = logits[:N_EXPERTS] + logits[N_EXPERTS:]
        scores = 1.0 / (1.0 + jnp.exp(-logits))
        biased = scores + rb_ref[...]
        rows = lambda t: [t[j:j + 1, :] for j in range(N_EXPERTS)]
        gates, sel = _route(rows(scores), rows(biased))
        sub = lax.broadcasted_iota(jnp.int32, (8, tile), 0)
        onehot = jnp.where(sub == sel, 1.0, 0.0).astype(BF16)
        rank = jnp.dot(onehot, tri_ref[...], preferred_element_type=F32)
        last_lane = lax.broadcasted_iota(jnp.int32, (1, tile), 1) == tile - 1
        pos = jnp.zeros((1, tile), F32)
        start = jnp.float32(0.0)
        off_ref[0] = 0
        for g in range(N_GROUPS):
            pos = jnp.where(sel == g, start + rank[g:g + 1, :] - 1.0, pos)
            start = start + jnp.sum(jnp.where(last_lane, rank[g:g + 1, :], 0.0))
            off_ref[g + 1] = start.astype(jnp.int32)
        pos_ref[...] = pos
        gate_rows = jnp.concatenate(gates + [jnp.zeros((LANES - N_EXPERTS, tile), F32)], axis=0)
        g_hi, g_lo = _split_bf16(gate_rows)
        for c in range(n_chunks):
            perm = perm_rows(c)
            rows_c = slice(c * MOE_CHUNK, (c + 1) * MOE_CHUNK)
            xs_ref[rows_c, :] = jnp.dot(perm, xb, preferred_element_type=F32).astype(BF16)
            gs_ref[rows_c, :] = _dot_nt(perm, g_hi) + _dot_nt(perm, g_lo)
        ys_ref[...] = jnp.zeros(ys_ref.shape, F32)

    first = lax.shift_right_logical(off_ref[grp], MOE_CHUNK_LOG2)
    last = lax.shift_right_logical(off_ref[grp + 1] + (MOE_CHUNK - 1), MOE_CHUNK_LOG2)
    chunk_lane = lax.broadcasted_iota(jnp.int32, (MOE_CHUNK, LANES), 1)

    def chunk(c, carry):
        r0 = pl.multiple_of(c * MOE_CHUNK, MOE_CHUNK)
        xc = xs_ref[pl.ds(r0, MOE_CHUNK), :]
        gc = gs_ref[pl.ds(r0, MOE_CHUNK), :]
        yc = None
        for j in range(EXPERTS_PER_GROUP):
            gate = jnp.sum(jnp.where(chunk_lane == grp * EXPERTS_PER_GROUP + j, gc, 0.0), axis=-1, keepdims=True)
            h = _silu(jnp.dot(xc, wg_ref[j], preferred_element_type=F32)) * jnp.dot(
                xc, wu_ref[j], preferred_element_type=F32)
            part = jnp.dot((h * gate).astype(BF16), wd_ref[j], preferred_element_type=F32)
            yc = part if yc is None else yc + part
        ys_ref[pl.ds(r0, MOE_CHUNK), :] += yc
        return carry

    lax.fori_loop(first, last, chunk, 0)

    @pl.when(grp == N_GROUPS - 1)
    def _unsort_and_norm():
        xb = xb_ref[...]
        hs = _silu(jnp.dot(xb, sg_ref[...], preferred_element_type=F32)) * jnp.dot(
            xb, su_ref[...], preferred_element_type=F32)
        acc = jnp.dot(hs.astype(BF16), sd_ref[...], preferred_element_type=F32)
        for c in range(n_chunks):
            acc = acc + lax.dot_general(perm_rows(c), ys_ref[c * MOE_CHUNK:(c + 1) * MOE_CHUNK, :].astype(BF16),
                                        (((0,), (0,)), ((), ())), preferred_element_type=F32)
        y_ref[...] = _layer_norm(DEEPNORM_ALPHA * x_ref[...] + acc, g_ref[...], b_ref[...])


def _moe(x, tri, w_router, router_bias, w_gate, w_up, w_down, ws_gate, ws_up, ws_down, ln_g, ln_b):
    n = x.shape[0]
    row = pl.BlockSpec((MOE_TILE, D_MODEL), lambda i, g: (i, 0))
    full = lambda r, c: pl.BlockSpec((r, c), lambda i, g: (0, 0))
    group_w = lambda r, c: pl.BlockSpec((EXPERTS_PER_GROUP, r, c), lambda i, g: (g, 0, 0))
    return pl.pallas_call(
        _moe_kernel,
        grid=(n // MOE_TILE, N_GROUPS),
        in_specs=[row, full(MOE_TILE, MOE_TILE), full(N_EXPERTS, D_MODEL), full(N_EXPERTS, 1),
                  group_w(D_MODEL, D_EXPERT), group_w(D_MODEL, D_EXPERT), group_w(D_EXPERT, D_MODEL),
                  full(D_MODEL, D_EXPERT), full(D_MODEL, D_EXPERT), full(D_EXPERT, D_MODEL),
                  full(1, D_MODEL), full(1, D_MODEL)],
        out_specs=row,
        out_shape=jax.ShapeDtypeStruct((n, D_MODEL), F32),
        scratch_shapes=[pltpu.VMEM((MOE_TILE, D_MODEL), BF16),
                        pltpu.VMEM((MOE_TILE, D_MODEL), BF16),
                        pltpu.VMEM((MOE_TILE, LANES), F32),
                        pltpu.VMEM((MOE_TILE, D_MODEL), F32),
                        pltpu.VMEM((1, MOE_TILE), F32),
                        pltpu.SMEM((N_GROUPS + 1,), jnp.int32)],
        compiler_params=pltpu.CompilerParams(dimension_semantics=("parallel", "arbitrary"),
                                             vmem_limit_bytes=MOE_VMEM_LIMIT),
        name="moe",
    )(x, tri, w_router, router_bias, w_gate, w_up, w_down, ws_gate, ws_up, ws_down, ln_g, ln_b)


def _alibi_slopes():
    n_heads = A_HEADS + B_HEADS
    h = jnp.arange(1, n_heads + 1, dtype=F32)
    return jnp.exp2(-8.0 * h / n_heads)


def _layer(x, p, l, shape):
    bsz, seq_len = shape
    tokens = bsz * seq_len
    qa, ka, va, qb, kb, vb = _inproj(x, p["w_in"][l])
    seq = lambda t: t.reshape(bsz, seq_len, t.shape[-1])
    oa = _attn_a(seq(qa), seq(ka), seq(va), p["hp_a"][l], p["lane_masks"], p["norm_a"][l])
    obs, lses = [], []
    for idx, (window, dilation) in enumerate(B_DILATIONS):
        o, lse = _attn_b_branch(qb[idx], kb[idx], vb[idx], p["hp_b"], p["lane_masks"], bsz, window, dilation)
        obs.append(o)
        lses.append(lse)
    x = _mix_out(x, oa.reshape(tokens, A_WIDTH), obs, lses, p["expand"], p["norm_b"][l],
                 p["w_out"][l], p["ln1_g"][l], p["ln1_b"][l])
    return _moe(x, p["tri"], p["w_router"], p["router_bias"], p["w_gate"][l], p["w_up"][l], p["w_down"][l],
                p["ws_gate"][l], p["ws_up"][l], p["ws_down"][l], p["ln2_g"][l], p["ln2_b"][l])


def _trunk(x, p):
    bsz, seq_len, _ = x.shape
    h = x.reshape(bsz * seq_len, D_MODEL)
    for l in range(DEPTH):
        h = _layer(h, p, l, (bsz, seq_len))
    return h.reshape(bsz, seq_len, D_MODEL)


def _prepare(w_in, sink_a, norm_a, norm_b, w_out, ln1_g, ln1_b, w_router, router_bias,
             w_gate, w_up, w_down, ws_gate, ws_up, ws_down, ln2_g, ln2_b):
    slopes = _alibi_slopes()
    slopes_a = jnp.broadcast_to(slopes[0::2], (DEPTH, A_HEADS))
    head_lane = jnp.arange(B_WIDTH)[None, :] // HEAD_DIM
    lane_half = jnp.arange(LANES)[None, :] // HEAD_DIM

    def dup_kv(w):
        h0, h1 = w[..., :HEAD_DIM], w[..., HEAD_DIM:]
        return jnp.concatenate([h0, h0, h1, h1], axis=-1)

    lo = A_WIDTH
    w_proj = jnp.concatenate([w_in[..., :lo], dup_kv(w_in[..., lo:lo + A_KV_WIDTH]),
                              dup_kv(w_in[..., lo + A_KV_WIDTH:lo + 2 * A_KV_WIDTH]),
                              w_in[..., lo + 2 * A_KV_WIDTH:]], axis=-1)
    return {
        "w_in": w_proj.astype(BF16),
        "hp_a": jnp.stack([slopes_a, sink_a.astype(F32) * LOG2E], axis=1),
        "hp_b": jnp.stack([slopes[1::2], jnp.zeros((B_HEADS,), F32)], axis=0),
        "lane_masks": (lane_half == jnp.arange(2)[:, None]).astype(BF16),
        "expand": (jnp.arange(LANES)[:, None] == head_lane).astype(BF16),
        "tri": jnp.tri(MOE_TILE, dtype=BF16).T,
        "norm_a": norm_a.reshape(DEPTH, 1, A_WIDTH),
        "norm_b": norm_b.reshape(DEPTH, 1, B_WIDTH),
        "w_out": w_out.astype(BF16),
        "ln1_g": ln1_g.reshape(DEPTH, 1, D_MODEL), "ln1_b": ln1_b.reshape(DEPTH, 1, D_MODEL),
        "w_router": w_router.T, "router_bias": router_bias.reshape(N_EXPERTS, 1),
        "w_gate": w_gate.astype(BF16), "w_up": w_up.astype(BF16), "w_down": w_down.astype(BF16),
        "ws_gate": ws_gate.astype(BF16), "ws_up": ws_up.astype(BF16), "ws_down": ws_down.astype(BF16),
        "ln2_g": ln2_g.reshape(DEPTH, 1, D_MODEL), "ln2_b": ln2_b.reshape(DEPTH, 1, D_MODEL),
    }


def kernel(x_prompt, x_sample, w_in, sink_a, norm_a, norm_b, w_out, ln1_g, ln1_b, w_router, router_bias,
           w_gate, w_up, w_down, ws_gate, ws_up, ws_down, ln2_g, ln2_b):
    p = _prepare(w_in, sink_a, norm_a, norm_b, w_out, ln1_g, ln1_b, w_router, router_bias,
                 w_gate, w_up, w_down, ws_gate, ws_up, ws_down, ln2_g, ln2_b)
    return _trunk(x_prompt, p), _trunk(x_sample, p)
```

```python
import functools

import jax
import jax.numpy as jnp
from jax import lax
from jax.experimental import pallas as pl
from jax.experimental.pallas import tpu as pltpu

F32 = jnp.float32
BF16 = jnp.bfloat16

D_MODEL = 1024
DEPTH = 4
HEAD_DIM = 64
A_HEADS = 8
A_KV_HEADS = 2
A_RADIUS = 128
B_HEADS = 8
B_DILATIONS = ((128, 1), (512, 4), (2048, 16))
A_WIDTH = A_HEADS * HEAD_DIM
A_KV_WIDTH = A_KV_HEADS * HEAD_DIM
B_WIDTH = B_HEADS * HEAD_DIM
N_EXPERTS = 16
N_GROUPS = 4
EXPERTS_PER_GROUP = N_EXPERTS // N_GROUPS
D_EXPERT = 256
DEEPNORM_ALPHA = (2 * DEPTH) ** 0.25
LN_EPS = 1e-5
RMS_EPS = 1e-6
LOG2E = 1.4426950408889634

LANES = 128
MASK_PENALTY = 1e30
VMEM_LIMIT = 48 * 1024 * 1024

Q_TILE = 128
Q_BLOCK = 512
TILE_UNROLL = 4
PROJ_TILE = 512
OUT_TILE = 512
MOE_TILE = 1024
MOE_CHUNK = 320
SORT_CHUNK = 256
ROW_ALIGN_LOG2 = 4
MOE_VMEM_LIMIT = 56 * 1024 * 1024
STRIDED_DILATIONS = tuple(d for _, d in B_DILATIONS if d > 1)
A_KV2_WIDTH = 2 * A_KV_WIDTH
PROJ_WIDTH = A_WIDTH + 2 * A_KV2_WIDTH + 3 * B_WIDTH


def _params(*sem):
    return pltpu.CompilerParams(dimension_semantics=sem, vmem_limit_bytes=VMEM_LIMIT)


def _dot_nt(a, b):
    return lax.dot_general(a, b, (((1,), (1,)), ((), ())), preferred_element_type=F32)


def _inproj_kernel(x_ref, w_ref, qa_ref, ka_ref, va_ref, *rest):
    b_refs, stage_ref = rest[:-1], rest[-1]
    x = x_ref[...].astype(BF16)
    q_scale = HEAD_DIM ** -0.5 * LOG2E

    def proj(lo, width):
        return jnp.dot(x, w_ref[:, lo:lo + width], preferred_element_type=F32)

    qa_ref[...] = (proj(0, A_WIDTH) * q_scale).astype(BF16)
    ka_ref[...] = proj(A_WIDTH, A_KV2_WIDTH).astype(BF16)
    va_ref[...] = proj(A_WIDTH + A_KV2_WIDTH, A_KV2_WIDTH).astype(BF16)
    lo = A_WIDTH + 2 * A_KV2_WIDTH
    n_layouts = 1 + len(STRIDED_DILATIONS)
    for j in range(3):
        r = proj(lo + j * B_WIDTH, B_WIDTH)
        if j == 0:
            r = r * q_scale
        outs = b_refs[j * n_layouts:(j + 1) * n_layouts]
        outs[0][...] = r.astype(BF16)
        for c in range(B_WIDTH // LANES):
            stage_ref[c] = r[:, c * LANES:(c + 1) * LANES]
        for d, out in zip(STRIDED_DILATIONS, outs[1:]):
            rows = PROJ_TILE // d
            for res in range(d):
                for c in range(B_WIDTH // LANES):
                    lo_c = res * B_WIDTH + c * LANES
                    out[:, lo_c:lo_c + LANES] = stage_ref[c, pl.ds(res, rows, stride=d), :].astype(BF16)


def _inproj(x, w_in):
    n = x.shape[0]
    shapes = [(n, A_WIDTH), (n, A_KV2_WIDTH), (n, A_KV2_WIDTH)]
    blocks = [(PROJ_TILE, A_WIDTH), (PROJ_TILE, A_KV2_WIDTH), (PROJ_TILE, A_KV2_WIDTH)]
    for _ in range(3):
        for d in (1,) + STRIDED_DILATIONS:
            shapes.append((n // d, d * B_WIDTH))
            blocks.append((PROJ_TILE // d, d * B_WIDTH))
    outs = pl.pallas_call(
        _inproj_kernel,
        grid=(n // PROJ_TILE,),
        in_specs=[pl.BlockSpec((PROJ_TILE, D_MODEL), lambda i: (i, 0)),
                  pl.BlockSpec((D_MODEL, PROJ_WIDTH), lambda i: (0, 0))],
        out_specs=[pl.BlockSpec(b, lambda i: (i, 0)) for b in blocks],
        out_shape=[jax.ShapeDtypeStruct(s, BF16) for s in shapes],
        scratch_shapes=[pltpu.VMEM((B_WIDTH // LANES, PROJ_TILE, LANES), F32)],
        compiler_params=_params("parallel"),
        name="inproj",
    )(x, w_in)
    n_layouts = 1 + len(STRIDED_DILATIONS)
    qa, ka, va = outs[:3]
    qb, kb, vb = (outs[3 + j * n_layouts:3 + (j + 1) * n_layouts] for j in range(3))
    return qa, ka, va, qb, kb, vb


def _band_bias_table(slopes, radius, width, dist_scale):
    qi = jnp.arange(Q_TILE)[:, None]
    kj = jnp.arange(width)[None, :]
    cases = []
    for offset in (0, -radius, Q_TILE - width):
        dist = jnp.abs(kj - qi + offset)
        pen = jnp.where(dist <= radius, dist.astype(F32) * (dist_scale * LOG2E), MASK_PENALTY)
        cases.append(-slopes[:, None, None] * pen[None])
    return jnp.stack(cases, axis=0)


def _tile_case(q0, seq_len):
    return jnp.where(q0 == 0, 0, jnp.where(q0 == seq_len - Q_TILE, 2, 1))


def _with_ones(v):
    return jnp.concatenate([v, jnp.ones(v.shape, v.dtype)], axis=-1)


def _attn_a_kernel(sink_ref, mask_ref, norm_ref, bias_ref, q_ref, k_ref, v_ref, o_ref, acc_ref, *, seq_len):
    width = Q_TILE + 2 * A_RADIUS
    blk = pl.program_id(1)
    lane = lax.broadcasted_iota(jnp.int32, (Q_TILE, LANES), 1)

    def tile(t, carry):
        r0 = pl.multiple_of(t * Q_TILE, Q_TILE)
        q0 = blk * Q_BLOCK + r0
        ks = pl.multiple_of(jnp.clip(q0 - A_RADIUS, 0, seq_len - width), Q_TILE)
        case = _tile_case(q0, seq_len)
        for g in range(A_KV_HEADS):
            heads = [(c, half) for c in (2 * g, 2 * g + 1) for half in range(2)]
            q4 = jnp.concatenate([q_ref[0, pl.ds(r0, Q_TILE), c * LANES:(c + 1) * LANES] * mask_ref[half:half + 1, :]
                                  for c, half in heads], axis=0)
            kg = k_ref[0, pl.ds(ks, width), g * LANES:(g + 1) * LANES]
            vg = v_ref[0, pl.ds(ks, width), g * LANES:(g + 1) * LANES]
            s4 = _dot_nt(q4, kg)
            probs, inv = [], []
            for j, (c, half) in enumerate(heads):
                h = 2 * c + half
                sb = s4[j * Q_TILE:(j + 1) * Q_TILE] + bias_ref[case, h]
                sink = sink_ref[h]
                m = jnp.maximum(jnp.max(sb, axis=-1, keepdims=True), sink)
                p = jnp.exp2(sb - m)
                inv.append(1.0 / (jnp.sum(p, axis=-1, keepdims=True) + jnp.exp2(sink - m)))
                probs.append(p.astype(BF16))
            o4 = jnp.dot(jnp.concatenate(probs, axis=0), vg, preferred_element_type=F32)
            for jc, c in enumerate((2 * g, 2 * g + 1)):
                lo = o4[(2 * jc) * Q_TILE:(2 * jc + 1) * Q_TILE] * inv[2 * jc]
                hi = o4[(2 * jc + 1) * Q_TILE:(2 * jc + 2) * Q_TILE] * inv[2 * jc + 1]
                acc_ref[:, c * LANES:(c + 1) * LANES] = jnp.where(lane < HEAD_DIM, lo, hi)
        o = acc_ref[...]
        ms = jnp.mean(o * o, axis=-1, keepdims=True)
        o_ref[0, pl.ds(r0, Q_TILE), :] = (o * lax.rsqrt(ms + RMS_EPS) * norm_ref[...]).astype(BF16)
        return carry

    lax.fori_loop(0, Q_BLOCK // Q_TILE, tile, 0, unroll=TILE_UNROLL)


def _attn_a(qa, ka, va, sinks, slopes, lane_masks, norm_a):
    bsz, seq_len, _ = qa.shape
    width = Q_TILE + 2 * A_RADIUS
    bias = _band_bias_table(slopes, A_RADIUS, width, 1.0)
    kv_spec = pl.BlockSpec((1, seq_len, A_KV2_WIDTH), lambda b, i: (b, 0, 0))
    return pl.pallas_call(
        functools.partial(_attn_a_kernel, seq_len=seq_len),
        grid=(bsz, seq_len // Q_BLOCK),
        in_specs=[pl.BlockSpec(memory_space=pltpu.SMEM),
                  pl.BlockSpec((2, LANES), lambda b, i: (0, 0)),
                  pl.BlockSpec((1, A_WIDTH), lambda b, i: (0, 0)),
                  pl.BlockSpec(bias.shape, lambda b, i: (0, 0, 0, 0)),
                  pl.BlockSpec((1, Q_BLOCK, A_WIDTH), lambda b, i: (b, i, 0)),
                  kv_spec, kv_spec],
        out_specs=pl.BlockSpec((1, Q_BLOCK, A_WIDTH), lambda b, i: (b, i, 0)),
        out_shape=jax.ShapeDtypeStruct((bsz, seq_len, A_WIDTH), BF16),
        scratch_shapes=[pltpu.VMEM((Q_TILE, A_WIDTH), F32)],
        compiler_params=_params("parallel", "arbitrary"),
        name="attn_window",
    )(sinks, lane_masks, norm_a, bias, qa, ka, va)


def _attn_b_kernel(mask_ref, bias_ref, q_ref, k_ref, v_ref, o_ref, lse_ref, *,
                   seq_len, q_block, res_block, radius):
    width = min(Q_TILE + 2 * radius, seq_len)
    mxu_sums = width >= 2 * LANES
    blk = pl.program_id(2)
    lane = lax.broadcasted_iota(jnp.int32, (Q_TILE, LANES), 1)

    for rr in range(res_block):
        base = rr * B_WIDTH

        def tile(t, carry, base=base, rr=rr):
            r0 = pl.multiple_of(t * Q_TILE, Q_TILE)
            q0 = blk * q_block + r0
            ks = pl.multiple_of(jnp.clip(q0 - radius, 0, seq_len - width), radius)
            case = _tile_case(q0, seq_len)
            max_tile = jnp.zeros((Q_TILE, LANES), F32)
            den_tile = jnp.ones((Q_TILE, LANES), F32)
            for c in range(B_WIDTH // LANES):
                cs = slice(base + c * LANES, base + (c + 1) * LANES)
                qc = q_ref[0, pl.ds(r0, Q_TILE), cs]
                kc = k_ref[0, pl.ds(ks, width), cs]
                vc = v_ref[0, pl.ds(ks, width), cs]
                if mxu_sums:
                    vc = _with_ones(vc)
                q2 = jnp.concatenate([qc * mask_ref[0:1, :], qc * mask_ref[1:2, :]], axis=0)
                s2 = _dot_nt(q2, kc)
                probs, maxes, sums = [], [], []
                for half in range(2):
                    s = s2[half * Q_TILE:(half + 1) * Q_TILE] + bias_ref[case, 2 * c + half]
                    m = jnp.max(s, axis=-1, keepdims=True)
                    p = jnp.exp2(s - m)
                    if not mxu_sums:
                        sums.append(jnp.sum(p, axis=-1, keepdims=True))
                    probs.append(p.astype(BF16))
                    maxes.append(m)
                o2 = jnp.dot(jnp.concatenate(probs, axis=0), vc, preferred_element_type=F32)
                if mxu_sums:
                    sums = [o2[:Q_TILE, LANES:], o2[Q_TILE:, LANES:]]
                low = lane < HEAD_DIM
                num = jnp.where(low, o2[:Q_TILE, :LANES], o2[Q_TILE:, :LANES])
                o_ref[0, pl.ds(r0, Q_TILE), cs] = (num / jnp.where(low, sums[0], sums[1])).astype(BF16)
                for half in range(2):
                    here = lane == 2 * c + half
                    max_tile = jnp.where(here, maxes[half], max_tile)
                    den_tile = jnp.where(here, sums[half], den_tile)
            lse_ref[0, pl.ds(r0, Q_TILE), rr * LANES:(rr + 1) * LANES] = max_tile + jnp.log2(den_tile)
            return carry

        lax.fori_loop(0, q_block // Q_TILE, tile, 0, unroll=min(TILE_UNROLL, q_block // Q_TILE))


def _attn_b_branch(qv, kv, vv, slopes, lane_masks, bsz, window, dilation):
    seq_len = qv.shape[0] // bsz
    radius = (window // 2) // dilation
    q_block = min(Q_BLOCK, seq_len)
    res_block = min(dilation, max(1, Q_BLOCK // q_block))
    bias = _band_bias_table(slopes, radius, min(Q_TILE + 2 * radius, seq_len), float(dilation))
    view = lambda t: t.reshape(bsz, seq_len, t.shape[-1])
    kv_spec = pl.BlockSpec((1, seq_len, res_block * B_WIDTH), lambda b, r, i: (b, 0, r))
    o, lse = pl.pallas_call(
        functools.partial(_attn_b_kernel, seq_len=seq_len, q_block=q_block, res_block=res_block, radius=radius),
        grid=(bsz, dilation // res_block, seq_len // q_block),
        in_specs=[pl.BlockSpec((2, LANES), lambda b, r, i: (0, 0)),
                  pl.BlockSpec(bias.shape, lambda b, r, i: (0, 0, 0, 0)),
                  pl.BlockSpec((1, q_block, res_block * B_WIDTH), lambda b, r, i: (b, i, r)),
                  kv_spec, kv_spec],
        out_specs=[pl.BlockSpec((1, q_block, res_block * B_WIDTH), lambda b, r, i: (b, i, r)),
                   pl.BlockSpec((1, q_block, res_block * LANES), lambda b, r, i: (b, i, r))],
        out_shape=[jax.ShapeDtypeStruct((bsz, seq_len, dilation * B_WIDTH), BF16),
                   jax.ShapeDtypeStruct((bsz, seq_len, dilation * LANES), F32)],
        compiler_params=_params("parallel", "parallel", "arbitrary"),
        name=f"attn_dilated_{dilation}",
    )(lane_masks, bias, view(qv), view(kv), view(vv))
    return o.reshape(bsz * seq_len, dilation * B_WIDTH), lse.reshape(bsz * seq_len, dilation * LANES)


def _layer_norm(v, g, b):
    mu = jnp.mean(v, axis=-1, keepdims=True)
    c = v - mu
    var = jnp.mean(c * c, axis=-1, keepdims=True)
    return c * lax.rsqrt(var + LN_EPS) * g + b


def _mix_out_kernel(x_ref, oa_ref, *rest):
    n_br = len(B_DILATIONS)
    ob_refs, lse_refs = rest[:n_br], rest[n_br:2 * n_br]
    expand_ref, norm_b_ref, w_ref, g_ref, b_ref, y_ref, ob_stage, lse_stage = rest[2 * n_br:]
    obs, lses = [], []
    for idx, ((_, d), o_ref, l_ref) in enumerate(zip(B_DILATIONS, ob_refs, lse_refs)):
        if d == 1:
            obs.append(o_ref[...].astype(F32))
            lses.append(l_ref[...])
            continue
        rows = OUT_TILE // d
        n_cols = B_WIDTH // LANES
        for res in range(d):
            for c in range(n_cols):
                lo_c = res * B_WIDTH + c * LANES
                ob_stage[idx * n_cols + c, pl.ds(res, rows, stride=d), :] = o_ref[:, lo_c:lo_c + LANES].astype(F32)
            lse_stage[idx, pl.ds(res, rows, stride=d), :] = l_ref[:, res * LANES:(res + 1) * LANES]
        obs.append(jnp.concatenate([ob_stage[idx * n_cols + c] for c in range(n_cols)], axis=-1))
        lses.append(lse_stage[idx])
    top = functools.reduce(jnp.maximum, lses)
    es = [jnp.exp2(l - top) for l in lses]
    inv = 1.0 / functools.reduce(lambda a, b: a + b, es)
    ob = None
    for e, o in zip(es, obs):
        wgt = e * inv
        wide = jnp.dot(jnp.concatenate(_split_bf16(wgt), axis=-1), expand_ref[...], preferred_element_type=F32)
        ob = wide * o if ob is None else ob + wide * o
    ms = jnp.mean(ob * ob, axis=-1, keepdims=True)
    ob = (ob * lax.rsqrt(ms + RMS_EPS) * norm_b_ref[...]).astype(BF16)
    mix = (jnp.dot(oa_ref[...], w_ref[:A_WIDTH, :], preferred_element_type=F32)
           + jnp.dot(ob, w_ref[A_WIDTH:, :], preferred_element_type=F32))
    y_ref[...] = _layer_norm(DEEPNORM_ALPHA * x_ref[...] + mix, g_ref[...], b_ref[...])


def _mix_out(x, oa, obs, lses, expand, norm_b, w_out, ln_g, ln_b):
    n = x.shape[0]
    row = lambda w: pl.BlockSpec((OUT_TILE, w), lambda i: (i, 0))
    full = lambda r, c: pl.BlockSpec((r, c), lambda i: (0, 0))
    strided = lambda w: [pl.BlockSpec((OUT_TILE // d, d * w), lambda i: (i, 0)) for _, d in B_DILATIONS]
    n_br = len(B_DILATIONS)
    return pl.pallas_call(
        _mix_out_kernel,
        grid=(n // OUT_TILE,),
        in_specs=[row(D_MODEL), row(A_WIDTH), *strided(B_WIDTH), *strided(LANES),
                  full(2 * LANES, B_WIDTH), full(1, B_WIDTH), full(D_MODEL, D_MODEL),
                  full(1, D_MODEL), full(1, D_MODEL)],
        out_specs=row(D_MODEL),
        out_shape=jax.ShapeDtypeStruct((n, D_MODEL), F32),
        scratch_shapes=[pltpu.VMEM((n_br * B_WIDTH // LANES, OUT_TILE, LANES), F32),
                        pltpu.VMEM((n_br, OUT_TILE, LANES), F32)],
        compiler_params=_params("parallel"),
        name="mix_out",
    )(x, oa, *obs, *lses, expand, norm_b, w_out, ln_g, ln_b)


def _silu(v):
    return v / (1.0 + jnp.exp(-v))


def _route(scores, biased):
    def top2_sum(c):
        hi01, lo01 = jnp.maximum(c[0], c[1]), jnp.minimum(c[0], c[1])
        hi23, lo23 = jnp.maximum(c[2], c[3]), jnp.minimum(c[2], c[3])
        first = jnp.maximum(hi01, hi23)
        second = jnp.maximum(jnp.minimum(hi01, hi23), jnp.where(hi01 >= hi23, lo01, lo23))
        return first + second

    groups = [biased[g * EXPERTS_PER_GROUP:(g + 1) * EXPERTS_PER_GROUP] for g in range(N_GROUPS)]
    gscore = [top2_sum(c) for c in groups]
    sel = jnp.zeros_like(gscore[0], dtype=jnp.int32)
    best = gscore[0]
    for g in range(1, N_GROUPS):
        better = gscore[g] > best
        sel = jnp.where(better, g, sel)
        best = jnp.where(better, gscore[g], best)

    def pick(cols):
        out = []
        for j in range(EXPERTS_PER_GROUP):
            v = cols[j]
            for g in range(1, N_GROUPS):
                v = jnp.where(sel == g, cols[g * EXPERTS_PER_GROUP + j], v)
            out.append(v)
        return out

    cb, cs = pick(biased), pick(scores)
    i1 = jnp.zeros_like(sel)
    v1 = cb[0]
    for j in range(1, EXPERTS_PER_GROUP):
        better = cb[j] > v1
        i1 = jnp.where(better, j, i1)
        v1 = jnp.where(better, cb[j], v1)
    i2 = jnp.where(i1 == 0, 1, 0)
    v2 = jnp.where(i1 == 0, cb[1], cb[0])
    for j in range(1, EXPERTS_PER_GROUP):
        better = (cb[j] > v2) & (i1 != j) & (i2 != j)
        i2 = jnp.where(better, j, i2)
        v2 = jnp.where(better, cb[j], v2)
    s1 = sum(jnp.where(i1 == j, cs[j], 0.0) for j in range(EXPERTS_PER_GROUP))
    s2 = sum(jnp.where(i2 == j, cs[j], 0.0) for j in range(EXPERTS_PER_GROUP))
    total = s1 + s2
    gates = []
    for e in range(N_EXPERTS):
        g, j = divmod(e, EXPERTS_PER_GROUP)
        in_group = sel == g
        gate = jnp.where(in_group & (i1 == j), s1 / total, 0.0) + jnp.where(in_group & (i2 == j), s2 / total, 0.0)
        gates.append(gate)
    return gates, sel


def _split_bf16(v):
    hi = v.astype(BF16)
    return hi, (v - hi.astype(F32)).astype(BF16)


def _moe_kernel(x_ref, tri_ref, wr_ref, rb_ref, wg_ref, wu_ref, wd_ref, sg_ref, su_ref, sd_ref,
                g_ref, b_ref, y_ref, xb_ref, xs_ref, gs_ref, ys_ref, pos_ref, off_ref):
    grp = pl.program_id(1)
    tile = MOE_TILE
    n_chunks = tile // SORT_CHUNK

    def perm_rows(c):
        slot = (lax.broadcasted_iota(jnp.int32, (SORT_CHUNK, tile), 0) + c * SORT_CHUNK).astype(F32)
        return jnp.where(pos_ref[...] == slot, 1.0, 0.0).astype(BF16)

    @pl.when(grp == 0)
    def _route_and_sort():
        x = x_ref[...]
        xb = x.astype(BF16)
        xb_ref[...] = xb
        x_lo = (x - xb.astype(F32)).astype(BF16)
        w_hi, w_lo = _split_bf16(wr_ref[...])
        logits = (_dot_nt(jnp.concatenate([w_hi, w_lo], axis=0), xb)
                  + jnp.concatenate([_dot_nt(w_hi, x_lo), jnp.zeros((N_EXPERTS, tile), F32)], axis=0))
        logits = logits[:N_EXPERTS] + logits[N_EXPERTS:]
        scores = 1.0 / (1.0 + jnp.exp(-logits))
        biased = scores + rb_ref[...]
        rows = lambda t: [t[j:j + 1, :] for j in range(N_EXPERTS)]
        gates, sel = _route(rows(scores), rows(biased))
        sub = lax.broadcasted_iota(jnp.int32, (8, tile), 0)
        onehot = jnp.where(sub == sel, 1.0, 0.0).astype(BF16)
        rank = jnp.dot(onehot, tri_ref[...], preferred_element_type=F32)
        last_lane = lax.broadcasted_iota(jnp.int32, (1, tile), 1) == tile - 1
        pos = jnp.zeros((1, tile), F32)
        start = jnp.float32(0.0)
        off_ref[0] = 0
        for g in range(N_GROUPS):
            pos = jnp.where(sel == g, start + rank[g:g + 1, :] - 1.0, pos)
            start = start + jnp.sum(jnp.where(last_lane, rank[g:g + 1, :], 0.0))
            off_ref[g + 1] = start.astype(jnp.int32)
        pos_ref[...] = pos
        gate_rows = jnp.concatenate(gates + [jnp.zeros((LANES - N_EXPERTS, tile), F32)], axis=0)
        g_split = jnp.concatenate(_split_bf16(gate_rows), axis=0)
        for c in range(n_chunks):
            perm = perm_rows(c)
            rows_c = slice(c * SORT_CHUNK, (c + 1) * SORT_CHUNK)
            xs_ref[rows_c, :] = jnp.dot(perm, xb, preferred_element_type=F32).astype(BF16)
            g2 = _dot_nt(perm, g_split)
            gs_ref[rows_c, :] = g2[:, :LANES] + g2[:, LANES:]
        xs_ref[tile:, :] = jnp.zeros((MOE_CHUNK, D_MODEL), BF16)
        gs_ref[tile:, :] = jnp.zeros((MOE_CHUNK, LANES), F32)
        ys_ref[...] = jnp.zeros(ys_ref.shape, F32)

    begin = lax.shift_left(lax.shift_right_logical(off_ref[grp], ROW_ALIGN_LOG2), ROW_ALIGN_LOG2)
    end = off_ref[grp + 1]
    visits = jnp.where(end > off_ref[grp], lax.div(end - begin + (MOE_CHUNK - 1), MOE_CHUNK), 0)
    chunk_lane = lax.broadcasted_iota(jnp.int32, (MOE_CHUNK, LANES), 1)

    def chunk(c, carry):
        r0 = pl.multiple_of(begin + c * MOE_CHUNK, 1 << ROW_ALIGN_LOG2)
        xc = xs_ref[pl.ds(r0, MOE_CHUNK), :]
        gc = gs_ref[pl.ds(r0, MOE_CHUNK), :]
        yc = None
        for j in range(EXPERTS_PER_GROUP):
            gate = jnp.sum(jnp.where(chunk_lane == grp * EXPERTS_PER_GROUP + j, gc, 0.0), axis=-1, keepdims=True)
            h = _silu(jnp.dot(xc, wg_ref[j], preferred_element_type=F32)) * jnp.dot(
                xc, wu_ref[j], preferred_element_type=F32)
            part = jnp.dot((h * gate).astype(BF16), wd_ref[j], preferred_element_type=F32)
            yc = part if yc is None else yc + part
        ys_ref[pl.ds(r0, MOE_CHUNK), :] += yc
        return carry

    lax.fori_loop(0, visits, chunk, 0)

    @pl.when(grp == N_GROUPS - 1)
    def _unsort_and_norm():
        xb = xb_ref[...]
        hs = _silu(jnp.dot(xb, sg_ref[...], preferred_element_type=F32)) * jnp.dot(
            xb, su_ref[...], preferred_element_type=F32)
        acc = jnp.dot(hs.astype(BF16), sd_ref[...], preferred_element_type=F32)
        for c in range(n_chunks):
            acc = acc + lax.dot_general(perm_rows(c), ys_ref[c * SORT_CHUNK:(c + 1) * SORT_CHUNK, :].astype(BF16),
                                        (((0,), (0,)), ((), ())), preferred_element_type=F32)
        y_ref[...] = _layer_norm(DEEPNORM_ALPHA * x_ref[...] + acc, g_ref[...], b_ref[...])


def _moe(x, tri, w_router, router_bias, w_gate, w_up, w_down, ws_gate, ws_up, ws_down, ln_g, ln_b):
    n = x.shape[0]
    row = pl.BlockSpec((MOE_TILE, D_MODEL), lambda i, g: (i, 0))
    full = lambda r, c: pl.BlockSpec((r, c), lambda i, g: (0, 0))
    group_w = lambda r, c: pl.BlockSpec((EXPERTS_PER_GROUP, r, c), lambda i, g: (g, 0, 0))
    return pl.pallas_call(
        _moe_kernel,
        grid=(n // MOE_TILE, N_GROUPS),
        in_specs=[row, full(MOE_TILE, MOE_TILE), full(N_EXPERTS, D_MODEL), full(N_EXPERTS, 1),
                  group_w(D_MODEL, D_EXPERT), group_w(D_MODEL, D_EXPERT), group_w(D_EXPERT, D_MODEL),
                  full(D_MODEL, D_EXPERT), full(D_MODEL, D_EXPERT), full(D_EXPERT, D_MODEL),
                  full(1, D_MODEL), full(1, D_MODEL)],
        out_specs=row,
        out_shape=jax.ShapeDtypeStruct((n, D_MODEL), F32),
        scratch_shapes=[pltpu.VMEM((MOE_TILE, D_MODEL), BF16),
                        pltpu.VMEM((MOE_TILE + MOE_CHUNK, D_MODEL), BF16),
                        pltpu.VMEM((MOE_TILE + MOE_CHUNK, LANES), F32),
                        pltpu.VMEM((MOE_TILE + MOE_CHUNK, D_MODEL), F32),
                        pltpu.VMEM((1, MOE_TILE), F32),
                        pltpu.SMEM((N_GROUPS + 1,), jnp.int32)],
        compiler_params=pltpu.CompilerParams(dimension_semantics=("parallel", "arbitrary"),
                                             vmem_limit_bytes=MOE_VMEM_LIMIT),
        name="moe",
    )(x, tri, w_router, router_bias, w_gate, w_up, w_down, ws_gate, ws_up, ws_down, ln_g, ln_b)


def _alibi_slopes():
    n_heads = A_HEADS + B_HEADS
    h = jnp.arange(1, n_heads + 1, dtype=F32)
    return jnp.exp2(-8.0 * h / n_heads)


def _layer(x, p, l, shape):
    bsz, seq_len = shape
    tokens = bsz * seq_len
    qa, ka, va, qb, kb, vb = _inproj(x, p["w_in"][l])
    seq = lambda t: t.reshape(bsz, seq_len, t.shape[-1])
    oa = _attn_a(seq(qa), seq(ka), seq(va), p["sink2"][l], p["slopes_a"], p["lane_masks"], p["norm_a"][l])
    obs, lses = [], []
    for idx, (window, dilation) in enumerate(B_DILATIONS):
        o, lse = _attn_b_branch(qb[idx], kb[idx], vb[idx], p["slopes_b"], p["lane_masks"], bsz, window, dilation)
        obs.append(o)
        lses.append(lse)
    x = _mix_out(x, oa.reshape(tokens, A_WIDTH), obs, lses, p["expand"], p["norm_b"][l],
                 p["w_out"][l], p["ln1_g"][l], p["ln1_b"][l])
    return _moe(x, p["tri"], p["w_router"], p["router_bias"], p["w_gate"][l], p["w_up"][l], p["w_down"][l],
                p["ws_gate"][l], p["ws_up"][l], p["ws_down"][l], p["ln2_g"][l], p["ln2_b"][l])


def _trunk(x, p):
    bsz, seq_len, _ = x.shape
    h = x.reshape(bsz * seq_len, D_MODEL)
    for l in range(DEPTH):
        h = _layer(h, p, l, (bsz, seq_len))
    return h.reshape(bsz, seq_len, D_MODEL)


def _prepare(w_in, sink_a, norm_a, norm_b, w_out, ln1_g, ln1_b, w_router, router_bias,
             w_gate, w_up, w_down, ws_gate, ws_up, ws_down, ln2_g, ln2_b):
    slopes = _alibi_slopes()
    head_lane = jnp.arange(B_WIDTH)[None, :] // HEAD_DIM
    lane_half = jnp.arange(LANES)[None, :] // HEAD_DIM

    def dup_kv(w):
        h0, h1 = w[..., :HEAD_DIM], w[..., HEAD_DIM:]
        return jnp.concatenate([h0, h0, h1, h1], axis=-1)

    lo = A_WIDTH
    w_proj = jnp.concatenate([w_in[..., :lo], dup_kv(w_in[..., lo:lo + A_KV_WIDTH]),
                              dup_kv(w_in[..., lo + A_KV_WIDTH:lo + 2 * A_KV_WIDTH]),
                              w_in[..., lo + 2 * A_KV_WIDTH:]], axis=-1)
    return {
        "w_in": w_proj.astype(BF16),
        "sink2": sink_a.astype(F32) * LOG2E,
        "slopes_a": slopes[0::2], "slopes_b": slopes[1::2],
        "lane_masks": (lane_half == jnp.arange(2)[:, None]).astype(BF16),
        "expand": (jnp.arange(2 * LANES)[:, None] % LANES == head_lane).astype(BF16),
        "tri": jnp.tri(MOE_TILE, dtype=BF16).T,
        "norm_a": norm_a.reshape(DEPTH, 1, A_WIDTH),
        "norm_b": norm_b.reshape(DEPTH, 1, B_WIDTH),
        "w_out": w_out.astype(BF16),
        "ln1_g": ln1_g.reshape(DEPTH, 1, D_MODEL), "ln1_b": ln1_b.reshape(DEPTH, 1, D_MODEL),
        "w_router": w_router.T, "router_bias": router_bias.reshape(N_EXPERTS, 1),
        "w_gate": w_gate.astype(BF16), "w_up": w_up.astype(BF16), "w_down": w_down.astype(BF16),
        "ws_gate": ws_gate.astype(BF16), "ws_up": ws_up.astype(BF16), "ws_down": ws_down.astype(BF16),
        "ln2_g": ln2_g.reshape(DEPTH, 1, D_MODEL), "ln2_b": ln2_b.reshape(DEPTH, 1, D_MODEL),
    }


def kernel(x_prompt, x_sample, w_in, sink_a, norm_a, norm_b, w_out, ln1_g, ln1_b, w_router, router_bias,
           w_gate, w_up, w_down, ws_gate, ws_up, ws_down, ln2_g, ln2_b):
    p = _prepare(w_in, sink_a, norm_a, norm_b, w_out, ln1_g, ln1_b, w_router, router_bias,
                 w_gate, w_up, w_down, ws_gate, ws_up, ws_down, ln2_g, ln2_b)
    return _trunk(x_prompt, p), _trunk(x_sample, p)
```

```python
import functools

import jax
import jax.numpy as jnp
from jax import lax
from jax.experimental import pallas as pl
from jax.experimental.pallas import tpu as pltpu

F32 = jnp.float32
BF16 = jnp.bfloat16

D_MODEL = 1024
DEPTH = 4
HEAD_DIM = 64
A_HEADS = 8
A_KV_HEADS = 2
A_RADIUS = 128
B_HEADS = 8
B_DILATIONS = ((128, 1), (512, 4), (2048, 16))
A_WIDTH = A_HEADS * HEAD_DIM
A_KV_WIDTH = A_KV_HEADS * HEAD_DIM
B_WIDTH = B_HEADS * HEAD_DIM
N_EXPERTS = 16
N_GROUPS = 4
EXPERTS_PER_GROUP = N_EXPERTS // N_GROUPS
D_EXPERT = 256
DEEPNORM_ALPHA = (2 * DEPTH) ** 0.25
LN_EPS = 1e-5
RMS_EPS = 1e-6
LOG2E = 1.4426950408889634

LANES = 128
MASK_PENALTY = 1e30
VMEM_LIMIT = 48 * 1024 * 1024

Q_TILE = 128
Q_BLOCK = 1024
TILE_UNROLL = 4
PROJ_TILE = 1024
OUT_TILE = 1024
MOE_TILE = 1024
MOE_CHUNK = 320
SORT_CHUNK = 256
ROW_ALIGN_LOG2 = 4
MOE_VMEM_LIMIT = 56 * 1024 * 1024
STRIDED_DILATIONS = tuple(d for _, d in B_DILATIONS if d > 1)
A_KV2_WIDTH = 2 * A_KV_WIDTH
PROJ_WIDTH = A_WIDTH + 2 * A_KV2_WIDTH + 3 * B_WIDTH


def _params(*sem):
    return pltpu.CompilerParams(dimension_semantics=sem, vmem_limit_bytes=VMEM_LIMIT)


def _dot_nt(a, b):
    return lax.dot_general(a, b, (((1,), (1,)), ((), ())), preferred_element_type=F32)


def _inproj_kernel(x_ref, w_ref, qa_ref, ka_ref, va_ref, *rest):
    b_refs, stage_ref = rest[:-1], rest[-1]
    x = x_ref[...].astype(BF16)
    q_scale = HEAD_DIM ** -0.5 * LOG2E

    def proj(lo, width):
        return jnp.dot(x, w_ref[:, lo:lo + width], preferred_element_type=F32)

    qa_ref[...] = (proj(0, A_WIDTH) * q_scale).astype(BF16)
    ka_ref[...] = proj(A_WIDTH, A_KV2_WIDTH).astype(BF16)
    va_ref[...] = proj(A_WIDTH + A_KV2_WIDTH, A_KV2_WIDTH).astype(BF16)
    lo = A_WIDTH + 2 * A_KV2_WIDTH
    n_layouts = 1 + len(STRIDED_DILATIONS)
    for j in range(3):
        r = proj(lo + j * B_WIDTH, B_WIDTH)
        if j == 0:
            r = r * q_scale
        outs = b_refs[j * n_layouts:(j + 1) * n_layouts]
        outs[0][...] = r.astype(BF16)
        n_cols = B_WIDTH // LANES
        for c in range(n_cols):
            stage_ref[c] = r[:, c * LANES:(c + 1) * LANES]
        for d, out in zip(STRIDED_DILATIONS, outs[1:]):
            rows = PROJ_TILE // d
            for res in range(d):
                for c in range(n_cols):
                    lo_c = res * B_WIDTH + c * LANES
                    out[:, lo_c:lo_c + LANES] = stage_ref[c, pl.ds(res, rows, stride=d), :].astype(BF16)


def _inproj(x, w_in):
    n = x.shape[0]
    shapes = [(n, A_WIDTH), (n, A_KV2_WIDTH), (n, A_KV2_WIDTH)]
    blocks = [(PROJ_TILE, A_WIDTH), (PROJ_TILE, A_KV2_WIDTH), (PROJ_TILE, A_KV2_WIDTH)]
    for _ in range(3):
        for d in (1,) + STRIDED_DILATIONS:
            shapes.append((n // d, d * B_WIDTH))
            blocks.append((PROJ_TILE // d, d * B_WIDTH))
    outs = pl.pallas_call(
        _inproj_kernel,
        grid=(n // PROJ_TILE,),
        in_specs=[pl.BlockSpec((PROJ_TILE, D_MODEL), lambda i: (i, 0)),
                  pl.BlockSpec((D_MODEL, PROJ_WIDTH), lambda i: (0, 0), pipeline_mode=pl.Buffered(1))],
        out_specs=[pl.BlockSpec(b, lambda i: (i, 0)) for b in blocks],
        out_shape=[jax.ShapeDtypeStruct(s, BF16) for s in shapes],
        scratch_shapes=[pltpu.VMEM((B_WIDTH // LANES, PROJ_TILE, LANES), F32)],
        compiler_params=_params("parallel"),
        name="inproj",
    )(x, w_in)
    n_layouts = 1 + len(STRIDED_DILATIONS)
    qa, ka, va = outs[:3]
    qb, kb, vb = (outs[3 + j * n_layouts:3 + (j + 1) * n_layouts] for j in range(3))
    return qa, ka, va, qb, kb, vb


def _band_bias_table(slopes, radius, width, dist_scale):
    qi = jnp.arange(Q_TILE)[:, None]
    kj = jnp.arange(width)[None, :]
    cases = []
    for offset in (0, -radius, Q_TILE - width):
        dist = jnp.abs(kj - qi + offset)
        pen = jnp.where(dist <= radius, dist.astype(F32) * (dist_scale * LOG2E), MASK_PENALTY)
        cases.append(-slopes[:, None, None] * pen[None])
    return jnp.stack(cases, axis=0)


def _tile_case(q0, seq_len):
    return jnp.where(q0 == 0, 0, jnp.where(q0 == seq_len - Q_TILE, 2, 1))


def _with_ones(v):
    return jnp.concatenate([v, jnp.ones(v.shape, v.dtype)], axis=-1)


def _attn_a_kernel(sink_ref, mask_ref, norm_ref, bias_ref, q_ref, k_ref, v_ref, o_ref, acc_ref, *, seq_len):
    width = Q_TILE + 2 * A_RADIUS
    blk = pl.program_id(1)
    lane = lax.broadcasted_iota(jnp.int32, (Q_TILE, LANES), 1)

    def tile(t, carry):
        r0 = pl.multiple_of(t * Q_TILE, Q_TILE)
        q0 = blk * Q_BLOCK + r0
        ks = pl.multiple_of(jnp.clip(q0 - A_RADIUS, 0, seq_len - width), Q_TILE)
        case = _tile_case(q0, seq_len)
        for g in range(A_KV_HEADS):
            heads = [(c, half) for c in (2 * g, 2 * g + 1) for half in range(2)]
            q4 = jnp.concatenate([q_ref[0, pl.ds(r0, Q_TILE), c * LANES:(c + 1) * LANES] * mask_ref[half:half + 1, :]
                                  for c, half in heads], axis=0)
            kg = k_ref[0, pl.ds(ks, width), g * LANES:(g + 1) * LANES]
            vg = v_ref[0, pl.ds(ks, width), g * LANES:(g + 1) * LANES]
            s4 = _dot_nt(q4, kg)
            probs, inv = [], []
            for j, (c, half) in enumerate(heads):
                h = 2 * c + half
                sb = s4[j * Q_TILE:(j + 1) * Q_TILE] + bias_ref[case, h]
                sink = sink_ref[h]
                m = jnp.maximum(jnp.max(sb, axis=-1, keepdims=True), sink)
                p = jnp.exp2(sb - m)
                inv.append(1.0 / (jnp.sum(p, axis=-1, keepdims=True) + jnp.exp2(sink - m)))
                probs.append(p.astype(BF16))
            o4 = jnp.dot(jnp.concatenate(probs, axis=0), vg, preferred_element_type=F32)
            for jc, c in enumerate((2 * g, 2 * g + 1)):
                lo = o4[(2 * jc) * Q_TILE:(2 * jc + 1) * Q_TILE] * inv[2 * jc]
                hi = o4[(2 * jc + 1) * Q_TILE:(2 * jc + 2) * Q_TILE] * inv[2 * jc + 1]
                acc_ref[:, c * LANES:(c + 1) * LANES] = jnp.where(lane < HEAD_DIM, lo, hi)
        o = acc_ref[...]
        ms = jnp.mean(o * o, axis=-1, keepdims=True)
        o_ref[0, pl.ds(r0, Q_TILE), :] = (o * lax.rsqrt(ms + RMS_EPS) * norm_ref[...]).astype(BF16)
        return carry

    lax.fori_loop(0, Q_BLOCK // Q_TILE, tile, 0, unroll=TILE_UNROLL)


def _attn_a(qa, ka, va, sinks, slopes, lane_masks, norm_a):
    bsz, seq_len, _ = qa.shape
    width = Q_TILE + 2 * A_RADIUS
    bias = _band_bias_table(slopes, A_RADIUS, width, 1.0)
    kv_spec = pl.BlockSpec((1, seq_len, A_KV2_WIDTH), lambda b, i: (b, 0, 0))
    return pl.pallas_call(
        functools.partial(_attn_a_kernel, seq_len=seq_len),
        grid=(bsz, seq_len // Q_BLOCK),
        in_specs=[pl.BlockSpec(memory_space=pltpu.SMEM),
                  pl.BlockSpec((2, LANES), lambda b, i: (0, 0)),
                  pl.BlockSpec((1, A_WIDTH), lambda b, i: (0, 0)),
                  pl.BlockSpec(bias.shape, lambda b, i: (0, 0, 0, 0), pipeline_mode=pl.Buffered(1)),
                  pl.BlockSpec((1, Q_BLOCK, A_WIDTH), lambda b, i: (b, i, 0)),
                  kv_spec, kv_spec],
        out_specs=pl.BlockSpec((1, Q_BLOCK, A_WIDTH), lambda b, i: (b, i, 0)),
        out_shape=jax.ShapeDtypeStruct((bsz, seq_len, A_WIDTH), BF16),
        scratch_shapes=[pltpu.VMEM((Q_TILE, A_WIDTH), F32)],
        compiler_params=_params("parallel", "arbitrary"),
        name="attn_window",
    )(sinks, lane_masks, norm_a, bias, qa, ka, va)


def _attn_b_kernel(mask_ref, bias_ref, q_ref, k_ref, v_ref, o_ref, lse_ref, *,
                   seq_len, q_block, res_block, radius):
    width = min(Q_TILE + 2 * radius, seq_len)
    mxu_sums = width >= 2 * LANES
    blk = pl.program_id(2)
    lane = lax.broadcasted_iota(jnp.int32, (Q_TILE, LANES), 1)

    for rr in range(res_block):
        base = rr * B_WIDTH

        def tile(t, carry, base=base, rr=rr):
            r0 = pl.multiple_of(t * Q_TILE, Q_TILE)
            q0 = blk * q_block + r0
            ks = pl.multiple_of(jnp.clip(q0 - radius, 0, seq_len - width), radius)
            case = _tile_case(q0, seq_len)
            max_tile = jnp.zeros((Q_TILE, LANES), F32)
            den_tile = jnp.ones((Q_TILE, LANES), F32)
            for c in range(B_WIDTH // LANES):
                cs = slice(base + c * LANES, base + (c + 1) * LANES)
                qc = q_ref[0, pl.ds(r0, Q_TILE), cs]
                kc = k_ref[0, pl.ds(ks, width), cs]
                vc = v_ref[0, pl.ds(ks, width), cs]
                if mxu_sums:
                    vc = _with_ones(vc)
                q2 = jnp.concatenate([qc * mask_ref[0:1, :], qc * mask_ref[1:2, :]], axis=0)
                s2 = _dot_nt(q2, kc)
                probs, maxes, sums = [], [], []
                for half in range(2):
                    s = s2[half * Q_TILE:(half + 1) * Q_TILE] + bias_ref[case, 2 * c + half]
                    m = jnp.max(s, axis=-1, keepdims=True)
                    p = jnp.exp2(s - m)
                    if not mxu_sums:
                        sums.append(jnp.sum(p, axis=-1, keepdims=True))
                    probs.append(p.astype(BF16))
                    maxes.append(m)
                o2 = jnp.dot(jnp.concatenate(probs, axis=0), vc, preferred_element_type=F32)
                if mxu_sums:
                    sums = [o2[:Q_TILE, LANES:], o2[Q_TILE:, LANES:]]
                low = lane < HEAD_DIM
                num = jnp.where(low, o2[:Q_TILE, :LANES], o2[Q_TILE:, :LANES])
                o_ref[0, pl.ds(r0, Q_TILE), cs] = (num / jnp.where(low, sums[0], sums[1])).astype(BF16)
                for half in range(2):
                    here = lane == 2 * c + half
                    max_tile = jnp.where(here, maxes[half], max_tile)
                    den_tile = jnp.where(here, sums[half], den_tile)
            lse_ref[0, pl.ds(r0, Q_TILE), rr * LANES:(rr + 1) * LANES] = max_tile + jnp.log2(den_tile)
            return carry

        lax.fori_loop(0, q_block // Q_TILE, tile, 0, unroll=min(TILE_UNROLL, q_block // Q_TILE))


def _attn_b_branch(qv, kv, vv, slopes, lane_masks, bsz, window, dilation):
    seq_len = qv.shape[0] // bsz
    radius = (window // 2) // dilation
    q_block = min(Q_BLOCK, seq_len)
    res_block = min(dilation, max(1, TILE_UNROLL * Q_TILE // q_block))
    bias = _band_bias_table(slopes, radius, min(Q_TILE + 2 * radius, seq_len), float(dilation))
    view = lambda t: t.reshape(bsz, seq_len, t.shape[-1])
    kv_spec = pl.BlockSpec((1, seq_len, res_block * B_WIDTH), lambda b, r, i: (b, 0, r))
    o, lse = pl.pallas_call(
        functools.partial(_attn_b_kernel, seq_len=seq_len, q_block=q_block, res_block=res_block, radius=radius),
        grid=(bsz, dilation // res_block, seq_len // q_block),
        in_specs=[pl.BlockSpec((2, LANES), lambda b, r, i: (0, 0)),
                  pl.BlockSpec(bias.shape, lambda b, r, i: (0, 0, 0, 0), pipeline_mode=pl.Buffered(1)),
                  pl.BlockSpec((1, q_block, res_block * B_WIDTH), lambda b, r, i: (b, i, r)),
                  kv_spec, kv_spec],
        out_specs=[pl.BlockSpec((1, q_block, res_block * B_WIDTH), lambda b, r, i: (b, i, r)),
                   pl.BlockSpec((1, q_block, res_block * LANES), lambda b, r, i: (b, i, r))],
        out_shape=[jax.ShapeDtypeStruct((bsz, seq_len, dilation * B_WIDTH), BF16),
                   jax.ShapeDtypeStruct((bsz, seq_len, dilation * LANES), F32)],
        compiler_params=_params("parallel", "parallel", "arbitrary"),
        name=f"attn_dilated_{dilation}",
    )(lane_masks, bias, view(qv), view(kv), view(vv))
    return o.reshape(bsz * seq_len, dilation * B_WIDTH), lse.reshape(bsz * seq_len, dilation * LANES)


def _layer_norm(v, g, b):
    mu = jnp.mean(v, axis=-1, keepdims=True)
    c = v - mu
    var = jnp.mean(c * c, axis=-1, keepdims=True)
    return c * lax.rsqrt(var + LN_EPS) * g + b


def _mix_out_kernel(x_ref, oa_ref, *rest):
    n_br = len(B_DILATIONS)
    ob_refs, lse_refs = rest[:n_br], rest[n_br:2 * n_br]
    expand_ref, norm_b_ref, w_ref, g_ref, b_ref, y_ref, ob_stage, lse_stage = rest[2 * n_br:]
    obs, lses = [], []
    for idx, ((_, d), o_ref, l_ref) in enumerate(zip(B_DILATIONS, ob_refs, lse_refs)):
        if d == 1:
            obs.append(o_ref[...].astype(F32))
            lses.append(l_ref[...])
            continue
        rows = OUT_TILE // d
        n_cols = B_WIDTH // LANES
        for res in range(d):
            for c in range(n_cols):
                lo_c = res * B_WIDTH + c * LANES
                ob_stage[idx * n_cols + c, pl.ds(res, rows, stride=d), :] = o_ref[:, lo_c:lo_c + LANES].astype(F32)
            lse_stage[idx, pl.ds(res, rows, stride=d), :] = l_ref[:, res * LANES:(res + 1) * LANES]
        obs.append(jnp.concatenate([ob_stage[idx * n_cols + c] for c in range(n_cols)], axis=-1))
        lses.append(lse_stage[idx])
    top = functools.reduce(jnp.maximum, lses)
    es = [jnp.exp2(l - top) for l in lses]
    inv = 1.0 / functools.reduce(lambda a, b: a + b, es)
    ob = None
    for e, o in zip(es, obs):
        wgt = e * inv
        wide = jnp.dot(jnp.concatenate(_split_bf16(wgt), axis=-1), expand_ref[...], preferred_element_type=F32)
        ob = wide * o if ob is None else ob + wide * o
    ms = jnp.mean(ob * ob, axis=-1, keepdims=True)
    ob = (ob * lax.rsqrt(ms + RMS_EPS) * norm_b_ref[...]).astype(BF16)
    mix = (jnp.dot(oa_ref[...], w_ref[:A_WIDTH, :], preferred_element_type=F32)
           + jnp.dot(ob, w_ref[A_WIDTH:, :], preferred_element_type=F32))
    y_ref[...] = _layer_norm(DEEPNORM_ALPHA * x_ref[...] + mix, g_ref[...], b_ref[...])


def _mix_out(x, oa, obs, lses, expand, norm_b, w_out, ln_g, ln_b):
    n = x.shape[0]
    row = lambda w: pl.BlockSpec((OUT_TILE, w), lambda i: (i, 0))
    full = lambda r, c: pl.BlockSpec((r, c), lambda i: (0, 0), pipeline_mode=pl.Buffered(1))
    strided = lambda w: [pl.BlockSpec((OUT_TILE // d, d * w), lambda i: (i, 0)) for _, d in B_DILATIONS]
    n_br = len(B_DILATIONS)
    return pl.pallas_call(
        _mix_out_kernel,
        grid=(n // OUT_TILE,),
        in_specs=[row(D_MODEL), row(A_WIDTH), *strided(B_WIDTH), *strided(LANES),
                  full(2 * LANES, B_WIDTH), full(1, B_WIDTH), full(D_MODEL, D_MODEL),
                  full(1, D_MODEL), full(1, D_MODEL)],
        out_specs=row(D_MODEL),
        out_shape=jax.ShapeDtypeStruct((n, D_MODEL), F32),
        scratch_shapes=[pltpu.VMEM((n_br * B_WIDTH // LANES, OUT_TILE, LANES), F32),
                        pltpu.VMEM((n_br, OUT_TILE, LANES), F32)],
        compiler_params=_params("parallel"),
        name="mix_out",
    )(x, oa, *obs, *lses, expand, norm_b, w_out, ln_g, ln_b)


def _silu(v):
    return v / (1.0 + jnp.exp(-v))


def _route(scores, biased):
    def top2_sum(c):
        hi01, lo01 = jnp.maximum(c[0], c[1]), jnp.minimum(c[0], c[1])
        hi23, lo23 = jnp.maximum(c[2], c[3]), jnp.minimum(c[2], c[3])
        first = jnp.maximum(hi01, hi23)
        second = jnp.maximum(jnp.minimum(hi01, hi23), jnp.where(hi01 >= hi23, lo01, lo23))
        return first + second

    groups = [biased[g * EXPERTS_PER_GROUP:(g + 1) * EXPERTS_PER_GROUP] for g in range(N_GROUPS)]
    gscore = [top2_sum(c) for c in groups]
    sel = jnp.zeros_like(gscore[0], dtype=jnp.int32)
    best = gscore[0]
    for g in range(1, N_GROUPS):
        better = gscore[g] > best
        sel = jnp.where(better, g, sel)
        best = jnp.where(better, gscore[g], best)

    def pick(cols):
        out = []
        for j in range(EXPERTS_PER_GROUP):
            v = cols[j]
            for g in range(1, N_GROUPS):
                v = jnp.where(sel == g, cols[g * EXPERTS_PER_GROUP + j], v)
            out.append(v)
        return out

    cb, cs = pick(biased), pick(scores)
    i1 = jnp.zeros_like(sel)
    v1 = cb[0]
    for j in range(1, EXPERTS_PER_GROUP):
        better = cb[j] > v1
        i1 = jnp.where(better, j, i1)
        v1 = jnp.where(better, cb[j], v1)
    i2 = jnp.where(i1 == 0, 1, 0)
    v2 = jnp.where(i1 == 0, cb[1], cb[0])
    for j in range(1, EXPERTS_PER_GROUP):
        better = (cb[j] > v2) & (i1 != j) & (i2 != j)
        i2 = jnp.where(better, j, i2)
        v2 = jnp.where(better, cb[j], v2)
    s1 = sum(jnp.where(i1 == j, cs[j], 0.0) for j in range(EXPERTS_PER_GROUP))
    s2 = sum(jnp.where(i2 == j, cs[j], 0.0) for j in range(EXPERTS_PER_GROUP))
    total = s1 + s2
    gates = []
    for e in range(N_EXPERTS):
        g, j = divmod(e, EXPERTS_PER_GROUP)
        in_group = sel == g
        gate = jnp.where(in_group & (i1 == j), s1 / total, 0.0) + jnp.where(in_group & (i2 == j), s2 / total, 0.0)
        gates.append(gate)
    return gates, sel


def _split_bf16(v):
    hi = v.astype(BF16)
    return hi, (v - hi.astype(F32)).astype(BF16)


def _moe_kernel(x_ref, tri_ref, wr_ref, rb_ref, wg_ref, wu_ref, wd_ref, sg_ref, su_ref, sd_ref,
                g_ref, b_ref, y_ref, xb_ref, xs_ref, gs_ref, ys_ref, pos_ref, off_ref):
    grp = pl.program_id(1)
    tile = MOE_TILE
    n_chunks = tile // SORT_CHUNK

    def perm_rows(c):
        slot = (lax.broadcasted_iota(jnp.int32, (SORT_CHUNK, tile), 0) + c * SORT_CHUNK).astype(F32)
        return jnp.where(pos_ref[...] == slot, 1.0, 0.0).astype(BF16)

    @pl.when(grp == 0)
    def _route_and_sort():
        x = x_ref[...]
        xb = x.astype(BF16)
        xb_ref[...] = xb
        x_lo = (x - xb.astype(F32)).astype(BF16)
        w_hi, w_lo = _split_bf16(wr_ref[...])
        logits = (_dot_nt(jnp.concatenate([w_hi, w_lo], axis=0), xb)
                  + jnp.concatenate([_dot_nt(w_hi, x_lo), jnp.zeros((N_EXPERTS, tile), F32)], axis=0))
        logits = logits[:N_EXPERTS] + logits[N_EXPERTS:]
        scores = 1.0 / (1.0 + jnp.exp(-logits))
        biased = scores + rb_ref[...]
        rows = lambda t: [t[j:j + 1, :] for j in range(N_EXPERTS)]
        gates, sel = _route(rows(scores), rows(biased))
        sub = lax.broadcasted_iota(jnp.int32, (8, tile), 0)
        onehot = jnp.where(sub == sel, 1.0, 0.0).astype(BF16)
        rank = jnp.dot(onehot, tri_ref[...], preferred_element_type=F32)
        last_lane = lax.broadcasted_iota(jnp.int32, (1, tile), 1) == tile - 1
        pos = jnp.zeros((1, tile), F32)
        start = jnp.float32(0.0)
        off_ref[0] = 0
        for g in range(N_GROUPS):
            pos = jnp.where(sel == g, start + rank[g:g + 1, :] - 1.0, pos)
            start = start + jnp.sum(jnp.where(last_lane, rank[g:g + 1, :], 0.0))
            off_ref[g + 1] = start.astype(jnp.int32)
        pos_ref[...] = pos
        gate_rows = jnp.concatenate(gates + [jnp.zeros((LANES - N_EXPERTS, tile), F32)], axis=0)
        g_split = jnp.concatenate(_split_bf16(gate_rows), axis=0)
        for c in range(n_chunks):
            perm = perm_rows(c)
            rows_c = slice(c * SORT_CHUNK, (c + 1) * SORT_CHUNK)
            xs_ref[rows_c, :] = jnp.dot(perm, xb, preferred_element_type=F32).astype(BF16)
            g2 = _dot_nt(perm, g_split)
            gs_ref[rows_c, :] = g2[:, :LANES] + g2[:, LANES:]
        xs_ref[tile:, :] = jnp.zeros((MOE_CHUNK, D_MODEL), BF16)
        gs_ref[tile:, :] = jnp.zeros((MOE_CHUNK, LANES), F32)
        ys_ref[...] = jnp.zeros(ys_ref.shape, F32)

    begin = lax.shift_left(lax.shift_right_logical(off_ref[grp], ROW_ALIGN_LOG2), ROW_ALIGN_LOG2)
    end = off_ref[grp + 1]
    visits = jnp.where(end > off_ref[grp], lax.div(end - begin + (MOE_CHUNK - 1), MOE_CHUNK), 0)
    chunk_lane = lax.broadcasted_iota(jnp.int32, (MOE_CHUNK, LANES), 1)

    def chunk(c, carry):
        r0 = pl.multiple_of(begin + c * MOE_CHUNK, 1 << ROW_ALIGN_LOG2)
        xc = xs_ref[pl.ds(r0, MOE_CHUNK), :]
        gc = gs_ref[pl.ds(r0, MOE_CHUNK), :]
        yc = None
        for j in range(EXPERTS_PER_GROUP):
            gate = jnp.sum(jnp.where(chunk_lane == grp * EXPERTS_PER_GROUP + j, gc, 0.0), axis=-1, keepdims=True)
            h = _silu(jnp.dot(xc, wg_ref[j], preferred_element_type=F32)) * jnp.dot(
                xc, wu_ref[j], preferred_element_type=F32)
            part = jnp.dot((h * gate).astype(BF16), wd_ref[j], preferred_element_type=F32)
            yc = part if yc is None else yc + part
        ys_ref[pl.ds(r0, MOE_CHUNK), :] += yc
        return carry

    lax.fori_loop(0, visits, chunk, 0)

    @pl.when(grp == N_GROUPS - 1)
    def _unsort_and_norm():
        xb = xb_ref[...]
        hs = _silu(jnp.dot(xb, sg_ref[...], preferred_element_type=F32)) * jnp.dot(
            xb, su_ref[...], preferred_element_type=F32)
        acc = jnp.dot(hs.astype(BF16), sd_ref[...], preferred_element_type=F32)
        for c in range(n_chunks):
            acc = acc + lax.dot_general(perm_rows(c), ys_ref[c * SORT_CHUNK:(c + 1) * SORT_CHUNK, :].astype(BF16),
                                        (((0,), (0,)), ((), ())), preferred_element_type=F32)
        y_ref[...] = _layer_norm(DEEPNORM_ALPHA * x_ref[...] + acc, g_ref[...], b_ref[...])


def _moe(x, tri, w_router, router_bias, w_gate, w_up, w_down, ws_gate, ws_up, ws_down, ln_g, ln_b):
    n = x.shape[0]
    row = pl.BlockSpec((MOE_TILE, D_MODEL), lambda i, g: (i, 0))
    full = lambda r, c: pl.BlockSpec((r, c), lambda i, g: (0, 0), pipeline_mode=pl.Buffered(1))
    group_w = lambda r, c: pl.BlockSpec((EXPERTS_PER_GROUP, r, c), lambda i, g: (g, 0, 0))
    return pl.pallas_call(
        _moe_kernel,
        grid=(n // MOE_TILE, N_GROUPS),
        in_specs=[row, full(MOE_TILE, MOE_TILE), full(N_EXPERTS, D_MODEL), full(N_EXPERTS, 1),
                  group_w(D_MODEL, D_EXPERT), group_w(D_MODEL, D_EXPERT), group_w(D_EXPERT, D_MODEL),
                  full(D_MODEL, D_EXPERT), full(D_MODEL, D_EXPERT), full(D_EXPERT, D_MODEL),
                  full(1, D_MODEL), full(1, D_MODEL)],
        out_specs=row,
        out_shape=jax.ShapeDtypeStruct((n, D_MODEL), F32),
        scratch_shapes=[pltpu.VMEM((MOE_TILE, D_MODEL), BF16),
                        pltpu.VMEM((MOE_TILE + MOE_CHUNK, D_MODEL), BF16),
                        pltpu.VMEM((MOE_TILE + MOE_CHUNK, LANES), F32),
                        pltpu.VMEM((MOE_TILE + MOE_CHUNK, D_MODEL), F32),
                        pltpu.VMEM((1, MOE_TILE), F32),
                        pltpu.SMEM((N_GROUPS + 1,), jnp.int32)],
        compiler_params=pltpu.CompilerParams(dimension_semantics=("parallel", "arbitrary"),
                                             vmem_limit_bytes=MOE_VMEM_LIMIT),
        name="moe",
    )(x, tri, w_router, router_bias, w_gate, w_up, w_down, ws_gate, ws_up, ws_down, ln_g, ln_b)


def _alibi_slopes():
    n_heads = A_HEADS + B_HEADS
    h = jnp.arange(1, n_heads + 1, dtype=F32)
    return jnp.exp2(-8.0 * h / n_heads)


def _layer(x, p, l, shape):
    bsz, seq_len = shape
    tokens = bsz * seq_len
    qa, ka, va, qb, kb, vb = _inproj(x, p["w_in"][l])
    seq = lambda t: t.reshape(bsz, seq_len, t.shape[-1])
    oa = _attn_a(seq(qa), seq(ka), seq(va), p["sink2"][l], p["slopes_a"], p["lane_masks"], p["norm_a"][l])
    obs, lses = [], []
    for idx, (window, dilation) in enumerate(B_DILATIONS):
        o, lse = _attn_b_branch(qb[idx], kb[idx], vb[idx], p["slopes_b"], p["lane_masks"], bsz, window, dilation)
        obs.append(o)
        lses.append(lse)
    x = _mix_out(x, oa.reshape(tokens, A_WIDTH), obs, lses, p["expand"], p["norm_b"][l],
                 p["w_out"][l], p["ln1_g"][l], p["ln1_b"][l])
    return _moe(x, p["tri"], p["w_router"], p["router_bias"], p["w_gate"][l], p["w_up"][l], p["w_down"][l],
                p["ws_gate"][l], p["ws_up"][l], p["ws_down"][l], p["ln2_g"][l], p["ln2_b"][l])


def _trunk(x, p):
    bsz, seq_len, _ = x.shape
    h = x.reshape(bsz * seq_len, D_MODEL)
    for l in range(DEPTH):
        h = _layer(h, p, l, (bsz, seq_len))
    return h.reshape(bsz, seq_len, D_MODEL)


def _prepare(w_in, sink_a, norm_a, norm_b, w_out, ln1_g, ln1_b, w_router, router_bias,
             w_gate, w_up, w_down, ws_gate, ws_up, ws_down, ln2_g, ln2_b):
    slopes = _alibi_slopes()
    head_lane = jnp.arange(B_WIDTH)[None, :] // HEAD_DIM
    lane_half = jnp.arange(LANES)[None, :] // HEAD_DIM

    def dup_kv(w):
        h0, h1 = w[..., :HEAD_DIM], w[..., HEAD_DIM:]
        return jnp.concatenate([h0, h0, h1, h1], axis=-1)

    lo = A_WIDTH
    w_proj = jnp.concatenate([w_in[..., :lo], dup_kv(w_in[..., lo:lo + A_KV_WIDTH]),
                              dup_kv(w_in[..., lo + A_KV_WIDTH:lo + 2 * A_KV_WIDTH]),
                              w_in[..., lo + 2 * A_KV_WIDTH:]], axis=-1)
    return {
        "w_in": w_proj.astype(BF16),
        "sink2": sink_a.astype(F32) * LOG2E,
        "slopes_a": slopes[0::2], "slopes_b": slopes[1::2],
        "lane_masks": (lane_half == jnp.arange(2)[:, None]).astype(BF16),
        "expand": (jnp.arange(2 * LANES)[:, None] % LANES == head_lane).astype(BF16),
        "tri": jnp.tri(MOE_TILE, dtype=BF16).T,
        "norm_a": norm_a.reshape(DEPTH, 1, A_WIDTH),
        "norm_b": norm_b.reshape(DEPTH, 1, B_WIDTH),
        "w_out": w_out.astype(BF16),
        "ln1_g": ln1_g.reshape(DEPTH, 1, D_MODEL), "ln1_b": ln1_b.reshape(DEPTH, 1, D_MODEL),
        "w_router": w_router.T, "router_bias": router_bias.reshape(N_EXPERTS, 1),
        "w_gate": w_gate.astype(BF16), "w_up": w_up.astype(BF16), "w_down": w_down.astype(BF16),
        "ws_gate": ws_gate.astype(BF16), "ws_up": ws_up.astype(BF16), "ws_down": ws_down.astype(BF16),
        "ln2_g": ln2_g.reshape(DEPTH, 1, D_MODEL), "ln2_b": ln2_b.reshape(DEPTH, 1, D_MODEL),
    }


def kernel(x_prompt, x_sample, w_in, sink_a, norm_a, norm_b, w_out, ln1_g, ln1_b, w_router, router_bias,
           w_gate, w_up, w_down, ws_gate, ws_up, ws_down, ln2_g, ln2_b):
    p = _prepare(w_in, sink_a, norm_a, norm_b, w_out, ln1_g, ln1_b, w_router, router_bias,
                 w_gate, w_up, w_down, ws_gate, ws_up, ws_down, ln2_g, ln2_b)
    return _trunk(x_prompt, p), _trunk(x_sample, p)
```

```python
import functools

import jax
import jax.numpy as jnp
from jax import lax
from jax.experimental import pallas as pl
from jax.experimental.pallas import tpu as pltpu

F32 = jnp.float32
BF16 = jnp.bfloat16

D_MODEL = 1024
DEPTH = 4
HEAD_DIM = 64
A_HEADS = 8
A_KV_HEADS = 2
A_RADIUS = 128
B_HEADS = 8
B_DILATIONS = ((128, 1), (512, 4), (2048, 16))
A_WIDTH = A_HEADS * HEAD_DIM
A_KV_WIDTH = A_KV_HEADS * HEAD_DIM
B_WIDTH = B_HEADS * HEAD_DIM
N_EXPERTS = 16
N_GROUPS = 4
EXPERTS_PER_GROUP = N_EXPERTS // N_GROUPS
D_EXPERT = 256
DEEPNORM_ALPHA = (2 * DEPTH) ** 0.25
LN_EPS = 1e-5
RMS_EPS = 1e-6
LOG2E = 1.4426950408889634

LANES = 128
MASK_PENALTY = 1e30
VMEM_LIMIT = 48 * 1024 * 1024

Q_TILE = 128
Q_BLOCK = 1024
TILE_UNROLL = 4
PROJ_TILE = 1024
OUT_TILE = 1024
MOE_TILE = 1024
MOE_CHUNK = 320
SORT_CHUNK = 256
ROW_ALIGN_LOG2 = 4
MOE_VMEM_LIMIT = 56 * 1024 * 1024
STRIDED_DILATIONS = tuple(d for _, d in B_DILATIONS if d > 1)
A_KV2_WIDTH = 2 * A_KV_WIDTH
PROJ_WIDTH = A_WIDTH + 2 * A_KV2_WIDTH + 3 * B_WIDTH


def _params(*sem):
    return pltpu.CompilerParams(dimension_semantics=sem, vmem_limit_bytes=VMEM_LIMIT)


def _dot_nt(a, b):
    return lax.dot_general(a, b, (((1,), (1,)), ((), ())), preferred_element_type=F32)


def _inproj_kernel(x_ref, w_ref, qa_ref, ka_ref, va_ref, *b_refs):
    q_scale = HEAD_DIM ** -0.5 * LOG2E

    def proj(lo, width):
        return jnp.dot(x_ref[...].astype(BF16), w_ref[:, lo:lo + width], preferred_element_type=F32)

    qa_ref[...] = (proj(0, A_WIDTH) * q_scale).astype(BF16)
    ka_ref[...] = proj(A_WIDTH, A_KV2_WIDTH).astype(BF16)
    va_ref[...] = proj(A_WIDTH + A_KV2_WIDTH, A_KV2_WIDTH).astype(BF16)
    lo = A_WIDTH + 2 * A_KV2_WIDTH
    n_layouts = 1 + len(STRIDED_DILATIONS)
    for j in range(3):
        r = proj(lo + j * B_WIDTH, B_WIDTH)
        if j == 0:
            r = r * q_scale
        outs = b_refs[j * n_layouts:(j + 1) * n_layouts]
        outs[0][...] = r.astype(BF16)
        for d, out in zip(STRIDED_DILATIONS, outs[1:]):
            rows = PROJ_TILE // d
            for c in range(B_WIDTH // LANES):
                blk = r[:, c * LANES:(c + 1) * LANES].reshape(rows, d, LANES)
                blk = pltpu.einshape("mrc->rmc", blk)
                for res in range(d):
                    lo_c = res * B_WIDTH + c * LANES
                    out[:, lo_c:lo_c + LANES] = blk[res].astype(BF16)


def _inproj(x, w_in):
    n = x.shape[0]
    shapes = [(n, A_WIDTH), (n, A_KV2_WIDTH), (n, A_KV2_WIDTH)]
    blocks = [(PROJ_TILE, A_WIDTH), (PROJ_TILE, A_KV2_WIDTH), (PROJ_TILE, A_KV2_WIDTH)]
    for _ in range(3):
        for d in (1,) + STRIDED_DILATIONS:
            shapes.append((n // d, d * B_WIDTH))
            blocks.append((PROJ_TILE // d, d * B_WIDTH))
    outs = pl.pallas_call(
        _inproj_kernel,
        grid=(n // PROJ_TILE,),
        in_specs=[pl.BlockSpec((PROJ_TILE, D_MODEL), lambda i: (i, 0)),
                  pl.BlockSpec((D_MODEL, PROJ_WIDTH), lambda i: (0, 0), pipeline_mode=pl.Buffered(1))],
        out_specs=[pl.BlockSpec(b, lambda i: (i, 0)) for b in blocks],
        out_shape=[jax.ShapeDtypeStruct(s, BF16) for s in shapes],
        compiler_params=_params("parallel"),
        name="inproj",
    )(x, w_in)
    n_layouts = 1 + len(STRIDED_DILATIONS)
    qa, ka, va = outs[:3]
    qb, kb, vb = (outs[3 + j * n_layouts:3 + (j + 1) * n_layouts] for j in range(3))
    return qa, ka, va, qb, kb, vb


def _band_bias_table(slopes, radius, width, dist_scale):
    qi = jnp.arange(Q_TILE)[:, None]
    kj = jnp.arange(width)[None, :]
    cases = []
    for offset in (0, -radius, Q_TILE - width):
        dist = jnp.abs(kj - qi + offset)
        pen = jnp.where(dist <= radius, dist.astype(F32) * (dist_scale * LOG2E), MASK_PENALTY)
        cases.append(-slopes[:, None, None] * pen[None])
    return jnp.stack(cases, axis=0)


def _tile_case(q0, seq_len):
    return jnp.where(q0 == 0, 0, jnp.where(q0 == seq_len - Q_TILE, 2, 1))


def _with_ones(v):
    return jnp.concatenate([v, jnp.ones(v.shape, v.dtype)], axis=-1)


def _attn_a_kernel(sink_ref, mask_ref, norm_ref, bias_ref, q_ref, k_ref, v_ref, o_ref, acc_ref, *, seq_len):
    width = Q_TILE + 2 * A_RADIUS
    blk = pl.program_id(1)
    lane = lax.broadcasted_iota(jnp.int32, (Q_TILE, LANES), 1)

    def tile(t, carry):
        r0 = pl.multiple_of(t * Q_TILE, Q_TILE)
        q0 = blk * Q_BLOCK + r0
        ks = pl.multiple_of(jnp.clip(q0 - A_RADIUS, 0, seq_len - width), Q_TILE)
        case = _tile_case(q0, seq_len)
        for g in range(A_KV_HEADS):
            heads = [(c, half) for c in (2 * g, 2 * g + 1) for half in range(2)]
            q4 = jnp.concatenate([q_ref[0, pl.ds(r0, Q_TILE), c * LANES:(c + 1) * LANES] * mask_ref[half:half + 1, :]
                                  for c, half in heads], axis=0)
            kg = k_ref[0, pl.ds(ks, width), g * LANES:(g + 1) * LANES]
            vg = v_ref[0, pl.ds(ks, width), g * LANES:(g + 1) * LANES]
            s4 = _dot_nt(q4, kg)
            probs, inv = [], []
            for j, (c, half) in enumerate(heads):
                h = 2 * c + half
                sb = s4[j * Q_TILE:(j + 1) * Q_TILE] + bias_ref[case, h]
                sink = sink_ref[h]
                m = jnp.maximum(jnp.max(sb, axis=-1, keepdims=True), sink)
                p = jnp.exp2(sb - m)
                inv.append(1.0 / (jnp.sum(p, axis=-1, keepdims=True) + jnp.exp2(sink - m)))
                probs.append(p.astype(BF16))
            o4 = jnp.dot(jnp.concatenate(probs, axis=0), vg, preferred_element_type=F32)
            for jc, c in enumerate((2 * g, 2 * g + 1)):
                lo = o4[(2 * jc) * Q_TILE:(2 * jc + 1) * Q_TILE] * inv[2 * jc]
                hi = o4[(2 * jc + 1) * Q_TILE:(2 * jc + 2) * Q_TILE] * inv[2 * jc + 1]
                acc_ref[:, c * LANES:(c + 1) * LANES] = jnp.where(lane < HEAD_DIM, lo, hi)
        o = acc_ref[...]
        ms = jnp.mean(o * o, axis=-1, keepdims=True)
        o_ref[0, pl.ds(r0, Q_TILE), :] = (o * lax.rsqrt(ms + RMS_EPS) * norm_ref[...]).astype(BF16)
        return carry

    lax.fori_loop(0, Q_BLOCK // Q_TILE, tile, 0, unroll=TILE_UNROLL)


def _attn_a(qa, ka, va, sinks, slopes, lane_masks, norm_a):
    bsz, seq_len, _ = qa.shape
    width = Q_TILE + 2 * A_RADIUS
    bias = _band_bias_table(slopes, A_RADIUS, width, 1.0)
    kv_spec = pl.BlockSpec((1, seq_len, A_KV2_WIDTH), lambda b, i: (b, 0, 0))
    return pl.pallas_call(
        functools.partial(_attn_a_kernel, seq_len=seq_len),
        grid=(bsz, seq_len // Q_BLOCK),
        in_specs=[pl.BlockSpec(memory_space=pltpu.SMEM),
                  pl.BlockSpec((2, LANES), lambda b, i: (0, 0)),
                  pl.BlockSpec((1, A_WIDTH), lambda b, i: (0, 0)),
                  pl.BlockSpec(bias.shape, lambda b, i: (0, 0, 0, 0), pipeline_mode=pl.Buffered(1)),
                  pl.BlockSpec((1, Q_BLOCK, A_WIDTH), lambda b, i: (b, i, 0)),
                  kv_spec, kv_spec],
        out_specs=pl.BlockSpec((1, Q_BLOCK, A_WIDTH), lambda b, i: (b, i, 0)),
        out_shape=jax.ShapeDtypeStruct((bsz, seq_len, A_WIDTH), BF16),
        scratch_shapes=[pltpu.VMEM((Q_TILE, A_WIDTH), F32)],
        compiler_params=_params("parallel", "arbitrary"),
        name="attn_window",
    )(sinks, lane_masks, norm_a, bias, qa, ka, va)


def _attn_b_kernel(mask_ref, bias_ref, q_ref, k_ref, v_ref, o_ref, lse_ref, *,
                   seq_len, q_block, res_block, radius):
    width = min(Q_TILE + 2 * radius, seq_len)
    mxu_sums = width >= 2 * LANES
    blk = pl.program_id(2)
    lane = lax.broadcasted_iota(jnp.int32, (Q_TILE, LANES), 1)

    for rr in range(res_block):
        base = rr * B_WIDTH

        def tile(t, carry, base=base, rr=rr):
            r0 = pl.multiple_of(t * Q_TILE, Q_TILE)
            q0 = blk * q_block + r0
            ks = pl.multiple_of(jnp.clip(q0 - radius, 0, seq_len - width), radius)
            case = _tile_case(q0, seq_len)
            max_tile = jnp.zeros((Q_TILE, LANES), F32)
            den_tile = jnp.ones((Q_TILE, LANES), F32)
            for c in range(B_WIDTH // LANES):
                cs = slice(base + c * LANES, base + (c + 1) * LANES)
                qc = q_ref[0, pl.ds(r0, Q_TILE), cs]
                kc = k_ref[0, pl.ds(ks, width), cs]
                vc = v_ref[0, pl.ds(ks, width), cs]
                if mxu_sums:
                    vc = _with_ones(vc)
                q2 = jnp.concatenate([qc * mask_ref[0:1, :], qc * mask_ref[1:2, :]], axis=0)
                s2 = _dot_nt(q2, kc)
                probs, maxes, sums = [], [], []
                for half in range(2):
                    s = s2[half * Q_TILE:(half + 1) * Q_TILE] + bias_ref[case, 2 * c + half]
                    m = jnp.max(s, axis=-1, keepdims=True)
                    p = jnp.exp2(s - m)
                    if not mxu_sums:
                        sums.append(jnp.sum(p, axis=-1, keepdims=True))
                    probs.append(p.astype(BF16))
                    maxes.append(m)
                o2 = jnp.dot(jnp.concatenate(probs, axis=0), vc, preferred_element_type=F32)
                if mxu_sums:
                    sums = [o2[:Q_TILE, LANES:], o2[Q_TILE:, LANES:]]
                low = lane < HEAD_DIM
                num = jnp.where(low, o2[:Q_TILE, :LANES], o2[Q_TILE:, :LANES])
                o_ref[0, pl.ds(r0, Q_TILE), cs] = (num / jnp.where(low, sums[0], sums[1])).astype(BF16)
                for half in range(2):
                    here = lane == 2 * c + half
                    max_tile = jnp.where(here, maxes[half], max_tile)
                    den_tile = jnp.where(here, sums[half], den_tile)
            lse_ref[0, pl.ds(r0, Q_TILE), rr * LANES:(rr + 1) * LANES] = max_tile + jnp.log2(den_tile)
            return carry

        lax.fori_loop(0, q_block // Q_TILE, tile, 0, unroll=min(TILE_UNROLL, q_block // Q_TILE))


def _attn_b_branch(qv, kv, vv, slopes, lane_masks, bsz, window, dilation):
    seq_len = qv.shape[0] // bsz
    radius = (window // 2) // dilation
    q_block = min(Q_BLOCK, seq_len)
    res_block = min(dilation, max(1, TILE_UNROLL * Q_TILE // q_block))
    bias = _band_bias_table(slopes, radius, min(Q_TILE + 2 * radius, seq_len), float(dilation))
    view = lambda t: t.reshape(bsz, seq_len, t.shape[-1])
    kv_spec = pl.BlockSpec((1, seq_len, res_block * B_WIDTH), lambda b, r, i: (b, 0, r))
    o, lse = pl.pallas_call(
        functools.partial(_attn_b_kernel, seq_len=seq_len, q_block=q_block, res_block=res_block, radius=radius),
        grid=(bsz, dilation // res_block, seq_len // q_block),
        in_specs=[pl.BlockSpec((2, LANES), lambda b, r, i: (0, 0)),
                  pl.BlockSpec(bias.shape, lambda b, r, i: (0, 0, 0, 0), pipeline_mode=pl.Buffered(1)),
                  pl.BlockSpec((1, q_block, res_block * B_WIDTH), lambda b, r, i: (b, i, r)),
                  kv_spec, kv_spec],
        out_specs=[pl.BlockSpec((1, q_block, res_block * B_WIDTH), lambda b, r, i: (b, i, r)),
                   pl.BlockSpec((1, q_block, res_block * LANES), lambda b, r, i: (b, i, r))],
        out_shape=[jax.ShapeDtypeStruct((bsz, seq_len, dilation * B_WIDTH), BF16),
                   jax.ShapeDtypeStruct((bsz, seq_len, dilation * LANES), F32)],
        compiler_params=_params("parallel", "parallel", "arbitrary"),
        name=f"attn_dilated_{dilation}",
    )(lane_masks, bias, view(qv), view(kv), view(vv))
    return o.reshape(bsz * seq_len, dilation * B_WIDTH), lse.reshape(bsz * seq_len, dilation * LANES)


def _layer_norm(v, g, b):
    mu = jnp.mean(v, axis=-1, keepdims=True)
    c = v - mu
    var = jnp.mean(c * c, axis=-1, keepdims=True)
    return c * lax.rsqrt(var + LN_EPS) * g + b


def _mix_out_kernel(x_ref, oa_ref, *rest):
    n_br = len(B_DILATIONS)
    ob_refs, lse_refs = rest[:n_br], rest[n_br:2 * n_br]
    expand_ref, norm_b_ref, w_ref, g_ref, b_ref, y_ref, ob_stage, lse_stage = rest[2 * n_br:]
    obs, lses = [], []
    for idx, ((_, d), o_ref, l_ref) in enumerate(zip(B_DILATIONS, ob_refs, lse_refs)):
        if d == 1:
            obs.append(o_ref[...].astype(F32))
            lses.append(l_ref[...])
            continue
        rows = OUT_TILE // d
        n_cols = B_WIDTH // LANES
        for res in range(d):
            for c in range(n_cols):
                lo_c = res * B_WIDTH + c * LANES
                ob_stage[idx * n_cols + c, pl.ds(res, rows, stride=d), :] = o_ref[:, lo_c:lo_c + LANES].astype(F32)
            lse_stage[idx, pl.ds(res, rows, stride=d), :] = l_ref[:, res * LANES:(res + 1) * LANES]
        obs.append(jnp.concatenate([ob_stage[idx * n_cols + c] for c in range(n_cols)], axis=-1))
        lses.append(lse_stage[idx])
    top = functools.reduce(jnp.maximum, lses)
    es = [jnp.exp2(l - top) for l in lses]
    inv = 1.0 / functools.reduce(lambda a, b: a + b, es)
    ob = None
    for e, o in zip(es, obs):
        wgt = e * inv
        wide = jnp.dot(jnp.concatenate(_split_bf16(wgt), axis=-1), expand_ref[...], preferred_element_type=F32)
        ob = wide * o if ob is None else ob + wide * o
    ms = jnp.mean(ob * ob, axis=-1, keepdims=True)
    ob = (ob * lax.rsqrt(ms + RMS_EPS) * norm_b_ref[...]).astype(BF16)
    mix = (jnp.dot(oa_ref[...], w_ref[:A_WIDTH, :], preferred_element_type=F32)
           + jnp.dot(ob, w_ref[A_WIDTH:, :], preferred_element_type=F32))
    y_ref[...] = _layer_norm(DEEPNORM_ALPHA * x_ref[...] + mix, g_ref[...], b_ref[...])


def _mix_out(x, oa, obs, lses, expand, norm_b, w_out, ln_g, ln_b):
    n = x.shape[0]
    row = lambda w: pl.BlockSpec((OUT_TILE, w), lambda i: (i, 0))
    full = lambda r, c: pl.BlockSpec((r, c), lambda i: (0, 0), pipeline_mode=pl.Buffered(1))
    strided = lambda w: [pl.BlockSpec((OUT_TILE // d, d * w), lambda i: (i, 0)) for _, d in B_DILATIONS]
    n_br = len(B_DILATIONS)
    return pl.pallas_call(
        _mix_out_kernel,
        grid=(n // OUT_TILE,),
        in_specs=[row(D_MODEL), row(A_WIDTH), *strided(B_WIDTH), *strided(LANES),
                  full(2 * LANES, B_WIDTH), full(1, B_WIDTH), full(D_MODEL, D_MODEL),
                  full(1, D_MODEL), full(1, D_MODEL)],
        out_specs=row(D_MODEL),
        out_shape=jax.ShapeDtypeStruct((n, D_MODEL), F32),
        scratch_shapes=[pltpu.VMEM((n_br * B_WIDTH // LANES, OUT_TILE, LANES), F32),
                        pltpu.VMEM((n_br, OUT_TILE, LANES), F32)],
        compiler_params=_params("parallel"),
        name="mix_out",
    )(x, oa, *obs, *lses, expand, norm_b, w_out, ln_g, ln_b)


def _silu(v):
    return v / (1.0 + jnp.exp(-v))


def _route(scores, biased):
    def top2_sum(c):
        hi01, lo01 = jnp.maximum(c[0], c[1]), jnp.minimum(c[0], c[1])
        hi23, lo23 = jnp.maximum(c[2], c[3]), jnp.minimum(c[2], c[3])
        first = jnp.maximum(hi01, hi23)
        second = jnp.maximum(jnp.minimum(hi01, hi23), jnp.where(hi01 >= hi23, lo01, lo23))
        return first + second

    groups = [biased[g * EXPERTS_PER_GROUP:(g + 1) * EXPERTS_PER_GROUP] for g in range(N_GROUPS)]
    gscore = [top2_sum(c) for c in groups]
    sel = jnp.zeros_like(gscore[0], dtype=jnp.int32)
    best = gscore[0]
    for g in range(1, N_GROUPS):
        better = gscore[g] > best
        sel = jnp.where(better, g, sel)
        best = jnp.where(better, gscore[g], best)

    def pick(cols):
        out = []
        for j in range(EXPERTS_PER_GROUP):
            v = cols[j]
            for g in range(1, N_GROUPS):
                v = jnp.where(sel == g, cols[g * EXPERTS_PER_GROUP + j], v)
            out.append(v)
        return out

    cb, cs = pick(biased), pick(scores)
    i1 = jnp.zeros_like(sel)
    v1 = cb[0]
    for j in range(1, EXPERTS_PER_GROUP):
        better = cb[j] > v1
        i1 = jnp.where(better, j, i1)
        v1 = jnp.where(better, cb[j], v1)
    i2 = jnp.where(i1 == 0, 1, 0)
    v2 = jnp.where(i1 == 0, cb[1], cb[0])
    for j in range(1, EXPERTS_PER_GROUP):
        better = (cb[j] > v2) & (i1 != j) & (i2 != j)
        i2 = jnp.where(better, j, i2)
        v2 = jnp.where(better, cb[j], v2)
    s1 = sum(jnp.where(i1 == j, cs[j], 0.0) for j in range(EXPERTS_PER_GROUP))
    s2 = sum(jnp.where(i2 == j, cs[j], 0.0) for j in range(EXPERTS_PER_GROUP))
    total = s1 + s2
    gates = []
    for e in range(N_EXPERTS):
        g, j = divmod(e, EXPERTS_PER_GROUP)
        in_group = sel == g
        gate = jnp.where(in_group & (i1 == j), s1 / total, 0.0) + jnp.where(in_group & (i2 == j), s2 / total, 0.0)
        gates.append(gate)
    return gates, sel


def _split_bf16(v):
    hi = v.astype(BF16)
    return hi, (v - hi.astype(F32)).astype(BF16)


def _moe_kernel(x_ref, tri_ref, wr_ref, rb_ref, wg_ref, wu_ref, wd_ref, sg_ref, su_ref, sd_ref,
                g_ref, b_ref, y_ref, yb_ref, xb_ref, xs_ref, gs_ref, ys_ref, pos_ref, off_ref):
    grp = pl.program_id(1)
    tile = MOE_TILE
    n_chunks = tile // SORT_CHUNK

    def perm_rows(c):
        slot = (lax.broadcasted_iota(jnp.int32, (SORT_CHUNK, tile), 0) + c * SORT_CHUNK).astype(F32)
        return jnp.where(pos_ref[...] == slot, 1.0, 0.0).astype(BF16)

    @pl.when(grp == 0)
    def _route_and_sort():
        x = x_ref[...]
        xb = x.astype(BF16)
        xb_ref[...] = xb
        x_lo = (x - xb.astype(F32)).astype(BF16)
        w_hi, w_lo = _split_bf16(wr_ref[...])
        logits = (_dot_nt(jnp.concatenate([w_hi, w_lo], axis=0), xb)
                  + jnp.concatenate([_dot_nt(w_hi, x_lo), jnp.zeros((N_EXPERTS, tile), F32)], axis=0))
        logits = logits[:N_EXPERTS] + logits[N_EXPERTS:]
        scores = 1.0 / (1.0 + jnp.exp(-logits))
        biased = scores + rb_ref[...]
        rows = lambda t: [t[j:j + 1, :] for j in range(N_EXPERTS)]
        gates, sel = _route(rows(scores), rows(biased))
        sub = lax.broadcasted_iota(jnp.int32, (8, tile), 0)
        onehot = jnp.where(sub == sel, 1.0, 0.0).astype(BF16)
        rank = jnp.dot(onehot, tri_ref[...], preferred_element_type=F32)
        last_lane = lax.broadcasted_iota(jnp.int32, (1, tile), 1) == tile - 1
        pos = jnp.zeros((1, tile), F32)
        start = jnp.float32(0.0)
        off_ref[0] = 0
        for g in range(N_GROUPS):
            pos = jnp.where(sel == g, start + rank[g:g + 1, :] - 1.0, pos)
            start = start + jnp.sum(jnp.where(last_lane, rank[g:g + 1, :], 0.0))
            off_ref[g + 1] = start.astype(jnp.int32)
        pos_ref[...] = pos
        gate_rows = jnp.concatenate(gates + [jnp.zeros((LANES - N_EXPERTS, tile), F32)], axis=0)
        g_split = jnp.concatenate(_split_bf16(gate_rows), axis=0)
        for c in range(n_chunks):
            perm = perm_rows(c)
            rows_c = slice(c * SORT_CHUNK, (c + 1) * SORT_CHUNK)
            xs_ref[rows_c, :] = jnp.dot(perm, xb, preferred_element_type=F32).astype(BF16)
            g2 = _dot_nt(perm, g_split)
            gs_ref[rows_c, :] = g2[:, :LANES] + g2[:, LANES:]
        xs_ref[tile:, :] = jnp.zeros((MOE_CHUNK, D_MODEL), BF16)
        gs_ref[tile:, :] = jnp.zeros((MOE_CHUNK, LANES), F32)
        ys_ref[...] = jnp.zeros(ys_ref.shape, F32)

    begin = lax.shift_left(lax.shift_right_logical(off_ref[grp], ROW_ALIGN_LOG2), ROW_ALIGN_LOG2)
    end = off_ref[grp + 1]
    visits = jnp.where(end > off_ref[grp], lax.div(end - begin + (MOE_CHUNK - 1), MOE_CHUNK), 0)
    chunk_lane = lax.broadcasted_iota(jnp.int32, (MOE_CHUNK, LANES), 1)

    def chunk(c, carry):
        r0 = pl.multiple_of(begin + c * MOE_CHUNK, 1 << ROW_ALIGN_LOG2)
        xc = xs_ref[pl.ds(r0, MOE_CHUNK), :]
        gc = gs_ref[pl.ds(r0, MOE_CHUNK), :]
        yc = None
        for j in range(EXPERTS_PER_GROUP):
            gate = jnp.sum(jnp.where(chunk_lane == grp * EXPERTS_PER_GROUP + j, gc, 0.0), axis=-1, keepdims=True)
            h = _silu(jnp.dot(xc, wg_ref[j], preferred_element_type=F32)) * jnp.dot(
                xc, wu_ref[j], preferred_element_type=F32)
            part = jnp.dot((h * gate).astype(BF16), wd_ref[j], preferred_element_type=F32)
            yc = part if yc is None else yc + part
        ys_ref[pl.ds(r0, MOE_CHUNK), :] += yc
        return carry

    lax.fori_loop(0, visits, chunk, 0)

    @pl.when(grp == N_GROUPS - 1)
    def _unsort_and_norm():
        xb = xb_ref[...]
        hs = _silu(jnp.dot(xb, sg_ref[...], preferred_element_type=F32)) * jnp.dot(
            xb, su_ref[...], preferred_element_type=F32)
        acc = jnp.dot(hs.astype(BF16), sd_ref[...], preferred_element_type=F32)
        for c in range(n_chunks):
            acc = acc + lax.dot_general(perm_rows(c), ys_ref[c * SORT_CHUNK:(c + 1) * SORT_CHUNK, :].astype(BF16),
                                        (((0,), (0,)), ((), ())), preferred_element_type=F32)
        y = _layer_norm(DEEPNORM_ALPHA * x_ref[...] + acc, g_ref[...], b_ref[...])
        y_ref[...] = y
        yb_ref[...] = y.astype(BF16)


def _moe(x, tri, w_router, router_bias, w_gate, w_up, w_down, ws_gate, ws_up, ws_down, ln_g, ln_b):
    n = x.shape[0]
    row = pl.BlockSpec((MOE_TILE, D_MODEL), lambda i, g: (i, 0))
    full = lambda r, c: pl.BlockSpec((r, c), lambda i, g: (0, 0), pipeline_mode=pl.Buffered(1))
    group_w = lambda r, c: pl.BlockSpec((EXPERTS_PER_GROUP, r, c), lambda i, g: (g, 0, 0))
    return pl.pallas_call(
        _moe_kernel,
        grid=(n // MOE_TILE, N_GROUPS),
        in_specs=[row, full(MOE_TILE, MOE_TILE), full(N_EXPERTS, D_MODEL), full(N_EXPERTS, 1),
                  group_w(D_MODEL, D_EXPERT), group_w(D_MODEL, D_EXPERT), group_w(D_EXPERT, D_MODEL),
                  full(D_MODEL, D_EXPERT), full(D_MODEL, D_EXPERT), full(D_EXPERT, D_MODEL),
                  full(1, D_MODEL), full(1, D_MODEL)],
        out_specs=[row, row],
        out_shape=[jax.ShapeDtypeStruct((n, D_MODEL), F32), jax.ShapeDtypeStruct((n, D_MODEL), BF16)],
        scratch_shapes=[pltpu.VMEM((MOE_TILE, D_MODEL), BF16),
                        pltpu.VMEM((MOE_TILE + MOE_CHUNK, D_MODEL), BF16),
                        pltpu.VMEM((MOE_TILE + MOE_CHUNK, LANES), F32),
                        pltpu.VMEM((MOE_TILE + MOE_CHUNK, D_MODEL), F32),
                        pltpu.VMEM((1, MOE_TILE), F32),
                        pltpu.SMEM((N_GROUPS + 1,), jnp.int32)],
        compiler_params=pltpu.CompilerParams(dimension_semantics=("parallel", "arbitrary"),
                                             vmem_limit_bytes=MOE_VMEM_LIMIT),
        name="moe",
    )(x, tri, w_router, router_bias, w_gate, w_up, w_down, ws_gate, ws_up, ws_down, ln_g, ln_b)


def _alibi_slopes():
    n_heads = A_HEADS + B_HEADS
    h = jnp.arange(1, n_heads + 1, dtype=F32)
    return jnp.exp2(-8.0 * h / n_heads)


def _layer(x, xb, p, l, shape):
    bsz, seq_len = shape
    tokens = bsz * seq_len
    qa, ka, va, qb, kb, vb = _inproj(xb, p["w_in"][l])
    seq = lambda t: t.reshape(bsz, seq_len, t.shape[-1])
    oa = _attn_a(seq(qa), seq(ka), seq(va), p["sink2"][l], p["slopes_a"], p["lane_masks"], p["norm_a"][l])
    obs, lses = [], []
    for idx, (window, dilation) in enumerate(B_DILATIONS):
        o, lse = _attn_b_branch(qb[idx], kb[idx], vb[idx], p["slopes_b"], p["lane_masks"], bsz, window, dilation)
        obs.append(o)
        lses.append(lse)
    x = _mix_out(x, oa.reshape(tokens, A_WIDTH), obs, lses, p["expand"], p["norm_b"][l],
                 p["w_out"][l], p["ln1_g"][l], p["ln1_b"][l])
    return _moe(x, p["tri"], p["w_router"], p["router_bias"], p["w_gate"][l], p["w_up"][l], p["w_down"][l],
                p["ws_gate"][l], p["ws_up"][l], p["ws_down"][l], p["ln2_g"][l], p["ln2_b"][l])


def _trunk(x, p):
    bsz, seq_len, _ = x.shape
    h = x.reshape(bsz * seq_len, D_MODEL)
    hb = h
    for l in range(DEPTH):
        h, hb = _layer(h, hb, p, l, (bsz, seq_len))
    return h.reshape(bsz, seq_len, D_MODEL)


def _prepare(w_in, sink_a, norm_a, norm_b, w_out, ln1_g, ln1_b, w_router, router_bias,
             w_gate, w_up, w_down, ws_gate, ws_up, ws_down, ln2_g, ln2_b):
    slopes = _alibi_slopes()
    head_lane = jnp.arange(B_WIDTH)[None, :] // HEAD_DIM
    lane_half = jnp.arange(LANES)[None, :] // HEAD_DIM

    def dup_kv(w):
        h0, h1 = w[..., :HEAD_DIM], w[..., HEAD_DIM:]
        return jnp.concatenate([h0, h0, h1, h1], axis=-1)

    lo = A_WIDTH
    w_proj = jnp.concatenate([w_in[..., :lo], dup_kv(w_in[..., lo:lo + A_KV_WIDTH]),
                              dup_kv(w_in[..., lo + A_KV_WIDTH:lo + 2 * A_KV_WIDTH]),
                              w_in[..., lo + 2 * A_KV_WIDTH:]], axis=-1)
    return {
        "w_in": w_proj.astype(BF16),
        "sink2": sink_a.astype(F32) * LOG2E,
        "slopes_a": slopes[0::2], "slopes_b": slopes[1::2],
        "lane_masks": (lane_half == jnp.arange(2)[:, None]).astype(BF16),
        "expand": (jnp.arange(2 * LANES)[:, None] % LANES == head_lane).astype(BF16),
        "tri": jnp.tri(MOE_TILE, dtype=BF16).T,
        "norm_a": norm_a.reshape(DEPTH, 1, A_WIDTH),
        "norm_b": norm_b.reshape(DEPTH, 1, B_WIDTH),
        "w_out": w_out.astype(BF16),
        "ln1_g": ln1_g.reshape(DEPTH, 1, D_MODEL), "ln1_b": ln1_b.reshape(DEPTH, 1, D_MODEL),
        "w_router": w_router.T, "router_bias": router_bias.reshape(N_EXPERTS, 1),
        "w_gate": w_gate.astype(BF16), "w_up": w_up.astype(BF16), "w_down": w_down.astype(BF16),
        "ws_gate": ws_gate.astype(BF16), "ws_up": ws_up.astype(BF16), "ws_down": ws_down.astype(BF16),
        "ln2_g": ln2_g.reshape(DEPTH, 1, D_MODEL), "ln2_b": ln2_b.reshape(DEPTH, 1, D_MODEL),
    }


def kernel(x_prompt, x_sample, w_in, sink_a, norm_a, norm_b, w_out, ln1_g, ln1_b, w_router, router_bias,
           w_gate, w_up, w_down, ws_gate, ws_up, ws_down, ln2_g, ln2_b):
    p = _prepare(w_in, sink_a, norm_a, norm_b, w_out, ln1_g, ln1_b, w_router, router_bias,
                 w_gate, w_up, w_down, ws_gate, ws_up, ws_down, ln2_g, ln2_b)
    return _trunk(x_prompt, p), _trunk(x_sample, p)
```

```python
import functools

import jax
import jax.numpy as jnp
from jax import lax
from jax.experimental import pallas as pl
from jax.experimental.pallas import tpu as pltpu

F32 = jnp.float32
BF16 = jnp.bfloat16

D_MODEL = 1024
DEPTH = 4
HEAD_DIM = 64
A_HEADS = 8
A_KV_HEADS = 2
A_RADIUS = 128
B_HEADS = 8
B_DILATIONS = ((128, 1), (512, 4), (2048, 16))
A_WIDTH = A_HEADS * HEAD_DIM
A_KV_WIDTH = A_KV_HEADS * HEAD_DIM
B_WIDTH = B_HEADS * HEAD_DIM
N_EXPERTS = 16
N_GROUPS = 4
EXPERTS_PER_GROUP = N_EXPERTS // N_GROUPS
D_EXPERT = 256
DEEPNORM_ALPHA = (2 * DEPTH) ** 0.25
LN_EPS = 1e-5
RMS_EPS = 1e-6
LOG2E = 1.4426950408889634

LANES = 128
MASK_PENALTY = 1e30
VMEM_LIMIT = 48 * 1024 * 1024

Q_TILE = 128
Q_BLOCK = 1024
TILE_UNROLL = 4
PROJ_TILE = 1024
OUT_TILE = 1024
MOE_TILE = 1024
MOE_SORT = 512
MOE_CHUNK = 160
SORT_CHUNK = 256
ROW_ALIGN_LOG2 = 4
MOE_VMEM_LIMIT = 56 * 1024 * 1024
STRIDED_DILATIONS = tuple(d for _, d in B_DILATIONS if d > 1)
A_KV2_WIDTH = 2 * A_KV_WIDTH
PROJ_WIDTH = A_WIDTH + 2 * A_KV2_WIDTH + 3 * B_WIDTH


def _params(*sem):
    return pltpu.CompilerParams(dimension_semantics=sem, vmem_limit_bytes=VMEM_LIMIT)


def _dot_nt(a, b):
    return lax.dot_general(a, b, (((1,), (1,)), ((), ())), preferred_element_type=F32)


def _inproj_kernel(x_ref, w_ref, qa_ref, ka_ref, va_ref, *b_refs):
    q_scale = HEAD_DIM ** -0.5 * LOG2E

    def proj(lo, width):
        return jnp.dot(x_ref[...].astype(BF16), w_ref[:, lo:lo + width], preferred_element_type=F32)

    qa_ref[...] = (proj(0, A_WIDTH) * q_scale).astype(BF16)
    ka_ref[...] = proj(A_WIDTH, A_KV2_WIDTH).astype(BF16)
    va_ref[...] = proj(A_WIDTH + A_KV2_WIDTH, A_KV2_WIDTH).astype(BF16)
    lo = A_WIDTH + 2 * A_KV2_WIDTH
    n_layouts = 1 + len(STRIDED_DILATIONS)
    for j in range(3):
        r = proj(lo + j * B_WIDTH, B_WIDTH)
        if j == 0:
            r = r * q_scale
        outs = b_refs[j * n_layouts:(j + 1) * n_layouts]
        outs[0][...] = r.astype(BF16)
        for d, out in zip(STRIDED_DILATIONS, outs[1:]):
            rows = PROJ_TILE // d
            for c in range(B_WIDTH // LANES):
                blk = r[:, c * LANES:(c + 1) * LANES].reshape(rows, d, LANES)
                blk = pltpu.einshape("mrc->rmc", blk)
                for res in range(d):
                    lo_c = res * B_WIDTH + c * LANES
                    out[:, lo_c:lo_c + LANES] = blk[res].astype(BF16)


def _inproj(x, w_in):
    n = x.shape[0]
    shapes = [(n, A_WIDTH), (n, A_KV2_WIDTH), (n, A_KV2_WIDTH)]
    blocks = [(PROJ_TILE, A_WIDTH), (PROJ_TILE, A_KV2_WIDTH), (PROJ_TILE, A_KV2_WIDTH)]
    for _ in range(3):
        for d in (1,) + STRIDED_DILATIONS:
            shapes.append((n // d, d * B_WIDTH))
            blocks.append((PROJ_TILE // d, d * B_WIDTH))
    outs = pl.pallas_call(
        _inproj_kernel,
        grid=(n // PROJ_TILE,),
        in_specs=[pl.BlockSpec((PROJ_TILE, D_MODEL), lambda i: (i, 0)),
                  pl.BlockSpec((D_MODEL, PROJ_WIDTH), lambda i: (0, 0), pipeline_mode=pl.Buffered(1))],
        out_specs=[pl.BlockSpec(b, lambda i: (i, 0)) for b in blocks],
        out_shape=[jax.ShapeDtypeStruct(s, BF16) for s in shapes],
        compiler_params=_params("parallel"),
        name="inproj",
    )(x, w_in)
    n_layouts = 1 + len(STRIDED_DILATIONS)
    qa, ka, va = outs[:3]
    qb, kb, vb = (outs[3 + j * n_layouts:3 + (j + 1) * n_layouts] for j in range(3))
    return qa, ka, va, qb, kb, vb


def _band_bias_table(slopes, radius, width, dist_scale):
    qi = jnp.arange(Q_TILE)[:, None]
    kj = jnp.arange(width)[None, :]
    cases = []
    for offset in (0, -radius, Q_TILE - width):
        dist = jnp.abs(kj - qi + offset)
        pen = jnp.where(dist <= radius, dist.astype(F32) * (dist_scale * LOG2E), MASK_PENALTY)
        cases.append(-slopes[:, None, None] * pen[None])
    return jnp.stack(cases, axis=0)


def _tile_case(q0, seq_len):
    return jnp.where(q0 == 0, 0, jnp.where(q0 == seq_len - Q_TILE, 2, 1))


def _with_ones(v):
    return jnp.concatenate([v, jnp.ones(v.shape, v.dtype)], axis=-1)


def _attn_a_kernel(sink_ref, mask_ref, norm_ref, bias_ref, q_ref, k_ref, v_ref, o_ref, acc_ref, *, seq_len):
    width = Q_TILE + 2 * A_RADIUS
    blk = pl.program_id(1)
    lane = lax.broadcasted_iota(jnp.int32, (Q_TILE, LANES), 1)

    def tile(t, carry):
        r0 = pl.multiple_of(t * Q_TILE, Q_TILE)
        q0 = blk * Q_BLOCK + r0
        ks = pl.multiple_of(jnp.clip(q0 - A_RADIUS, 0, seq_len - width), Q_TILE)
        case = _tile_case(q0, seq_len)
        for g in range(A_KV_HEADS):
            heads = [(c, half) for c in (2 * g, 2 * g + 1) for half in range(2)]
            q4 = jnp.concatenate([q_ref[0, pl.ds(r0, Q_TILE), c * LANES:(c + 1) * LANES] * mask_ref[half:half + 1, :]
                                  for c, half in heads], axis=0)
            kg = k_ref[0, pl.ds(ks, width), g * LANES:(g + 1) * LANES]
            vg = v_ref[0, pl.ds(ks, width), g * LANES:(g + 1) * LANES]
            s4 = _dot_nt(q4, kg)
            probs, inv = [], []
            for j, (c, half) in enumerate(heads):
                h = 2 * c + half
                sb = s4[j * Q_TILE:(j + 1) * Q_TILE] + bias_ref[case, h]
                sink = sink_ref[h]
                m = jnp.maximum(jnp.max(sb, axis=-1, keepdims=True), sink)
                p = jnp.exp2(sb - m)
                inv.append(1.0 / (jnp.sum(p, axis=-1, keepdims=True) + jnp.exp2(sink - m)))
                probs.append(p.astype(BF16))
            o4 = jnp.dot(jnp.concatenate(probs, axis=0), vg, preferred_element_type=F32)
            for jc, c in enumerate((2 * g, 2 * g + 1)):
                lo = o4[(2 * jc) * Q_TILE:(2 * jc + 1) * Q_TILE] * inv[2 * jc]
                hi = o4[(2 * jc + 1) * Q_TILE:(2 * jc + 2) * Q_TILE] * inv[2 * jc + 1]
                acc_ref[:, c * LANES:(c + 1) * LANES] = jnp.where(lane < HEAD_DIM, lo, hi)
        o = acc_ref[...]
        ms = jnp.mean(o * o, axis=-1, keepdims=True)
        o_ref[0, pl.ds(r0, Q_TILE), :] = (o * lax.rsqrt(ms + RMS_EPS) * norm_ref[...]).astype(BF16)
        return carry

    lax.fori_loop(0, Q_BLOCK // Q_TILE, tile, 0, unroll=TILE_UNROLL)


def _attn_a(qa, ka, va, sinks, slopes, lane_masks, norm_a):
    bsz, seq_len, _ = qa.shape
    width = Q_TILE + 2 * A_RADIUS
    bias = _band_bias_table(slopes, A_RADIUS, width, 1.0)
    kv_spec = pl.BlockSpec((1, seq_len, A_KV2_WIDTH), lambda b, i: (b, 0, 0))
    return pl.pallas_call(
        functools.partial(_attn_a_kernel, seq_len=seq_len),
        grid=(bsz, seq_len // Q_BLOCK),
        in_specs=[pl.BlockSpec(memory_space=pltpu.SMEM),
                  pl.BlockSpec((2, LANES), lambda b, i: (0, 0)),
                  pl.BlockSpec((1, A_WIDTH), lambda b, i: (0, 0)),
                  pl.BlockSpec(bias.shape, lambda b, i: (0, 0, 0, 0), pipeline_mode=pl.Buffered(1)),
                  pl.BlockSpec((1, Q_BLOCK, A_WIDTH), lambda b, i: (b, i, 0)),
                  kv_spec, kv_spec],
        out_specs=pl.BlockSpec((1, Q_BLOCK, A_WIDTH), lambda b, i: (b, i, 0)),
        out_shape=jax.ShapeDtypeStruct((bsz, seq_len, A_WIDTH), BF16),
        scratch_shapes=[pltpu.VMEM((Q_TILE, A_WIDTH), F32)],
        compiler_params=_params("parallel", "arbitrary"),
        name="attn_window",
    )(sinks, lane_masks, norm_a, bias, qa, ka, va)


def _attn_b_kernel(mask_ref, bias_ref, q_ref, k_ref, v_ref, o_ref, lse_ref, *,
                   seq_len, q_block, res_block, radius):
    width = min(Q_TILE + 2 * radius, seq_len)
    mxu_sums = width >= 2 * LANES
    blk = pl.program_id(2)
    lane = lax.broadcasted_iota(jnp.int32, (Q_TILE, LANES), 1)

    for rr in range(res_block):
        base = rr * B_WIDTH

        def tile(t, carry, base=base, rr=rr):
            r0 = pl.multiple_of(t * Q_TILE, Q_TILE)
            q0 = blk * q_block + r0
            ks = pl.multiple_of(jnp.clip(q0 - radius, 0, seq_len - width), radius)
            case = _tile_case(q0, seq_len)
            max_tile = jnp.zeros((Q_TILE, LANES), F32)
            den_tile = jnp.ones((Q_TILE, LANES), F32)
            for c in range(B_WIDTH // LANES):
                cs = slice(base + c * LANES, base + (c + 1) * LANES)
                qc = q_ref[0, pl.ds(r0, Q_TILE), cs]
                kc = k_ref[0, pl.ds(ks, width), cs]
                vc = v_ref[0, pl.ds(ks, width), cs]
                if mxu_sums:
                    vc = _with_ones(vc)
                q2 = jnp.concatenate([qc * mask_ref[0:1, :], qc * mask_ref[1:2, :]], axis=0)
                s2 = _dot_nt(q2, kc)
                probs, maxes, sums = [], [], []
                for half in range(2):
                    s = s2[half * Q_TILE:(half + 1) * Q_TILE] + bias_ref[case, 2 * c + half]
                    m = jnp.max(s, axis=-1, keepdims=True)
                    p = jnp.exp2(s - m)
                    if not mxu_sums:
                        sums.append(jnp.sum(p, axis=-1, keepdims=True))
                    probs.append(p.astype(BF16))
                    maxes.append(m)
                o2 = jnp.dot(jnp.concatenate(probs, axis=0), vc, preferred_element_type=F32)
                if mxu_sums:
                    sums = [o2[:Q_TILE, LANES:], o2[Q_TILE:, LANES:]]
                low = lane < HEAD_DIM
                num = jnp.where(low, o2[:Q_TILE, :LANES], o2[Q_TILE:, :LANES])
                o_ref[0, pl.ds(r0, Q_TILE), cs] = (num / jnp.where(low, sums[0], sums[1])).astype(BF16)
                for half in range(2):
                    here = lane == 2 * c + half
                    max_tile = jnp.where(here, maxes[half], max_tile)
                    den_tile = jnp.where(here, sums[half], den_tile)
            lse_ref[0, pl.ds(r0, Q_TILE), rr * LANES:(rr + 1) * LANES] = max_tile + jnp.log2(den_tile)
            return carry

        lax.fori_loop(0, q_block // Q_TILE, tile, 0, unroll=min(TILE_UNROLL, q_block // Q_TILE))


def _attn_b_branch(qv, kv, vv, slopes, lane_masks, bsz, window, dilation):
    seq_len = qv.shape[0] // bsz
    radius = (window // 2) // dilation
    q_block = min(Q_BLOCK, seq_len)
    res_block = min(dilation, max(1, TILE_UNROLL * Q_TILE // q_block))
    bias = _band_bias_table(slopes, radius, min(Q_TILE + 2 * radius, seq_len), float(dilation))
    view = lambda t: t.reshape(bsz, seq_len, t.shape[-1])
    kv_spec = pl.BlockSpec((1, seq_len, res_block * B_WIDTH), lambda b, r, i: (b, 0, r))
    o, lse = pl.pallas_call(
        functools.partial(_attn_b_kernel, seq_len=seq_len, q_block=q_block, res_block=res_block, radius=radius),
        grid=(bsz, dilation // res_block, seq_len // q_block),
        in_specs=[pl.BlockSpec((2, LANES), lambda b, r, i: (0, 0)),
                  pl.BlockSpec(bias.shape, lambda b, r, i: (0, 0, 0, 0), pipeline_mode=pl.Buffered(1)),
                  pl.BlockSpec((1, q_block, res_block * B_WIDTH), lambda b, r, i: (b, i, r)),
                  kv_spec, kv_spec],
        out_specs=[pl.BlockSpec((1, q_block, res_block * B_WIDTH), lambda b, r, i: (b, i, r)),
                   pl.BlockSpec((1, q_block, res_block * LANES), lambda b, r, i: (b, i, r))],
        out_shape=[jax.ShapeDtypeStruct((bsz, seq_len, dilation * B_WIDTH), BF16),
                   jax.ShapeDtypeStruct((bsz, seq_len, dilation * LANES), F32)],
        compiler_params=_params("parallel", "parallel", "arbitrary"),
        name=f"attn_dilated_{dilation}",
    )(lane_masks, bias, view(qv), view(kv), view(vv))
    return o.reshape(bsz * seq_len, dilation * B_WIDTH), lse.reshape(bsz * seq_len, dilation * LANES)


def _layer_norm(v, g, b):
    mu = jnp.mean(v, axis=-1, keepdims=True)
    c = v - mu
    var = jnp.mean(c * c, axis=-1, keepdims=True)
    return c * lax.rsqrt(var + LN_EPS) * g + b


def _mix_out_kernel(x_ref, oa_ref, *rest):
    n_br = len(B_DILATIONS)
    ob_refs, lse_refs = rest[:n_br], rest[n_br:2 * n_br]
    expand_ref, norm_b_ref, w_ref, g_ref, b_ref, y_ref, ob_stage, lse_stage = rest[2 * n_br:]
    obs, lses = [], []
    for idx, ((_, d), o_ref, l_ref) in enumerate(zip(B_DILATIONS, ob_refs, lse_refs)):
        if d == 1:
            obs.append(o_ref[...].astype(F32))
            lses.append(l_ref[...])
            continue
        rows = OUT_TILE // d
        n_cols = B_WIDTH // LANES
        for res in range(d):
            for c in range(n_cols):
                lo_c = res * B_WIDTH + c * LANES
                ob_stage[idx * n_cols + c, pl.ds(res, rows, stride=d), :] = o_ref[:, lo_c:lo_c + LANES].astype(F32)
            lse_stage[idx, pl.ds(res, rows, stride=d), :] = l_ref[:, res * LANES:(res + 1) * LANES]
        obs.append(jnp.concatenate([ob_stage[idx * n_cols + c] for c in range(n_cols)], axis=-1))
        lses.append(lse_stage[idx])
    top = functools.reduce(jnp.maximum, lses)
    es = [jnp.exp2(l - top) for l in lses]
    inv = 1.0 / functools.reduce(lambda a, b: a + b, es)
    ob = None
    for e, o in zip(es, obs):
        wgt = e * inv
        wide = jnp.dot(jnp.concatenate(_split_bf16(wgt), axis=-1), expand_ref[...], preferred_element_type=F32)
        ob = wide * o if ob is None else ob + wide * o
    ms = jnp.mean(ob * ob, axis=-1, keepdims=True)
    ob = (ob * lax.rsqrt(ms + RMS_EPS) * norm_b_ref[...]).astype(BF16)
    mix = (jnp.dot(oa_ref[...], w_ref[:A_WIDTH, :], preferred_element_type=F32)
           + jnp.dot(ob, w_ref[A_WIDTH:, :], preferred_element_type=F32))
    y_ref[...] = _layer_norm(DEEPNORM_ALPHA * x_ref[...] + mix, g_ref[...], b_ref[...])


def _mix_out(x, oa, obs, lses, expand, norm_b, w_out, ln_g, ln_b):
    n = x.shape[0]
    row = lambda w: pl.BlockSpec((OUT_TILE, w), lambda i: (i, 0))
    full = lambda r, c: pl.BlockSpec((r, c), lambda i: (0, 0), pipeline_mode=pl.Buffered(1))
    strided = lambda w: [pl.BlockSpec((OUT_TILE // d, d * w), lambda i: (i, 0)) for _, d in B_DILATIONS]
    n_br = len(B_DILATIONS)
    return pl.pallas_call(
        _mix_out_kernel,
        grid=(n // OUT_TILE,),
        in_specs=[row(D_MODEL), row(A_WIDTH), *strided(B_WIDTH), *strided(LANES),
                  full(2 * LANES, B_WIDTH), full(1, B_WIDTH), full(D_MODEL, D_MODEL),
                  full(1, D_MODEL), full(1, D_MODEL)],
        out_specs=row(D_MODEL),
        out_shape=jax.ShapeDtypeStruct((n, D_MODEL), F32),
        scratch_shapes=[pltpu.VMEM((n_br * B_WIDTH // LANES, OUT_TILE, LANES), F32),
                        pltpu.VMEM((n_br, OUT_TILE, LANES), F32)],
        compiler_params=_params("parallel"),
        name="mix_out",
    )(x, oa, *obs, *lses, expand, norm_b, w_out, ln_g, ln_b)


def _silu(v):
    return v / (1.0 + jnp.exp(-v))


def _route(scores, biased):
    def top2_sum(c):
        hi01, lo01 = jnp.maximum(c[0], c[1]), jnp.minimum(c[0], c[1])
        hi23, lo23 = jnp.maximum(c[2], c[3]), jnp.minimum(c[2], c[3])
        first = jnp.maximum(hi01, hi23)
        second = jnp.maximum(jnp.minimum(hi01, hi23), jnp.where(hi01 >= hi23, lo01, lo23))
        return first + second

    groups = [biased[g * EXPERTS_PER_GROUP:(g + 1) * EXPERTS_PER_GROUP] for g in range(N_GROUPS)]
    gscore = [top2_sum(c) for c in groups]
    sel = jnp.zeros_like(gscore[0], dtype=jnp.int32)
    best = gscore[0]
    for g in range(1, N_GROUPS):
        better = gscore[g] > best
        sel = jnp.where(better, g, sel)
        best = jnp.where(better, gscore[g], best)

    def pick(cols):
        out = []
        for j in range(EXPERTS_PER_GROUP):
            v = cols[j]
            for g in range(1, N_GROUPS):
                v = jnp.where(sel == g, cols[g * EXPERTS_PER_GROUP + j], v)
            out.append(v)
        return out

    cb, cs = pick(biased), pick(scores)
    i1 = jnp.zeros_like(sel)
    v1 = cb[0]
    for j in range(1, EXPERTS_PER_GROUP):
        better = cb[j] > v1
        i1 = jnp.where(better, j, i1)
        v1 = jnp.where(better, cb[j], v1)
    i2 = jnp.where(i1 == 0, 1, 0)
    v2 = jnp.where(i1 == 0, cb[1], cb[0])
    for j in range(1, EXPERTS_PER_GROUP):
        better = (cb[j] > v2) & (i1 != j) & (i2 != j)
        i2 = jnp.where(better, j, i2)
        v2 = jnp.where(better, cb[j], v2)
    s1 = sum(jnp.where(i1 == j, cs[j], 0.0) for j in range(EXPERTS_PER_GROUP))
    s2 = sum(jnp.where(i2 == j, cs[j], 0.0) for j in range(EXPERTS_PER_GROUP))
    total = s1 + s2
    gates = []
    for e in range(N_EXPERTS):
        g, j = divmod(e, EXPERTS_PER_GROUP)
        in_group = sel == g
        gate = jnp.where(in_group & (i1 == j), s1 / total, 0.0) + jnp.where(in_group & (i2 == j), s2 / total, 0.0)
        gates.append(gate)
    return gates, sel


def _split_bf16(v):
    hi = v.astype(BF16)
    return hi, (v - hi.astype(F32)).astype(BF16)


def _moe_kernel(x_ref, tri_ref, wr_ref, rb_ref, wg_ref, wu_ref, wd_ref, sg_ref, su_ref, sd_ref,
                g_ref, b_ref, y_ref, yb_ref, xb_ref, xs_ref, gs_ref, ys_ref, pos_ref, off_ref):
    grp = pl.program_id(1)
    tile = MOE_SORT
    n_chunks = tile // SORT_CHUNK
    sections = range(MOE_TILE // MOE_SORT)

    def perm_rows(s, c):
        slot = (lax.broadcasted_iota(jnp.int32, (SORT_CHUNK, tile), 0) + c * SORT_CHUNK).astype(F32)
        return jnp.where(pos_ref[s] == slot, 1.0, 0.0).astype(BF16)

    def route_and_sort(s):
        x = x_ref[s * tile:(s + 1) * tile, :]
        xb = x.astype(BF16)
        xb_ref[s * tile:(s + 1) * tile, :] = xb
        x_lo = (x - xb.astype(F32)).astype(BF16)
        w_hi, w_lo = _split_bf16(wr_ref[...])
        logits = (_dot_nt(jnp.concatenate([w_hi, w_lo], axis=0), xb)
                  + jnp.concatenate([_dot_nt(w_hi, x_lo), jnp.zeros((N_EXPERTS, tile), F32)], axis=0))
        logits = logits[:N_EXPERTS] + logits[N_EXPERTS:]
        scores = 1.0 / (1.0 + jnp.exp(-logits))
        biased = scores + rb_ref[...]
        rows = lambda t: [t[j:j + 1, :] for j in range(N_EXPERTS)]
        gates, sel = _route(rows(scores), rows(biased))
        sub = lax.broadcasted_iota(jnp.int32, (8, tile), 0)
        onehot = jnp.where(sub == sel, 1.0, 0.0).astype(BF16)
        rank = jnp.dot(onehot, tri_ref[...], preferred_element_type=F32)
        last_lane = lax.broadcasted_iota(jnp.int32, (1, tile), 1) == tile - 1
        pos = jnp.zeros((1, tile), F32)
        start = jnp.float32(0.0)
        off_ref[s, 0] = 0
        for g in range(N_GROUPS):
            pos = jnp.where(sel == g, start + rank[g:g + 1, :] - 1.0, pos)
            start = start + jnp.sum(jnp.where(last_lane, rank[g:g + 1, :], 0.0))
            off_ref[s, g + 1] = start.astype(jnp.int32)
        pos_ref[s] = pos
        gate_rows = jnp.concatenate(gates + [jnp.zeros((LANES - N_EXPERTS, tile), F32)], axis=0)
        g_split = jnp.concatenate(_split_bf16(gate_rows), axis=0)
        for c in range(n_chunks):
            perm = perm_rows(s, c)
            rows_c = slice(c * SORT_CHUNK, (c + 1) * SORT_CHUNK)
            xs_ref[s, rows_c, :] = jnp.dot(perm, xb, preferred_element_type=F32).astype(BF16)
            g2 = _dot_nt(perm, g_split)
            gs_ref[s, rows_c, :] = g2[:, :LANES] + g2[:, LANES:]
        xs_ref[s, tile:, :] = jnp.zeros((MOE_CHUNK, D_MODEL), BF16)
        gs_ref[s, tile:, :] = jnp.zeros((MOE_CHUNK, LANES), F32)
        ys_ref[s] = jnp.zeros(ys_ref.shape[1:], F32)

    def run_group():
        begins, trips = [], []
        for s in sections:
            first, end = off_ref[s, grp], off_ref[s, grp + 1]
            begin = lax.shift_left(lax.shift_right_logical(first, ROW_ALIGN_LOG2), ROW_ALIGN_LOG2)
            begins.append(begin)
            trips.append(jnp.where(end > first, lax.div(end - begin + (MOE_CHUNK - 1), MOE_CHUNK), 0))
        chunk_lane = lax.broadcasted_iota(jnp.int32, (len(sections) * MOE_CHUNK, LANES), 1)

        def chunk(c, carry):
            starts = [pl.multiple_of(jnp.minimum(b + c * MOE_CHUNK, tile), 1 << ROW_ALIGN_LOG2) for b in begins]
            xc = jnp.concatenate([xs_ref[s, pl.ds(r0, MOE_CHUNK), :] for s, r0 in zip(sections, starts)], axis=0)
            gc = jnp.concatenate([gs_ref[s, pl.ds(r0, MOE_CHUNK), :] for s, r0 in zip(sections, starts)], axis=0)
            yc = None
            for j in range(EXPERTS_PER_GROUP):
                gate = jnp.sum(jnp.where(chunk_lane == grp * EXPERTS_PER_GROUP + j, gc, 0.0), axis=-1, keepdims=True)
                h = _silu(jnp.dot(xc, wg_ref[j], preferred_element_type=F32)) * jnp.dot(
                    xc, wu_ref[j], preferred_element_type=F32)
                part = jnp.dot((h * gate).astype(BF16), wd_ref[j], preferred_element_type=F32)
                yc = part if yc is None else yc + part
            for s, r0 in zip(sections, starts):
                ys_ref[s, pl.ds(r0, MOE_CHUNK), :] += yc[s * MOE_CHUNK:(s + 1) * MOE_CHUNK]
            return carry

        lax.fori_loop(0, functools.reduce(jnp.maximum, trips), chunk, 0)

    def unsort_and_norm(s):
        rows_s = slice(s * tile, (s + 1) * tile)
        xb = xb_ref[rows_s, :]
        hs = _silu(jnp.dot(xb, sg_ref[...], preferred_element_type=F32)) * jnp.dot(
            xb, su_ref[...], preferred_element_type=F32)
        acc = jnp.dot(hs.astype(BF16), sd_ref[...], preferred_element_type=F32)
        for c in range(n_chunks):
            acc = acc + lax.dot_general(perm_rows(s, c),
                                        ys_ref[s, c * SORT_CHUNK:(c + 1) * SORT_CHUNK, :].astype(BF16),
                                        (((0,), (0,)), ((), ())), preferred_element_type=F32)
        y = _layer_norm(DEEPNORM_ALPHA * x_ref[rows_s, :] + acc, g_ref[...], b_ref[...])
        y_ref[rows_s, :] = y
        yb_ref[rows_s, :] = y.astype(BF16)

    @pl.when(grp == 0)
    def _():
        for s in sections:
            route_and_sort(s)

    run_group()

    @pl.when(grp == N_GROUPS - 1)
    def _():
        for s in sections:
            unsort_and_norm(s)


def _moe(x, tri, w_router, router_bias, w_gate, w_up, w_down, ws_gate, ws_up, ws_down, ln_g, ln_b):
    n = x.shape[0]
    row = pl.BlockSpec((MOE_TILE, D_MODEL), lambda i, g: (i, 0))
    full = lambda r, c: pl.BlockSpec((r, c), lambda i, g: (0, 0), pipeline_mode=pl.Buffered(1))
    group_w = lambda r, c: pl.BlockSpec((EXPERTS_PER_GROUP, r, c), lambda i, g: (g, 0, 0))
    n_sec, sec_rows = MOE_TILE // MOE_SORT, MOE_SORT + MOE_CHUNK
    return pl.pallas_call(
        _moe_kernel,
        grid=(n // MOE_TILE, N_GROUPS),
        in_specs=[row, full(MOE_SORT, MOE_SORT), full(N_EXPERTS, D_MODEL), full(N_EXPERTS, 1),
                  group_w(D_MODEL, D_EXPERT), group_w(D_MODEL, D_EXPERT), group_w(D_EXPERT, D_MODEL),
                  full(D_MODEL, D_EXPERT), full(D_MODEL, D_EXPERT), full(D_EXPERT, D_MODEL),
                  full(1, D_MODEL), full(1, D_MODEL)],
        out_specs=[row, row],
        out_shape=[jax.ShapeDtypeStruct((n, D_MODEL), F32), jax.ShapeDtypeStruct((n, D_MODEL), BF16)],
        scratch_shapes=[pltpu.VMEM((MOE_TILE, D_MODEL), BF16),
                        pltpu.VMEM((n_sec, sec_rows, D_MODEL), BF16),
                        pltpu.VMEM((n_sec, sec_rows, LANES), F32),
                        pltpu.VMEM((n_sec, sec_rows, D_MODEL), F32),
                        pltpu.VMEM((n_sec, 1, MOE_SORT), F32),
                        pltpu.SMEM((n_sec, N_GROUPS + 1), jnp.int32)],
        compiler_params=pltpu.CompilerParams(dimension_semantics=("parallel", "arbitrary"),
                                             vmem_limit_bytes=MOE_VMEM_LIMIT),
        name="moe",
    )(x, tri, w_router, router_bias, w_gate, w_up, w_down, ws_gate, ws_up, ws_down, ln_g, ln_b)


def _alibi_slopes():
    n_heads = A_HEADS + B_HEADS
    h = jnp.arange(1, n_heads + 1, dtype=F32)
    return jnp.exp2(-8.0 * h / n_heads)


def _layer(x, xb, p, l, shape):
    bsz, seq_len = shape
    tokens = bsz * seq_len
    qa, ka, va, qb, kb, vb = _inproj(xb, p["w_in"][l])
    seq = lambda t: t.reshape(bsz, seq_len, t.shape[-1])
    oa = _attn_a(seq(qa), seq(ka), seq(va), p["sink2"][l], p["slopes_a"], p["lane_masks"], p["norm_a"][l])
    obs, lses = [], []
    for idx, (window, dilation) in enumerate(B_DILATIONS):
        o, lse = _attn_b_branch(qb[idx], kb[idx], vb[idx], p["slopes_b"], p["lane_masks"], bsz, window, dilation)
        obs.append(o)
        lses.append(lse)
    x = _mix_out(x, oa.reshape(tokens, A_WIDTH), obs, lses, p["expand"], p["norm_b"][l],
                 p["w_out"][l], p["ln1_g"][l], p["ln1_b"][l])
    return _moe(x, p["tri"], p["w_router"], p["router_bias"], p["w_gate"][l], p["w_up"][l], p["w_down"][l],
                p["ws_gate"][l], p["ws_up"][l], p["ws_down"][l], p["ln2_g"][l], p["ln2_b"][l])


def _trunk(x, p):
    bsz, seq_len, _ = x.shape
    h = x.reshape(bsz * seq_len, D_MODEL)
    hb = h
    for l in range(DEPTH):
        h, hb = _layer(h, hb, p, l, (bsz, seq_len))
    return h.reshape(bsz, seq_len, D_MODEL)


def _prepare(w_in, sink_a, norm_a, norm_b, w_out, ln1_g, ln1_b, w_router, router_bias,
             w_gate, w_up, w_down, ws_gate, ws_up, ws_down, ln2_g, ln2_b):
    slopes = _alibi_slopes()
    head_lane = jnp.arange(B_WIDTH)[None, :] // HEAD_DIM
    lane_half = jnp.arange(LANES)[None, :] // HEAD_DIM

    def dup_kv(w):
        h0, h1 = w[..., :HEAD_DIM], w[..., HEAD_DIM:]
        return jnp.concatenate([h0, h0, h1, h1], axis=-1)

    lo = A_WIDTH
    w_proj = jnp.concatenate([w_in[..., :lo], dup_kv(w_in[..., lo:lo + A_KV_WIDTH]),
                              dup_kv(w_in[..., lo + A_KV_WIDTH:lo + 2 * A_KV_WIDTH]),
                              w_in[..., lo + 2 * A_KV_WIDTH:]], axis=-1)
    return {
        "w_in": w_proj.astype(BF16),
        "sink2": sink_a.astype(F32) * LOG2E,
        "slopes_a": slopes[0::2], "slopes_b": slopes[1::2],
        "lane_masks": (lane_half == jnp.arange(2)[:, None]).astype(BF16),
        "expand": (jnp.arange(2 * LANES)[:, None] % LANES == head_lane).astype(BF16),
        "tri": jnp.tri(MOE_SORT, dtype=BF16).T,
        "norm_a": norm_a.reshape(DEPTH, 1, A_WIDTH),
        "norm_b": norm_b.reshape(DEPTH, 1, B_WIDTH),
        "w_out": w_out.astype(BF16),
        "ln1_g": ln1_g.reshape(DEPTH, 1, D_MODEL), "ln1_b": ln1_b.reshape(DEPTH, 1, D_MODEL),
        "w_router": w_router.T, "router_bias": router_bias.reshape(N_EXPERTS, 1),
        "w_gate": w_gate.astype(BF16), "w_up": w_up.astype(BF16), "w_down": w_down.astype(BF16),
        "ws_gate": ws_gate.astype(BF16), "ws_up": ws_up.astype(BF16), "ws_down": ws_down.astype(BF16),
        "ln2_g": ln2_g.reshape(DEPTH, 1, D_MODEL), "ln2_b": ln2_b.reshape(DEPTH, 1, D_MODEL),
    }


def kernel(x_prompt, x_sample, w_in, sink_a, norm_a, norm_b, w_out, ln1_g, ln1_b, w_router, router_bias,
           w_gate, w_up, w_down, ws_gate, ws_up, ws_down, ln2_g, ln2_b):
    p = _prepare(w_in, sink_a, norm_a, norm_b, w_out, ln1_g, ln1_b, w_router, router_bias,
                 w_gate, w_up, w_down, ws_gate, ws_up, ws_down, ln2_g, ln2_b)
    return _trunk(x_prompt, p), _trunk(x_sample, p)
```

```python
import functools

import jax
import jax.numpy as jnp
from jax import lax
from jax.experimental import pallas as pl
from jax.experimental.pallas import tpu as pltpu

F32 = jnp.float32
BF16 = jnp.bfloat16

D_MODEL = 1024
DEPTH = 4
HEAD_DIM = 64
A_HEADS = 8
A_KV_HEADS = 2
A_RADIUS = 128
B_HEADS = 8
B_DILATIONS = ((128, 1), (512, 4), (2048, 16))
A_WIDTH = A_HEADS * HEAD_DIM
A_KV_WIDTH = A_KV_HEADS * HEAD_DIM
B_WIDTH = B_HEADS * HEAD_DIM
N_EXPERTS = 16
N_GROUPS = 4
EXPERTS_PER_GROUP = N_EXPERTS // N_GROUPS
D_EXPERT = 256
DEEPNORM_ALPHA = (2 * DEPTH) ** 0.25
LN_EPS = 1e-5
RMS_EPS = 1e-6
LOG2E = 1.4426950408889634

LANES = 128
MASK_PENALTY = 1e30
VMEM_LIMIT = 48 * 1024 * 1024

Q_TILE = 128
Q_BLOCK = 1024
TILE_UNROLL = 4
WIDE_TILE_UNROLL = 8
PROJ_TILE = 1024
OUT_TILE = 1024
MOE_TILE = 1024
MOE_SORT = 512
MOE_CHUNK = 160
SORT_CHUNK = 256
ROW_ALIGN_LOG2 = 4
MOE_VMEM_LIMIT = 56 * 1024 * 1024
STRIDED_DILATIONS = tuple(d for _, d in B_DILATIONS if d > 1)
A_KV2_WIDTH = 2 * A_KV_WIDTH
PROJ_WIDTH = A_WIDTH + 2 * A_KV2_WIDTH + 3 * B_WIDTH


def _params(*sem):
    return pltpu.CompilerParams(dimension_semantics=sem, vmem_limit_bytes=VMEM_LIMIT)


def _dot_nt(a, b):
    return lax.dot_general(a, b, (((1,), (1,)), ((), ())), preferred_element_type=F32)


def _inproj_kernel(x_ref, w_ref, qa_ref, ka_ref, va_ref, *b_refs):
    q_scale = HEAD_DIM ** -0.5 * LOG2E

    def proj(lo, width):
        return jnp.dot(x_ref[...].astype(BF16), w_ref[:, lo:lo + width], preferred_element_type=F32)

    qa_ref[...] = (proj(0, A_WIDTH) * q_scale).astype(BF16)
    ka_ref[...] = proj(A_WIDTH, A_KV2_WIDTH).astype(BF16)
    va_ref[...] = proj(A_WIDTH + A_KV2_WIDTH, A_KV2_WIDTH).astype(BF16)
    lo = A_WIDTH + 2 * A_KV2_WIDTH
    n_layouts = 1 + len(STRIDED_DILATIONS)
    for j in range(3):
        r = proj(lo + j * B_WIDTH, B_WIDTH)
        if j == 0:
            r = r * q_scale
        outs = b_refs[j * n_layouts:(j + 1) * n_layouts]
        outs[0][...] = r.astype(BF16)
        for d, out in zip(STRIDED_DILATIONS, outs[1:]):
            rows = PROJ_TILE // d
            for c in range(B_WIDTH // LANES):
                blk = r[:, c * LANES:(c + 1) * LANES].reshape(rows, d, LANES)
                blk = pltpu.einshape("mrc->rmc", blk)
                for res in range(d):
                    lo_c = res * B_WIDTH + c * LANES
                    out[:, lo_c:lo_c + LANES] = blk[res].astype(BF16)


def _inproj(x, w_in):
    n = x.shape[0]
    shapes = [(n, A_WIDTH), (n, A_KV2_WIDTH), (n, A_KV2_WIDTH)]
    blocks = [(PROJ_TILE, A_WIDTH), (PROJ_TILE, A_KV2_WIDTH), (PROJ_TILE, A_KV2_WIDTH)]
    for _ in range(3):
        for d in (1,) + STRIDED_DILATIONS:
            shapes.append((n // d, d * B_WIDTH))
            blocks.append((PROJ_TILE // d, d * B_WIDTH))
    outs = pl.pallas_call(
        _inproj_kernel,
        grid=(n // PROJ_TILE,),
        in_specs=[pl.BlockSpec((PROJ_TILE, D_MODEL), lambda i: (i, 0)),
                  pl.BlockSpec((D_MODEL, PROJ_WIDTH), lambda i: (0, 0), pipeline_mode=pl.Buffered(1))],
        out_specs=[pl.BlockSpec(b, lambda i: (i, 0)) for b in blocks],
        out_shape=[jax.ShapeDtypeStruct(s, BF16) for s in shapes],
        compiler_params=_params("parallel"),
        name="inproj",
    )(x, w_in)
    n_layouts = 1 + len(STRIDED_DILATIONS)
    qa, ka, va = outs[:3]
    qb, kb, vb = (outs[3 + j * n_layouts:3 + (j + 1) * n_layouts] for j in range(3))
    return qa, ka, va, qb, kb, vb


def _band_bias_table(slopes, radius, width, dist_scale):
    qi = jnp.arange(Q_TILE)[:, None]
    kj = jnp.arange(width)[None, :]
    cases = []
    for offset in (0, -radius, Q_TILE - width):
        dist = jnp.abs(kj - qi + offset)
        pen = jnp.where(dist <= radius, dist.astype(F32) * (dist_scale * LOG2E), MASK_PENALTY)
        cases.append(-slopes[:, None, None] * pen[None])
    return jnp.stack(cases, axis=0)


def _tile_case(q0, seq_len):
    return jnp.where(q0 == 0, 0, jnp.where(q0 == seq_len - Q_TILE, 2, 1))


def _with_ones(v):
    return jnp.concatenate([v, jnp.ones(v.shape, v.dtype)], axis=-1)


def _attn_a_kernel(sink_ref, mask_ref, norm_ref, bias_ref, q_ref, k_ref, v_ref, o_ref, acc_ref, *, seq_len):
    width = Q_TILE + 2 * A_RADIUS
    blk = pl.program_id(1)
    lane = lax.broadcasted_iota(jnp.int32, (Q_TILE, LANES), 1)

    def tile(t, carry):
        r0 = pl.multiple_of(t * Q_TILE, Q_TILE)
        q0 = blk * Q_BLOCK + r0
        ks = pl.multiple_of(jnp.clip(q0 - A_RADIUS, 0, seq_len - width), Q_TILE)
        case = _tile_case(q0, seq_len)
        for g in range(A_KV_HEADS):
            heads = [(c, half) for c in (2 * g, 2 * g + 1) for half in range(2)]
            q4 = jnp.concatenate([q_ref[0, pl.ds(r0, Q_TILE), c * LANES:(c + 1) * LANES] * mask_ref[half:half + 1, :]
                                  for c, half in heads], axis=0)
            kg = k_ref[0, pl.ds(ks, width), g * LANES:(g + 1) * LANES]
            vg = v_ref[0, pl.ds(ks, width), g * LANES:(g + 1) * LANES]
            s4 = _dot_nt(q4, kg)
            probs, inv = [], []
            for j, (c, half) in enumerate(heads):
                h = 2 * c + half
                sb = s4[j * Q_TILE:(j + 1) * Q_TILE] + bias_ref[case, h]
                sink = sink_ref[h]
                m = jnp.maximum(jnp.max(sb, axis=-1, keepdims=True), sink)
                p = jnp.exp2(sb - m)
                inv.append(1.0 / (jnp.sum(p, axis=-1, keepdims=True) + jnp.exp2(sink - m)))
                probs.append(p.astype(BF16))
            o4 = jnp.dot(jnp.concatenate(probs, axis=0), vg, preferred_element_type=F32)
            for jc, c in enumerate((2 * g, 2 * g + 1)):
                lo = o4[(2 * jc) * Q_TILE:(2 * jc + 1) * Q_TILE] * inv[2 * jc]
                hi = o4[(2 * jc + 1) * Q_TILE:(2 * jc + 2) * Q_TILE] * inv[2 * jc + 1]
                acc_ref[:, c * LANES:(c + 1) * LANES] = jnp.where(lane < HEAD_DIM, lo, hi)
        o = acc_ref[...]
        ms = jnp.mean(o * o, axis=-1, keepdims=True)
        o_ref[0, pl.ds(r0, Q_TILE), :] = (o * lax.rsqrt(ms + RMS_EPS) * norm_ref[...]).astype(BF16)
        return carry

    lax.fori_loop(0, Q_BLOCK // Q_TILE, tile, 0, unroll=WIDE_TILE_UNROLL)


def _attn_a(qa, ka, va, sinks, slopes, lane_masks, norm_a):
    bsz, seq_len, _ = qa.shape
    width = Q_TILE + 2 * A_RADIUS
    bias = _band_bias_table(slopes, A_RADIUS, width, 1.0)
    kv_spec = pl.BlockSpec((1, seq_len, A_KV2_WIDTH), lambda b, i: (b, 0, 0))
    return pl.pallas_call(
        functools.partial(_attn_a_kernel, seq_len=seq_len),
        grid=(bsz, seq_len // Q_BLOCK),
        in_specs=[pl.BlockSpec(memory_space=pltpu.SMEM),
                  pl.BlockSpec((2, LANES), lambda b, i: (0, 0)),
                  pl.BlockSpec((1, A_WIDTH), lambda b, i: (0, 0)),
                  pl.BlockSpec(bias.shape, lambda b, i: (0, 0, 0, 0), pipeline_mode=pl.Buffered(1)),
                  pl.BlockSpec((1, Q_BLOCK, A_WIDTH), lambda b, i: (b, i, 0)),
                  kv_spec, kv_spec],
        out_specs=pl.BlockSpec((1, Q_BLOCK, A_WIDTH), lambda b, i: (b, i, 0)),
        out_shape=jax.ShapeDtypeStruct((bsz, seq_len, A_WIDTH), BF16),
        scratch_shapes=[pltpu.VMEM((Q_TILE, A_WIDTH), F32)],
        compiler_params=_params("parallel", "arbitrary"),
        name="attn_window",
    )(sinks, lane_masks, norm_a, bias, qa, ka, va)


def _attn_b_kernel(mask_ref, bias_ref, q_ref, k_ref, v_ref, o_ref, lse_ref, *,
                   seq_len, q_block, res_block, radius):
    width = min(Q_TILE + 2 * radius, seq_len)
    mxu_sums = width >= 2 * LANES
    blk = pl.program_id(2)
    lane = lax.broadcasted_iota(jnp.int32, (Q_TILE, LANES), 1)

    for rr in range(res_block):
        base = rr * B_WIDTH

        def tile(t, carry, base=base, rr=rr):
            r0 = pl.multiple_of(t * Q_TILE, Q_TILE)
            q0 = blk * q_block + r0
            ks = pl.multiple_of(jnp.clip(q0 - radius, 0, seq_len - width), radius)
            case = _tile_case(q0, seq_len)
            max_tile = jnp.zeros((Q_TILE, LANES), F32)
            den_tile = jnp.ones((Q_TILE, LANES), F32)
            for c in range(B_WIDTH // LANES):
                cs = slice(base + c * LANES, base + (c + 1) * LANES)
                qc = q_ref[0, pl.ds(r0, Q_TILE), cs]
                kc = k_ref[0, pl.ds(ks, width), cs]
                vc = v_ref[0, pl.ds(ks, width), cs]
                if mxu_sums:
                    vc = _with_ones(vc)
                q2 = jnp.concatenate([qc * mask_ref[0:1, :], qc * mask_ref[1:2, :]], axis=0)
                s2 = _dot_nt(q2, kc)
                probs, maxes, sums = [], [], []
                for half in range(2):
                    s = s2[half * Q_TILE:(half + 1) * Q_TILE] + bias_ref[case, 2 * c + half]
                    m = jnp.max(s, axis=-1, keepdims=True)
                    p = jnp.exp2(s - m)
                    if not mxu_sums:
                        sums.append(jnp.sum(p, axis=-1, keepdims=True))
                    probs.append(p.astype(BF16))
                    maxes.append(m)
                o2 = jnp.dot(jnp.concatenate(probs, axis=0), vc, preferred_element_type=F32)
                if mxu_sums:
                    sums = [o2[:Q_TILE, LANES:], o2[Q_TILE:, LANES:]]
                low = lane < HEAD_DIM
                num = jnp.where(low, o2[:Q_TILE, :LANES], o2[Q_TILE:, :LANES])
                o_ref[0, pl.ds(r0, Q_TILE), cs] = (num / jnp.where(low, sums[0], sums[1])).astype(BF16)
                for half in range(2):
                    here = lane == 2 * c + half
                    max_tile = jnp.where(here, maxes[half], max_tile)
                    den_tile = jnp.where(here, sums[half], den_tile)
            lse_ref[0, pl.ds(r0, Q_TILE), rr * LANES:(rr + 1) * LANES] = max_tile + jnp.log2(den_tile)
            return carry

        lax.fori_loop(0, q_block // Q_TILE, tile, 0, unroll=min(WIDE_TILE_UNROLL, q_block // Q_TILE))


def _attn_b_branch(qv, kv, vv, slopes, lane_masks, bsz, window, dilation):
    seq_len = qv.shape[0] // bsz
    radius = (window // 2) // dilation
    q_block = min(Q_BLOCK, seq_len)
    width = min(Q_TILE + 2 * radius, seq_len)
    tiles = WIDE_TILE_UNROLL if width >= 2 * LANES else TILE_UNROLL
    res_block = min(dilation, max(1, tiles * Q_TILE // q_block))
    bias = _band_bias_table(slopes, radius, width, float(dilation))
    view = lambda t: t.reshape(bsz, seq_len, t.shape[-1])
    kv_spec = pl.BlockSpec((1, seq_len, res_block * B_WIDTH), lambda b, r, i: (b, 0, r))
    o, lse = pl.pallas_call(
        functools.partial(_attn_b_kernel, seq_len=seq_len, q_block=q_block, res_block=res_block, radius=radius),
        grid=(bsz, dilation // res_block, seq_len // q_block),
        in_specs=[pl.BlockSpec((2, LANES), lambda b, r, i: (0, 0)),
                  pl.BlockSpec(bias.shape, lambda b, r, i: (0, 0, 0, 0), pipeline_mode=pl.Buffered(1)),
                  pl.BlockSpec((1, q_block, res_block * B_WIDTH), lambda b, r, i: (b, i, r)),
                  kv_spec, kv_spec],
        out_specs=[pl.BlockSpec((1, q_block, res_block * B_WIDTH), lambda b, r, i: (b, i, r)),
                   pl.BlockSpec((1, q_block, res_block * LANES), lambda b, r, i: (b, i, r))],
        out_shape=[jax.ShapeDtypeStruct((bsz, seq_len, dilation * B_WIDTH), BF16),
                   jax.ShapeDtypeStruct((bsz, seq_len, dilation * LANES), F32)],
        compiler_params=_params("parallel", "parallel", "arbitrary"),
        name=f"attn_dilated_{dilation}",
    )(lane_masks, bias, view(qv), view(kv), view(vv))
    return o.reshape(bsz * seq_len, dilation * B_WIDTH), lse.reshape(bsz * seq_len, dilation * LANES)


def _layer_norm(v, g, b):
    mu = jnp.mean(v, axis=-1, keepdims=True)
    c = v - mu
    var = jnp.mean(c * c, axis=-1, keepdims=True)
    return c * lax.rsqrt(var + LN_EPS) * g + b


def _mix_out_kernel(x_ref, oa_ref, *rest):
    n_br = len(B_DILATIONS)
    ob_refs, lse_refs = rest[:n_br], rest[n_br:2 * n_br]
    expand_ref, norm_b_ref, w_ref, g_ref, b_ref, y_ref, ob_stage, lse_stage = rest[2 * n_br:]
    obs, lses = [], []
    for idx, ((_, d), o_ref, l_ref) in enumerate(zip(B_DILATIONS, ob_refs, lse_refs)):
        if d == 1:
            obs.append(o_ref[...].astype(F32))
            lses.append(l_ref[...])
            continue
        rows = OUT_TILE // d
        n_cols = B_WIDTH // LANES
        for res in range(d):
            for c in range(n_cols):
                lo_c = res * B_WIDTH + c * LANES
                ob_stage[idx * n_cols + c, pl.ds(res, rows, stride=d), :] = o_ref[:, lo_c:lo_c + LANES].astype(F32)
            lse_stage[idx, pl.ds(res, rows, stride=d), :] = l_ref[:, res * LANES:(res + 1) * LANES]
        obs.append(jnp.concatenate([ob_stage[idx * n_cols + c] for c in range(n_cols)], axis=-1))
        lses.append(lse_stage[idx])
    top = functools.reduce(jnp.maximum, lses)
    es = [jnp.exp2(l - top) for l in lses]
    inv = 1.0 / functools.reduce(lambda a, b: a + b, es)
    ob = None
    for e, o in zip(es, obs):
        wgt = e * inv
        wide = jnp.dot(jnp.concatenate(_split_bf16(wgt), axis=-1), expand_ref[...], preferred_element_type=F32)
        ob = wide * o if ob is None else ob + wide * o
    ms = jnp.mean(ob * ob, axis=-1, keepdims=True)
    ob = (ob * lax.rsqrt(ms + RMS_EPS) * norm_b_ref[...]).astype(BF16)
    mix = (jnp.dot(oa_ref[...], w_ref[:A_WIDTH, :], preferred_element_type=F32)
           + jnp.dot(ob, w_ref[A_WIDTH:, :], preferred_element_type=F32))
    y_ref[...] = _layer_norm(DEEPNORM_ALPHA * x_ref[...] + mix, g_ref[...], b_ref[...])


def _mix_out(x, oa, obs, lses, expand, norm_b, w_out, ln_g, ln_b):
    n = x.shape[0]
    row = lambda w: pl.BlockSpec((OUT_TILE, w), lambda i: (i, 0))
    full = lambda r, c: pl.BlockSpec((r, c), lambda i: (0, 0), pipeline_mode=pl.Buffered(1))
    strided = lambda w: [pl.BlockSpec((OUT_TILE // d, d * w), lambda i: (i, 0)) for _, d in B_DILATIONS]
    n_br = len(B_DILATIONS)
    return pl.pallas_call(
        _mix_out_kernel,
        grid=(n // OUT_TILE,),
        in_specs=[row(D_MODEL), row(A_WIDTH), *strided(B_WIDTH), *strided(LANES),
                  full(2 * LANES, B_WIDTH), full(1, B_WIDTH), full(D_MODEL, D_MODEL),
                  full(1, D_MODEL), full(1, D_MODEL)],
        out_specs=row(D_MODEL),
        out_shape=jax.ShapeDtypeStruct((n, D_MODEL), F32),
        scratch_shapes=[pltpu.VMEM((n_br * B_WIDTH // LANES, OUT_TILE, LANES), F32),
                        pltpu.VMEM((n_br, OUT_TILE, LANES), F32)],
        compiler_params=_params("parallel"),
        name="mix_out",
    )(x, oa, *obs, *lses, expand, norm_b, w_out, ln_g, ln_b)


def _silu(v):
    return v / (1.0 + jnp.exp(-v))


def _route(scores, biased):
    def top2_sum(c):
        hi01, lo01 = jnp.maximum(c[0], c[1]), jnp.minimum(c[0], c[1])
        hi23, lo23 = jnp.maximum(c[2], c[3]), jnp.minimum(c[2], c[3])
        first = jnp.maximum(hi01, hi23)
        second = jnp.maximum(jnp.minimum(hi01, hi23), jnp.where(hi01 >= hi23, lo01, lo23))
        return first + second

    groups = [biased[g * EXPERTS_PER_GROUP:(g + 1) * EXPERTS_PER_GROUP] for g in range(N_GROUPS)]
    gscore = [top2_sum(c) for c in groups]
    sel = jnp.zeros_like(gscore[0], dtype=jnp.int32)
    best = gscore[0]
    for g in range(1, N_GROUPS):
        better = gscore[g] > best
        sel = jnp.where(better, g, sel)
        best = jnp.where(better, gscore[g], best)

    def pick(cols):
        out = []
        for j in range(EXPERTS_PER_GROUP):
            v = cols[j]
            for g in range(1, N_GROUPS):
                v = jnp.where(sel == g, cols[g * EXPERTS_PER_GROUP + j], v)
            out.append(v)
        return out

    cb, cs = pick(biased), pick(scores)
    i1 = jnp.zeros_like(sel)
    v1 = cb[0]
    for j in range(1, EXPERTS_PER_GROUP):
        better = cb[j] > v1
        i1 = jnp.where(better, j, i1)
        v1 = jnp.where(better, cb[j], v1)
    i2 = jnp.where(i1 == 0, 1, 0)
    v2 = jnp.where(i1 == 0, cb[1], cb[0])
    for j in range(1, EXPERTS_PER_GROUP):
        better = (cb[j] > v2) & (i1 != j) & (i2 != j)
        i2 = jnp.where(better, j, i2)
        v2 = jnp.where(better, cb[j], v2)
    s1 = sum(jnp.where(i1 == j, cs[j], 0.0) for j in range(EXPERTS_PER_GROUP))
    s2 = sum(jnp.where(i2 == j, cs[j], 0.0) for j in range(EXPERTS_PER_GROUP))
    total = s1 + s2
    gates = []
    for e in range(N_EXPERTS):
        g, j = divmod(e, EXPERTS_PER_GROUP)
        in_group = sel == g
        gate = jnp.where(in_group & (i1 == j), s1 / total, 0.0) + jnp.where(in_group & (i2 == j), s2 / total, 0.0)
        gates.append(gate)
    return gates, sel


def _split_bf16(v):
    hi = v.astype(BF16)
    return hi, (v - hi.astype(F32)).astype(BF16)


def _moe_kernel(x_ref, tri_ref, wr_ref, rb_ref, wg_ref, wu_ref, wd_ref, sg_ref, su_ref, sd_ref,
                g_ref, b_ref, y_ref, yb_ref, xb_ref, xs_ref, gs_ref, ys_ref, pos_ref, off_ref):
    grp = pl.program_id(1)
    tile = MOE_SORT
    n_chunks = tile // SORT_CHUNK
    sections = range(MOE_TILE // MOE_SORT)

    def perm_rows(s, c):
        slot = (lax.broadcasted_iota(jnp.int32, (SORT_CHUNK, tile), 0) + c * SORT_CHUNK).astype(F32)
        return jnp.where(pos_ref[s] == slot, 1.0, 0.0).astype(BF16)

    def route_and_sort(s):
        x = x_ref[s * tile:(s + 1) * tile, :]
        xb = x.astype(BF16)
        xb_ref[s * tile:(s + 1) * tile, :] = xb
        x_lo = (x - xb.astype(F32)).astype(BF16)
        w_hi, w_lo = _split_bf16(wr_ref[...])
        logits = (_dot_nt(jnp.concatenate([w_hi, w_lo], axis=0), xb)
                  + jnp.concatenate([_dot_nt(w_hi, x_lo), jnp.zeros((N_EXPERTS, tile), F32)], axis=0))
        logits = logits[:N_EXPERTS] + logits[N_EXPERTS:]
        scores = 1.0 / (1.0 + jnp.exp(-logits))
        biased = scores + rb_ref[...]
        rows = lambda t: [t[j:j + 1, :] for j in range(N_EXPERTS)]
        gates, sel = _route(rows(scores), rows(biased))
        sub = lax.broadcasted_iota(jnp.int32, (8, tile), 0)
        onehot = jnp.where(sub == sel, 1.0, 0.0).astype(BF16)
        rank = jnp.dot(onehot, tri_ref[...], preferred_element_type=F32)
        last_lane = lax.broadcasted_iota(jnp.int32, (1, tile), 1) == tile - 1
        pos = jnp.zeros((1, tile), F32)
        start = jnp.float32(0.0)
        off_ref[s, 0] = 0
        for g in range(N_GROUPS):
            pos = jnp.where(sel == g, start + rank[g:g + 1, :] - 1.0, pos)
            start = start + jnp.sum(jnp.where(last_lane, rank[g:g + 1, :], 0.0))
            off_ref[s, g + 1] = start.astype(jnp.int32)
        pos_ref[s] = pos
        gate_rows = jnp.concatenate(gates + [jnp.zeros((LANES - N_EXPERTS, tile), F32)], axis=0)
        g_split = jnp.concatenate(_split_bf16(gate_rows), axis=0)
        for c in range(n_chunks):
            perm = perm_rows(s, c)
            rows_c = slice(c * SORT_CHUNK, (c + 1) * SORT_CHUNK)
            xs_ref[s, rows_c, :] = jnp.dot(perm, xb, preferred_element_type=F32).astype(BF16)
            g2 = _dot_nt(perm, g_split)
            gs_ref[s, rows_c, :] = g2[:, :LANES] + g2[:, LANES:]
        xs_ref[s, tile:, :] = jnp.zeros((MOE_CHUNK, D_MODEL), BF16)
        gs_ref[s, tile:, :] = jnp.zeros((MOE_CHUNK, LANES), F32)
        ys_ref[s] = jnp.zeros(ys_ref.shape[1:], F32)

    def run_group():
        begins, trips = [], []
        for s in sections:
            first, end = off_ref[s, grp], off_ref[s, grp + 1]
            begin = lax.shift_left(lax.shift_right_logical(first, ROW_ALIGN_LOG2), ROW_ALIGN_LOG2)
            begins.append(begin)
            trips.append(jnp.where(end > first, lax.div(end - begin + (MOE_CHUNK - 1), MOE_CHUNK), 0))
        chunk_lane = lax.broadcasted_iota(jnp.int32, (len(sections) * MOE_CHUNK, LANES), 1)

        def chunk(c, carry):
            starts = [pl.multiple_of(jnp.minimum(b + c * MOE_CHUNK, tile), 1 << ROW_ALIGN_LOG2) for b in begins]
            xc = jnp.concatenate([xs_ref[s, pl.ds(r0, MOE_CHUNK), :] for s, r0 in zip(sections, starts)], axis=0)
            gc = jnp.concatenate([gs_ref[s, pl.ds(r0, MOE_CHUNK), :] for s, r0 in zip(sections, starts)], axis=0)
            yc = None
            for j in range(EXPERTS_PER_GROUP):
                gate = jnp.sum(jnp.where(chunk_lane == grp * EXPERTS_PER_GROUP + j, gc, 0.0), axis=-1, keepdims=True)
                h = _silu(jnp.dot(xc, wg_ref[j], preferred_element_type=F32)) * jnp.dot(
                    xc, wu_ref[j], preferred_element_type=F32)
                part = jnp.dot((h * gate).astype(BF16), wd_ref[j], preferred_element_type=F32)
                yc = part if yc is None else yc + part
            for s, r0 in zip(sections, starts):
                ys_ref[s, pl.ds(r0, MOE_CHUNK), :] += yc[s * MOE_CHUNK:(s + 1) * MOE_CHUNK]
            return carry

        lax.fori_loop(0, functools.reduce(jnp.maximum, trips), chunk, 0)

    def unsort_and_norm(s):
        rows_s = slice(s * tile, (s + 1) * tile)
        xb = xb_ref[rows_s, :]
        hs = _silu(jnp.dot(xb, sg_ref[...], preferred_element_type=F32)) * jnp.dot(
            xb, su_ref[...], preferred_element_type=F32)
        acc = jnp.dot(hs.astype(BF16), sd_ref[...], preferred_element_type=F32)
        for c in range(n_chunks):
            acc = acc + lax.dot_general(perm_rows(s, c),
                                        ys_ref[s, c * SORT_CHUNK:(c + 1) * SORT_CHUNK, :].astype(BF16),
                                        (((0,), (0,)), ((), ())), preferred_element_type=F32)
        y = _layer_norm(DEEPNORM_ALPHA * x_ref[rows_s, :] + acc, g_ref[...], b_ref[...])
        y_ref[rows_s, :] = y
        yb_ref[rows_s, :] = y.astype(BF16)

    @pl.when(grp == 0)
    def _():
        for s in sections:
            route_and_sort(s)

    run_group()

    @pl.when(grp == N_GROUPS - 1)
    def _():
        for s in sections:
            unsort_and_norm(s)


def _moe(x, tri, w_router, router_bias, w_gate, w_up, w_down, ws_gate, ws_up, ws_down, ln_g, ln_b):
    n = x.shape[0]
    row = pl.BlockSpec((MOE_TILE, D_MODEL), lambda i, g: (i, 0))
    full = lambda r, c: pl.BlockSpec((r, c), lambda i, g: (0, 0), pipeline_mode=pl.Buffered(1))
    group_w = lambda r, c: pl.BlockSpec((EXPERTS_PER_GROUP, r, c), lambda i, g: (g, 0, 0))
    n_sec, sec_rows = MOE_TILE // MOE_SORT, MOE_SORT + MOE_CHUNK
    return pl.pallas_call(
        _moe_kernel,
        grid=(n // MOE_TILE, N_GROUPS),
        in_specs=[row, full(MOE_SORT, MOE_SORT), full(N_EXPERTS, D_MODEL), full(N_EXPERTS, 1),
                  group_w(D_MODEL, D_EXPERT), group_w(D_MODEL, D_EXPERT), group_w(D_EXPERT, D_MODEL),
                  full(D_MODEL, D_EXPERT), full(D_MODEL, D_EXPERT), full(D_EXPERT, D_MODEL),
                  full(1, D_MODEL), full(1, D_MODEL)],
        out_specs=[row, row],
        out_shape=[jax.ShapeDtypeStruct((n, D_MODEL), F32), jax.ShapeDtypeStruct((n, D_MODEL), BF16)],
        scratch_shapes=[pltpu.VMEM((MOE_TILE, D_MODEL), BF16),
                        pltpu.VMEM((n_sec, sec_rows, D_MODEL), BF16),
                        pltpu.VMEM((n_sec, sec_rows, LANES), F32),
                        pltpu.VMEM((n_sec, sec_rows, D_MODEL), F32),
                        pltpu.VMEM((n_sec, 1, MOE_SORT), F32),
                        pltpu.SMEM((n_sec, N_GROUPS + 1), jnp.int32)],
        compiler_params=pltpu.CompilerParams(dimension_semantics=("parallel", "arbitrary"),
                                             vmem_limit_bytes=MOE_VMEM_LIMIT),
        name="moe",
    )(x, tri, w_router, router_bias, w_gate, w_up, w_down, ws_gate, ws_up, ws_down, ln_g, ln_b)


def _alibi_slopes():
    n_heads = A_HEADS + B_HEADS
    h = jnp.arange(1, n_heads + 1, dtype=F32)
    return jnp.exp2(-8.0 * h / n_heads)


def _layer(x, xb, p, l, shape):
    bsz, seq_len = shape
    tokens = bsz * seq_len
    qa, ka, va, qb, kb, vb = _inproj(xb, p["w_in"][l])
    seq = lambda t: t.reshape(bsz, seq_len, t.shape[-1])
    oa = _attn_a(seq(qa), seq(ka), seq(va), p["sink2"][l], p["slopes_a"], p["lane_masks"], p["norm_a"][l])
    obs, lses = [], []
    for idx, (window, dilation) in enumerate(B_DILATIONS):
        o, lse = _attn_b_branch(qb[idx], kb[idx], vb[idx], p["slopes_b"], p["lane_masks"], bsz, window, dilation)
        obs.append(o)
        lses.append(lse)
    x = _mix_out(x, oa.reshape(tokens, A_WIDTH), obs, lses, p["expand"], p["norm_b"][l],
                 p["w_out"][l], p["ln1_g"][l], p["ln1_b"][l])
    return _moe(x, p["tri"], p["w_router"], p["router_bias"], p["w_gate"][l], p["w_up"][l], p["w_down"][l],
                p["ws_gate"][l], p["ws_up"][l], p["ws_down"][l], p["ln2_g"][l], p["ln2_b"][l])


def _trunk(x, p):
    bsz, seq_len, _ = x.shape
    h = x.reshape(bsz * seq_len, D_MODEL)
    hb = h
    for l in range(DEPTH):
        h, hb = _layer(h, hb, p, l, (bsz, seq_len))
    return h.reshape(bsz, seq_len, D_MODEL)


def _prepare(w_in, sink_a, norm_a, norm_b, w_out, ln1_g, ln1_b, w_router, router_bias,
             w_gate, w_up, w_down, ws_gate, ws_up, ws_down, ln2_g, ln2_b):
    slopes = _alibi_slopes()
    head_lane = jnp.arange(B_WIDTH)[None, :] // HEAD_DIM
    lane_half = jnp.arange(LANES)[None, :] // HEAD_DIM

    def dup_kv(w):
        h0, h1 = w[..., :HEAD_DIM], w[..., HEAD_DIM:]
        return jnp.concatenate([h0, h0, h1, h1], axis=-1)

    lo = A_WIDTH
    w_proj = jnp.concatenate([w_in[..., :lo], dup_kv(w_in[..., lo:lo + A_KV_WIDTH]),
                              dup_kv(w_in[..., lo + A_KV_WIDTH:lo + 2 * A_KV_WIDTH]),
                              w_in[..., lo + 2 * A_KV_WIDTH:]], axis=-1)
    return {
        "w_in": w_proj.astype(BF16),
        "sink2": sink_a.astype(F32) * LOG2E,
        "slopes_a": slopes[0::2], "slopes_b": slopes[1::2],
        "lane_masks": (lane_half == jnp.arange(2)[:, None]).astype(BF16),
        "expand": (jnp.arange(2 * LANES)[:, None] % LANES == head_lane).astype(BF16),
        "tri": jnp.tri(MOE_SORT, dtype=BF16).T,
        "norm_a": norm_a.reshape(DEPTH, 1, A_WIDTH),
        "norm_b": norm_b.reshape(DEPTH, 1, B_WIDTH),
        "w_out": w_out.astype(BF16),
        "ln1_g": ln1_g.reshape(DEPTH, 1, D_MODEL), "ln1_b": ln1_b.reshape(DEPTH, 1, D_MODEL),
        "w_router": w_router.T, "router_bias": router_bias.reshape(N_EXPERTS, 1),
        "w_gate": w_gate.astype(BF16), "w_up": w_up.astype(BF16), "w_down": w_down.astype(BF16),
        "ws_gate": ws_gate.astype(BF16), "ws_up": ws_up.astype(BF16), "ws_down": ws_down.astype(BF16),
        "ln2_g": ln2_g.reshape(DEPTH, 1, D_MODEL), "ln2_b": ln2_b.reshape(DEPTH, 1, D_MODEL),
    }


def kernel(x_prompt, x_sample, w_in, sink_a, norm_a, norm_b, w_out, ln1_g, ln1_b, w_router, router_bias,
           w_gate, w_up, w_down, ws_gate, ws_up, ws_down, ln2_g, ln2_b):
    p = _prepare(w_in, sink_a, norm_a, norm_b, w_out, ln1_g, ln1_b, w_router, router_bias,
                 w_gate, w_up, w_down, ws_gate, ws_up, ws_down, ln2_g, ln2_b)
    return _trunk(x_prompt, p), _trunk(x_sample, p)
```

```python
import functools

import jax
import jax.numpy as jnp
from jax import lax
from jax.experimental import pallas as pl
from jax.experimental.pallas import tpu as pltpu

F32 = jnp.float32
BF16 = jnp.bfloat16

D_MODEL = 1024
DEPTH = 4
HEAD_DIM = 64
A_HEADS = 8
A_KV_HEADS = 2
A_RADIUS = 128
B_HEADS = 8
B_DILATIONS = ((128, 1), (512, 4), (2048, 16))
A_WIDTH = A_HEADS * HEAD_DIM
A_KV_WIDTH = A_KV_HEADS * HEAD_DIM
B_WIDTH = B_HEADS * HEAD_DIM
N_EXPERTS = 16
N_GROUPS = 4
EXPERTS_PER_GROUP = N_EXPERTS // N_GROUPS
D_EXPERT = 256
DEEPNORM_ALPHA = (2 * DEPTH) ** 0.25
LN_EPS = 1e-5
RMS_EPS = 1e-6
LOG2E = 1.4426950408889634

LANES = 128
MASK_PENALTY = 1e30
SUBLANES = 8
V7X_VMEM_BYTES = 64 * 1024 * 1024
VMEM_LIMIT = V7X_VMEM_BYTES * 3 // 4

Q_TILE = 128
Q_BLOCK = 1024
TILE_UNROLL = 4
WIDE_TILE_UNROLL = 8
PROJ_TILE = 1024
PROJ_ROWS = 256
OUT_TILE = 1024
MOE_TILE = 1024
MOE_SORT = 512
MOE_CHUNK = 160
SORT_CHUNK = 256
ROW_ALIGN_LOG2 = 4
MOE_VMEM_LIMIT = V7X_VMEM_BYTES * 7 // 8
STRIDED_DILATIONS = tuple(d for _, d in B_DILATIONS if d > 1)
assert all(b % a == 0 for a, b in zip((1,) + STRIDED_DILATIONS, STRIDED_DILATIONS)), "layouts are built level by level"
A_HEAD_ORDER = tuple(c + (A_HEADS // A_KV_HEADS) * half for c in range(A_HEADS // A_KV_HEADS) for half in range(2))
PROJ_WIDTH = A_WIDTH + 2 * A_KV_WIDTH + 3 * B_WIDTH


def _params(*sem):
    return pltpu.CompilerParams(dimension_semantics=sem, vmem_limit_bytes=VMEM_LIMIT)


def _dot_nt(a, b):
    return lax.dot_general(a, b, (((1,), (1,)), ((), ())), preferred_element_type=F32)


def _inproj_kernel(x_ref, w_ref, qa_ref, ka_ref, va_ref, *b_refs):
    q_scale = HEAD_DIM ** -0.5 * LOG2E

    proj = jnp.dot(x_ref[...].astype(BF16), w_ref[...], preferred_element_type=F32)
    qa_ref[...] = (proj[:, :A_WIDTH] * q_scale).astype(BF16)
    ka_ref[...] = proj[:, A_WIDTH:A_WIDTH + A_KV_WIDTH].astype(BF16)
    va_ref[...] = proj[:, A_WIDTH + A_KV_WIDTH:A_WIDTH + 2 * A_KV_WIDTH].astype(BF16)
    lo = A_WIDTH + 2 * A_KV_WIDTH
    n_layouts = 1 + len(STRIDED_DILATIONS)
    for j in range(3):
        outs = b_refs[j * n_layouts:(j + 1) * n_layouts]
        for rb in range(PROJ_TILE // PROJ_ROWS):
            rows_rb = slice(rb * PROJ_ROWS, (rb + 1) * PROJ_ROWS)
            r = proj[rows_rb, lo + j * B_WIDTH:lo + (j + 1) * B_WIDTH]
            if j == 0:
                r = r * q_scale
            outs[0][rows_rb, :] = r.astype(BF16)
            for c in range(B_WIDTH // LANES):
                pieces = [r[:, c * LANES:(c + 1) * LANES]]
                d = 1
                for d_next, out in zip(STRIDED_DILATIONS, outs[1:]):
                    f = d_next // d
                    split = []
                    for res, piece in enumerate(pieces):
                        parts = pltpu.einshape("mrc->rmc", piece.reshape(piece.shape[0] // f, f, LANES))
                        split.append([parts[q] for q in range(f)])
                    pieces = [split[res][q] for q in range(f) for res in range(d)]
                    d = d_next
                    n_rows = PROJ_ROWS // d
                    for res, piece in enumerate(pieces):
                        lo_c = res * B_WIDTH + c * LANES
                        out[rb * n_rows:(rb + 1) * n_rows, lo_c:lo_c + LANES] = piece.astype(BF16)


def _inproj(x, w_in):
    n = x.shape[0]
    shapes = [(n, A_WIDTH), (n, A_KV_WIDTH), (n, A_KV_WIDTH)]
    blocks = [(PROJ_TILE, A_WIDTH), (PROJ_TILE, A_KV_WIDTH), (PROJ_TILE, A_KV_WIDTH)]
    for _ in range(3):
        for d in (1,) + STRIDED_DILATIONS:
            shapes.append((n // d, d * B_WIDTH))
            blocks.append((PROJ_TILE // d, d * B_WIDTH))
    outs = pl.pallas_call(
        _inproj_kernel,
        grid=(n // PROJ_TILE,),
        in_specs=[pl.BlockSpec((PROJ_TILE, D_MODEL), lambda i: (i, 0)),
                  pl.BlockSpec((D_MODEL, PROJ_WIDTH), lambda i: (0, 0), pipeline_mode=pl.Buffered(1))],
        out_specs=[pl.BlockSpec(b, lambda i: (i, 0)) for b in blocks],
        out_shape=[jax.ShapeDtypeStruct(s, BF16) for s in shapes],
        compiler_params=_params("parallel"),
        name="inproj",
    )(x, w_in)
    n_layouts = 1 + len(STRIDED_DILATIONS)
    qa, ka, va = outs[:3]
    qb, kb, vb = (outs[3 + j * n_layouts:3 + (j + 1) * n_layouts] for j in range(3))
    return qa, ka, va, qb, kb, vb


def _band_bias_table(slopes, radius, width, dist_scale):
    qi = jnp.arange(Q_TILE)[:, None]
    kj = jnp.arange(width)[None, :]
    cases = []
    for offset in (0, -radius, Q_TILE - width):
        dist = jnp.abs(kj - qi + offset)
        pen = jnp.where(dist <= radius, dist.astype(F32) * (dist_scale * LOG2E), MASK_PENALTY)
        cases.append(-slopes[:, None, None] * pen[None])
    return jnp.stack(cases, axis=0)


def _tile_case(q0, seq_len):
    return jnp.where(q0 == 0, 0, jnp.where(q0 == seq_len - Q_TILE, 2, 1))


def _with_ones(v):
    return jnp.concatenate([v, jnp.ones(v.shape, v.dtype)], axis=-1)


def _attn_a_kernel(sink_ref, mask_ref, norm_ref, bias_ref, q_ref, k_ref, v_ref, o_ref, acc_ref, *, seq_len):
    width = Q_TILE + 2 * A_RADIUS
    blk = pl.program_id(1)
    lane = lax.broadcasted_iota(jnp.int32, (Q_TILE, LANES), 1)

    def tile(t, carry):
        r0 = pl.multiple_of(t * Q_TILE, Q_TILE)
        q0 = blk * Q_BLOCK + r0
        ks = pl.multiple_of(jnp.clip(q0 - A_RADIUS, 0, seq_len - width), Q_TILE)
        case = _tile_case(q0, seq_len)
        kg = k_ref[0, pl.ds(ks, width), :]
        vg = v_ref[0, pl.ds(ks, width), :]
        for g in range(A_WIDTH // LANES // 2):
            heads = [(c, half) for c in (2 * g, 2 * g + 1) for half in range(2)]
            q4 = jnp.concatenate([q_ref[0, pl.ds(r0, Q_TILE), c * LANES:(c + 1) * LANES] * mask_ref[half:half + 1, :]
                                  for c, half in heads], axis=0)
            s4 = _dot_nt(q4, kg)
            probs, inv = [], []
            for j, (c, half) in enumerate(heads):
                h = 2 * c + half
                sb = s4[j * Q_TILE:(j + 1) * Q_TILE] + bias_ref[case, h]
                sink = sink_ref[h]
                m = jnp.maximum(jnp.max(sb, axis=-1, keepdims=True), sink)
                p = jnp.exp2(sb - m)
                inv.append(1.0 / (jnp.sum(p, axis=-1, keepdims=True) + jnp.exp2(sink - m)))
                probs.append(p.astype(BF16))
            o4 = jnp.dot(jnp.concatenate(probs, axis=0), vg, preferred_element_type=F32)
            for jc, c in enumerate((2 * g, 2 * g + 1)):
                lo = o4[(2 * jc) * Q_TILE:(2 * jc + 1) * Q_TILE] * inv[2 * jc]
                hi = o4[(2 * jc + 1) * Q_TILE:(2 * jc + 2) * Q_TILE] * inv[2 * jc + 1]
                acc_ref[:, c * LANES:(c + 1) * LANES] = jnp.where(lane < HEAD_DIM, lo, hi)
        o = acc_ref[...]
        ms = jnp.mean(o * o, axis=-1, keepdims=True)
        o_ref[0, pl.ds(r0, Q_TILE), :] = (o * lax.rsqrt(ms + RMS_EPS) * norm_ref[...]).astype(BF16)
        return carry

    lax.fori_loop(0, Q_BLOCK // Q_TILE, tile, 0, unroll=WIDE_TILE_UNROLL)


def _attn_a(qa, ka, va, sinks, slopes, lane_masks, norm_a):
    bsz, seq_len, _ = qa.shape
    width = Q_TILE + 2 * A_RADIUS
    bias = _band_bias_table(slopes, A_RADIUS, width, 1.0)
    kv_spec = pl.BlockSpec((1, seq_len, A_KV_WIDTH), lambda b, i: (b, 0, 0))
    return pl.pallas_call(
        functools.partial(_attn_a_kernel, seq_len=seq_len),
        grid=(bsz, seq_len // Q_BLOCK),
        in_specs=[pl.BlockSpec(memory_space=pltpu.SMEM),
                  pl.BlockSpec((2, LANES), lambda b, i: (0, 0)),
                  pl.BlockSpec((1, A_WIDTH), lambda b, i: (0, 0)),
                  pl.BlockSpec(bias.shape, lambda b, i: (0, 0, 0, 0), pipeline_mode=pl.Buffered(1)),
                  pl.BlockSpec((1, Q_BLOCK, A_WIDTH), lambda b, i: (b, i, 0)),
                  kv_spec, kv_spec],
        out_specs=pl.BlockSpec((1, Q_BLOCK, A_WIDTH), lambda b, i: (b, i, 0)),
        out_shape=jax.ShapeDtypeStruct((bsz, seq_len, A_WIDTH), BF16),
        scratch_shapes=[pltpu.VMEM((Q_TILE, A_WIDTH), F32)],
        compiler_params=_params("parallel", "arbitrary"),
        name="attn_window",
    )(sinks, lane_masks, norm_a, bias, qa, ka, va)


def _attn_b_kernel(mask_ref, bias_ref, q_ref, k_ref, v_ref, o_ref, lse_ref, *,
                   seq_len, q_block, res_block, radius):
    width = min(Q_TILE + 2 * radius, seq_len)
    mxu_sums = width >= 2 * LANES
    blk = pl.program_id(2)
    lane = lax.broadcasted_iota(jnp.int32, (Q_TILE, LANES), 1)

    for rr in range(res_block):
        base = rr * B_WIDTH

        def tile(t, carry, base=base, rr=rr):
            r0 = pl.multiple_of(t * Q_TILE, Q_TILE)
            q0 = blk * q_block + r0
            ks = pl.multiple_of(jnp.clip(q0 - radius, 0, seq_len - width), radius)
            case = _tile_case(q0, seq_len)
            max_tile = jnp.zeros((Q_TILE, LANES), F32)
            den_tile = jnp.ones((Q_TILE, LANES), F32)
            for c in range(B_WIDTH // LANES):
                cs = slice(base + c * LANES, base + (c + 1) * LANES)
                qc = q_ref[0, pl.ds(r0, Q_TILE), cs]
                kc = k_ref[0, pl.ds(ks, width), cs]
                vc = v_ref[0, pl.ds(ks, width), cs]
                if mxu_sums:
                    vc = _with_ones(vc)
                q2 = jnp.concatenate([qc * mask_ref[0:1, :], qc * mask_ref[1:2, :]], axis=0)
                s2 = _dot_nt(q2, kc)
                probs, maxes, sums = [], [], []
                for half in range(2):
                    s = s2[half * Q_TILE:(half + 1) * Q_TILE] + bias_ref[case, 2 * c + half]
                    m = jnp.max(s, axis=-1, keepdims=True)
                    p = jnp.exp2(s - m)
                    if not mxu_sums:
                        sums.append(jnp.sum(p, axis=-1, keepdims=True))
                    probs.append(p.astype(BF16))
                    maxes.append(m)
                o2 = jnp.dot(jnp.concatenate(probs, axis=0), vc, preferred_element_type=F32)
                if mxu_sums:
                    sums = [o2[:Q_TILE, LANES:], o2[Q_TILE:, LANES:]]
                low = lane < HEAD_DIM
                num = jnp.where(low, o2[:Q_TILE, :LANES], o2[Q_TILE:, :LANES])
                o_ref[0, pl.ds(r0, Q_TILE), cs] = (num / jnp.where(low, sums[0], sums[1])).astype(BF16)
                for half in range(2):
                    here = lane == 2 * c + half
                    max_tile = jnp.where(here, maxes[half], max_tile)
                    den_tile = jnp.where(here, sums[half], den_tile)
            lse_ref[0, pl.ds(r0, Q_TILE), rr * LANES:(rr + 1) * LANES] = max_tile + jnp.log2(den_tile)
            return carry

        lax.fori_loop(0, q_block // Q_TILE, tile, 0, unroll=min(WIDE_TILE_UNROLL, q_block // Q_TILE))


def _attn_b_branch(qv, kv, vv, slopes, lane_masks, bsz, window, dilation):
    seq_len = qv.shape[0] // bsz
    radius = (window // 2) // dilation
    q_block = min(Q_BLOCK, seq_len)
    width = min(Q_TILE + 2 * radius, seq_len)
    tiles = WIDE_TILE_UNROLL if width >= 2 * LANES else TILE_UNROLL
    res_block = min(dilation, max(1, tiles * Q_TILE // q_block))
    bias = _band_bias_table(slopes, radius, width, float(dilation))
    view = lambda t: t.reshape(bsz, seq_len, t.shape[-1])
    kv_spec = pl.BlockSpec((1, seq_len, res_block * B_WIDTH), lambda b, r, i: (b, 0, r))
    o, lse = pl.pallas_call(
        functools.partial(_attn_b_kernel, seq_len=seq_len, q_block=q_block, res_block=res_block, radius=radius),
        grid=(bsz, dilation // res_block, seq_len // q_block),
        in_specs=[pl.BlockSpec((2, LANES), lambda b, r, i: (0, 0)),
                  pl.BlockSpec(bias.shape, lambda b, r, i: (0, 0, 0, 0), pipeline_mode=pl.Buffered(1)),
                  pl.BlockSpec((1, q_block, res_block * B_WIDTH), lambda b, r, i: (b, i, r)),
                  kv_spec, kv_spec],
        out_specs=[pl.BlockSpec((1, q_block, res_block * B_WIDTH), lambda b, r, i: (b, i, r)),
                   pl.BlockSpec((1, q_block, res_block * LANES), lambda b, r, i: (b, i, r))],
        out_shape=[jax.ShapeDtypeStruct((bsz, seq_len, dilation * B_WIDTH), BF16),
                   jax.ShapeDtypeStruct((bsz, seq_len, dilation * LANES), F32)],
        compiler_params=_params("parallel", "parallel", "arbitrary"),
        name=f"attn_dilated_{dilation}",
    )(lane_masks, bias, view(qv), view(kv), view(vv))
    return o.reshape(bsz * seq_len, dilation * B_WIDTH), lse.reshape(bsz * seq_len, dilation * LANES)


def _layer_norm(v, g, b):
    mu = jnp.mean(v, axis=-1, keepdims=True)
    c = v - mu
    var = jnp.mean(c * c, axis=-1, keepdims=True)
    return c * lax.rsqrt(var + LN_EPS) * g + b


def _mix_out_kernel(x_ref, oa_ref, *rest):
    n_br = len(B_DILATIONS)
    ob_refs, lse_refs = rest[:n_br], rest[n_br:2 * n_br]
    expand_ref, norm_b_ref, w_ref, g_ref, b_ref, y_ref, ob_stage, lse_stage = rest[2 * n_br:]
    obs, lses = [], []
    for idx, ((_, d), o_ref, l_ref) in enumerate(zip(B_DILATIONS, ob_refs, lse_refs)):
        if d == 1:
            obs.append(o_ref[...].astype(F32))
            lses.append(l_ref[...])
            continue
        rows = OUT_TILE // d
        n_cols = B_WIDTH // LANES
        for res in range(d):
            for c in range(n_cols):
                lo_c = res * B_WIDTH + c * LANES
                ob_stage[idx * n_cols + c, pl.ds(res, rows, stride=d), :] = o_ref[:, lo_c:lo_c + LANES].astype(F32)
            lse_stage[idx, pl.ds(res, rows, stride=d), :] = l_ref[:, res * LANES:(res + 1) * LANES]
        obs.append(jnp.concatenate([ob_stage[idx * n_cols + c] for c in range(n_cols)], axis=-1))
        lses.append(lse_stage[idx])
    top = functools.reduce(jnp.maximum, lses)
    es = [jnp.exp2(l - top) for l in lses]
    inv = 1.0 / functools.reduce(lambda a, b: a + b, es)
    ob = None
    for e, o in zip(es, obs):
        wgt = e * inv
        wide = jnp.dot(jnp.concatenate(_split_bf16(wgt), axis=-1), expand_ref[...], preferred_element_type=F32)
        ob = wide * o if ob is None else ob + wide * o
    ms = jnp.mean(ob * ob, axis=-1, keepdims=True)
    ob = (ob * lax.rsqrt(ms + RMS_EPS) * norm_b_ref[...]).astype(BF16)
    mix = jnp.dot(jnp.concatenate([oa_ref[...], ob], axis=-1), w_ref[...], preferred_element_type=F32)
    y_ref[...] = _layer_norm(DEEPNORM_ALPHA * x_ref[...] + mix, g_ref[...], b_ref[...])


def _mix_out(x, oa, obs, lses, expand, norm_b, w_out, ln_g, ln_b):
    n = x.shape[0]
    row = lambda w: pl.BlockSpec((OUT_TILE, w), lambda i: (i, 0))
    full = lambda r, c: pl.BlockSpec((r, c), lambda i: (0, 0), pipeline_mode=pl.Buffered(1))
    strided = lambda w: [pl.BlockSpec((OUT_TILE // d, d * w), lambda i: (i, 0)) for _, d in B_DILATIONS]
    n_br = len(B_DILATIONS)
    return pl.pallas_call(
        _mix_out_kernel,
        grid=(n // OUT_TILE,),
        in_specs=[row(D_MODEL), row(A_WIDTH), *strided(B_WIDTH), *strided(LANES),
                  full(2 * LANES, B_WIDTH), full(1, B_WIDTH), full(D_MODEL, D_MODEL),
                  full(1, D_MODEL), full(1, D_MODEL)],
        out_specs=row(D_MODEL),
        out_shape=jax.ShapeDtypeStruct((n, D_MODEL), F32),
        scratch_shapes=[pltpu.VMEM((n_br * B_WIDTH // LANES, OUT_TILE, LANES), F32),
                        pltpu.VMEM((n_br, OUT_TILE, LANES), F32)],
        compiler_params=_params("parallel"),
        name="mix_out",
    )(x, oa, *obs, *lses, expand, norm_b, w_out, ln_g, ln_b)


def _silu(v):
    return v / (1.0 + jnp.exp(-v))


def _route(scores, biased):
    def top2_sum(c):
        hi01, lo01 = jnp.maximum(c[0], c[1]), jnp.minimum(c[0], c[1])
        hi23, lo23 = jnp.maximum(c[2], c[3]), jnp.minimum(c[2], c[3])
        first = jnp.maximum(hi01, hi23)
        second = jnp.maximum(jnp.minimum(hi01, hi23), jnp.where(hi01 >= hi23, lo01, lo23))
        return first + second

    groups = [biased[g * EXPERTS_PER_GROUP:(g + 1) * EXPERTS_PER_GROUP] for g in range(N_GROUPS)]
    gscore = [top2_sum(c) for c in groups]
    sel = jnp.zeros_like(gscore[0], dtype=jnp.int32)
    best = gscore[0]
    for g in range(1, N_GROUPS):
        better = gscore[g] > best
        sel = jnp.where(better, g, sel)
        best = jnp.where(better, gscore[g], best)

    def pick(cols):
        out = []
        for j in range(EXPERTS_PER_GROUP):
            v = cols[j]
            for g in range(1, N_GROUPS):
                v = jnp.where(sel == g, cols[g * EXPERTS_PER_GROUP + j], v)
            out.append(v)
        return out

    cb, cs = pick(biased), pick(scores)
    i1 = jnp.zeros_like(sel)
    v1 = cb[0]
    for j in range(1, EXPERTS_PER_GROUP):
        better = cb[j] > v1
        i1 = jnp.where(better, j, i1)
        v1 = jnp.where(better, cb[j], v1)
    i2 = jnp.where(i1 == 0, 1, 0)
    v2 = jnp.where(i1 == 0, cb[1], cb[0])
    for j in range(1, EXPERTS_PER_GROUP):
        better = (cb[j] > v2) & (i1 != j) & (i2 != j)
        i2 = jnp.where(better, j, i2)
        v2 = jnp.where(better, cb[j], v2)
    s1 = sum(jnp.where(i1 == j, cs[j], 0.0) for j in range(EXPERTS_PER_GROUP))
    s2 = sum(jnp.where(i2 == j, cs[j], 0.0) for j in range(EXPERTS_PER_GROUP))
    total = s1 + s2
    gates = []
    for e in range(N_EXPERTS):
        g, j = divmod(e, EXPERTS_PER_GROUP)
        in_group = sel == g
        gate = jnp.where(in_group & (i1 == j), s1 / total, 0.0) + jnp.where(in_group & (i2 == j), s2 / total, 0.0)
        gates.append(gate)
    return gates, sel


def _split_bf16(v):
    hi = v.astype(BF16)
    return hi, (v - hi.astype(F32)).astype(BF16)


def _moe_kernel(x_ref, tri_ref, wr_ref, rb_ref, wgu_ref, wd_ref, sgu_ref, sd_ref,
                g_ref, b_ref, y_ref, yb_ref, xb_ref, xs_ref, gs_ref, ys_ref, pos_ref, off_ref):
    grp = pl.program_id(1)
    tile = MOE_SORT
    n_chunks = tile // SORT_CHUNK
    sections = range(MOE_TILE // MOE_SORT)

    def perm_rows(s, c):
        slot = (lax.broadcasted_iota(jnp.int32, (SORT_CHUNK, tile), 0) + c * SORT_CHUNK).astype(F32)
        return jnp.where(pos_ref[s] == slot, 1.0, 0.0).astype(BF16)

    def route_and_sort(s):
        x = x_ref[s * tile:(s + 1) * tile, :]
        xb = x.astype(BF16)
        xb_ref[s * tile:(s + 1) * tile, :] = xb
        x_lo = (x - xb.astype(F32)).astype(BF16)
        w_hi, w_lo = _split_bf16(wr_ref[...])
        logits = (_dot_nt(jnp.concatenate([w_hi, w_lo], axis=0), xb)
                  + jnp.concatenate([_dot_nt(w_hi, x_lo), jnp.zeros((N_EXPERTS, tile), F32)], axis=0))
        logits = logits[:N_EXPERTS] + logits[N_EXPERTS:]
        scores = 1.0 / (1.0 + jnp.exp(-logits))
        biased = scores + rb_ref[...]
        rows = lambda t: [t[j:j + 1, :] for j in range(N_EXPERTS)]
        gates, sel = _route(rows(scores), rows(biased))
        sub = lax.broadcasted_iota(jnp.int32, (SUBLANES, tile), 0)
        onehot = jnp.where(sub == sel, 1.0, 0.0).astype(BF16)
        rank = jnp.dot(onehot, tri_ref[...], preferred_element_type=F32)
        last_lane = lax.broadcasted_iota(jnp.int32, (1, tile), 1) == tile - 1
        pos = jnp.zeros((1, tile), F32)
        start = jnp.float32(0.0)
        off_ref[s, 0] = 0
        for g in range(N_GROUPS):
            pos = jnp.where(sel == g, start + rank[g:g + 1, :] - 1.0, pos)
            start = start + jnp.sum(jnp.where(last_lane, rank[g:g + 1, :], 0.0))
            off_ref[s, g + 1] = start.astype(jnp.int32)
        pos_ref[s] = pos
        gate_rows = jnp.concatenate(gates + [jnp.zeros((LANES - N_EXPERTS, tile), F32)], axis=0)
        g_split = jnp.concatenate(_split_bf16(gate_rows), axis=0)
        for c in range(n_chunks):
            perm = perm_rows(s, c)
            rows_c = slice(c * SORT_CHUNK, (c + 1) * SORT_CHUNK)
            xs_ref[s, rows_c, :] = jnp.dot(perm, xb, preferred_element_type=F32).astype(BF16)
            g2 = _dot_nt(perm, g_split)
            gs_ref[s, rows_c, :] = g2[:, :LANES] + g2[:, LANES:]
        xs_ref[s, tile:, :] = jnp.zeros((MOE_CHUNK, D_MODEL), BF16)
        gs_ref[s, tile:, :] = jnp.zeros((MOE_CHUNK, LANES), F32)
        ys_ref[s] = jnp.zeros(ys_ref.shape[1:], F32)

    def run_group():
        begins, trips = [], []
        for s in sections:
            first, end = off_ref[s, grp], off_ref[s, grp + 1]
            begin = lax.shift_left(lax.shift_right_logical(first, ROW_ALIGN_LOG2), ROW_ALIGN_LOG2)
            begins.append(begin)
            trips.append(jnp.where(end > first, lax.div(end - begin + (MOE_CHUNK - 1), MOE_CHUNK), 0))
        chunk_lane = lax.broadcasted_iota(jnp.int32, (len(sections) * MOE_CHUNK, LANES), 1)

        def chunk(c, carry):
            starts = [pl.multiple_of(jnp.minimum(b + c * MOE_CHUNK, tile), 1 << ROW_ALIGN_LOG2) for b in begins]
            xc = jnp.concatenate([xs_ref[s, pl.ds(r0, MOE_CHUNK), :] for s, r0 in zip(sections, starts)], axis=0)
            gc = jnp.concatenate([gs_ref[s, pl.ds(r0, MOE_CHUNK), :] for s, r0 in zip(sections, starts)], axis=0)
            yc = None
            for j in range(EXPERTS_PER_GROUP):
                gate = jnp.sum(jnp.where(chunk_lane == grp * EXPERTS_PER_GROUP + j, gc, 0.0), axis=-1, keepdims=True)
                gu = jnp.dot(xc, wgu_ref[j], preferred_element_type=F32)
                h = _silu(gu[:, :D_EXPERT]) * gu[:, D_EXPERT:]
                part = jnp.dot((h * gate).astype(BF16), wd_ref[j], preferred_element_type=F32)
                yc = part if yc is None else yc + part
            for s, r0 in zip(sections, starts):
                ys_ref[s, pl.ds(r0, MOE_CHUNK), :] += yc[s * MOE_CHUNK:(s + 1) * MOE_CHUNK]
            return carry

        lax.fori_loop(0, functools.reduce(jnp.maximum, trips), chunk, 0)

    def unsort_and_norm(s):
        rows_s = slice(s * tile, (s + 1) * tile)
        xb = xb_ref[rows_s, :]
        gu = jnp.dot(xb, sgu_ref[...], preferred_element_type=F32)
        hs = _silu(gu[:, :D_EXPERT]) * gu[:, D_EXPERT:]
        acc = jnp.dot(hs.astype(BF16), sd_ref[...], preferred_element_type=F32)
        for c in range(n_chunks):
            acc = acc + lax.dot_general(perm_rows(s, c),
                                        ys_ref[s, c * SORT_CHUNK:(c + 1) * SORT_CHUNK, :].astype(BF16),
                                        (((0,), (0,)), ((), ())), preferred_element_type=F32)
        y = _layer_norm(DEEPNORM_ALPHA * x_ref[rows_s, :] + acc, g_ref[...], b_ref[...])
        y_ref[rows_s, :] = y
        yb_ref[rows_s, :] = y.astype(BF16)

    @pl.when(grp == 0)
    def _():
        for s in sections:
            route_and_sort(s)

    run_group()

    @pl.when(grp == N_GROUPS - 1)
    def _():
        for s in sections:
            unsort_and_norm(s)


def _moe(x, tri, w_router, router_bias, w_gate_up, w_down, ws_gate_up, ws_down, ln_g, ln_b):
    n = x.shape[0]
    row = pl.BlockSpec((MOE_TILE, D_MODEL), lambda i, g: (i, 0))
    full = lambda r, c: pl.BlockSpec((r, c), lambda i, g: (0, 0), pipeline_mode=pl.Buffered(1))
    group_w = lambda r, c: pl.BlockSpec((EXPERTS_PER_GROUP, r, c), lambda i, g: (g, 0, 0))
    n_sec, sec_rows = MOE_TILE // MOE_SORT, MOE_SORT + MOE_CHUNK
    return pl.pallas_call(
        _moe_kernel,
        grid=(n // MOE_TILE, N_GROUPS),
        in_specs=[row, full(MOE_SORT, MOE_SORT), full(N_EXPERTS, D_MODEL), full(N_EXPERTS, 1),
                  group_w(D_MODEL, 2 * D_EXPERT), group_w(D_EXPERT, D_MODEL),
                  full(D_MODEL, 2 * D_EXPERT), full(D_EXPERT, D_MODEL),
                  full(1, D_MODEL), full(1, D_MODEL)],
        out_specs=[row, row],
        out_shape=[jax.ShapeDtypeStruct((n, D_MODEL), F32), jax.ShapeDtypeStruct((n, D_MODEL), BF16)],
        scratch_shapes=[pltpu.VMEM((MOE_TILE, D_MODEL), BF16),
                        pltpu.VMEM((n_sec, sec_rows, D_MODEL), BF16),
                        pltpu.VMEM((n_sec, sec_rows, LANES), F32),
                        pltpu.VMEM((n_sec, sec_rows, D_MODEL), F32),
                        pltpu.VMEM((n_sec, 1, MOE_SORT), F32),
                        pltpu.SMEM((n_sec, N_GROUPS + 1), jnp.int32)],
        compiler_params=pltpu.CompilerParams(dimension_semantics=("parallel", "arbitrary"),
                                             vmem_limit_bytes=MOE_VMEM_LIMIT),
        name="moe",
    )(x, tri, w_router, router_bias, w_gate_up, w_down, ws_gate_up, ws_down, ln_g, ln_b)


def _alibi_slopes():
    n_heads = A_HEADS + B_HEADS
    h = jnp.arange(1, n_heads + 1, dtype=F32)
    return jnp.exp2(-8.0 * h / n_heads)


def _layer(x, xb, p, l, shape):
    bsz, seq_len = shape
    tokens = bsz * seq_len
    qa, ka, va, qb, kb, vb = _inproj(xb, p["w_in"][l])
    seq = lambda t: t.reshape(bsz, seq_len, t.shape[-1])
    oa = _attn_a(seq(qa), seq(ka), seq(va), p["sink2"][l], p["slopes_a"], p["lane_masks"], p["norm_a"][l])
    obs, lses = [], []
    for idx, (window, dilation) in enumerate(B_DILATIONS):
        o, lse = _attn_b_branch(qb[idx], kb[idx], vb[idx], p["slopes_b"], p["lane_masks"], bsz, window, dilation)
        obs.append(o)
        lses.append(lse)
    x = _mix_out(x, oa.reshape(tokens, A_WIDTH), obs, lses, p["expand"], p["norm_b"][l],
                 p["w_out"][l], p["ln1_g"][l], p["ln1_b"][l])
    return _moe(x, p["tri"], p["w_router"], p["router_bias"], p["w_gate_up"][l], p["w_down"][l],
                p["ws_gate_up"][l], p["ws_down"][l], p["ln2_g"][l], p["ln2_b"][l])


def _trunk(x, p):
    bsz, seq_len, _ = x.shape
    h = x.reshape(bsz * seq_len, D_MODEL)
    hb = h
    for l in range(DEPTH):
        h, hb = _layer(h, hb, p, l, (bsz, seq_len))
    return h.reshape(bsz, seq_len, D_MODEL)


def _prepare(w_in, sink_a, norm_a, norm_b, w_out, ln1_g, ln1_b, w_router, router_bias,
             w_gate, w_up, w_down, ws_gate, ws_up, ws_down, ln2_g, ln2_b):
    slopes = _alibi_slopes()
    head_lane = jnp.arange(B_WIDTH)[None, :] // HEAD_DIM
    lane_half = jnp.arange(LANES)[None, :] // HEAD_DIM

    order = jnp.array(A_HEAD_ORDER)
    a_cols = (order[:, None] * HEAD_DIM + jnp.arange(HEAD_DIM)[None, :]).reshape(A_WIDTH)
    w_proj = jnp.concatenate([w_in[..., a_cols], w_in[..., A_WIDTH:]], axis=-1)
    w_mix = jnp.concatenate([w_out[:, a_cols, :], w_out[:, A_WIDTH:, :]], axis=1)
    return {
        "w_in": w_proj.astype(BF16),
        "sink2": sink_a.astype(F32)[:, order] * LOG2E,
        "slopes_a": slopes[0::2][order], "slopes_b": slopes[1::2],
        "lane_masks": (lane_half == jnp.arange(2)[:, None]).astype(BF16),
        "expand": (jnp.arange(2 * LANES)[:, None] % LANES == head_lane).astype(BF16),
        "tri": jnp.tri(MOE_SORT, dtype=BF16).T,
        "norm_a": norm_a[:, a_cols].reshape(DEPTH, 1, A_WIDTH),
        "norm_b": norm_b.reshape(DEPTH, 1, B_WIDTH),
        "w_out": w_mix.astype(BF16),
        "ln1_g": ln1_g.reshape(DEPTH, 1, D_MODEL), "ln1_b": ln1_b.reshape(DEPTH, 1, D_MODEL),
        "w_router": w_router.T, "router_bias": router_bias.reshape(N_EXPERTS, 1),
        "w_gate_up": jnp.concatenate([w_gate.astype(BF16), w_up.astype(BF16)], axis=-1),
        "w_down": w_down.astype(BF16),
        "ws_gate_up": jnp.concatenate([ws_gate.astype(BF16), ws_up.astype(BF16)], axis=-1),
        "ws_down": ws_down.astype(BF16),
        "ln2_g": ln2_g.reshape(DEPTH, 1, D_MODEL), "ln2_b": ln2_b.reshape(DEPTH, 1, D_MODEL),
    }


def kernel(x_prompt, x_sample, w_in, sink_a, norm_a, norm_b, w_out, ln1_g, ln1_b, w_router, router_bias,
           w_gate, w_up, w_down, ws_gate, ws_up, ws_down, ln2_g, ln2_b):
    p = _prepare(w_in, sink_a, norm_a, norm_b, w_out, ln1_g, ln1_b, w_router, router_bias,
                 w_gate, w_up, w_down, ws_gate, ws_up, ws_down, ln2_g, ln2_b)
    return _trunk(x_prompt, p), _trunk(x_sample, p)
```

```python
import functools

import jax
import jax.numpy as jnp
from jax import lax
from jax.experimental import pallas as pl
from jax.experimental.pallas import tpu as pltpu

F32 = jnp.float32
BF16 = jnp.bfloat16

D_MODEL = 1024
DEPTH = 4
HEAD_DIM = 64
A_HEADS = 8
A_KV_HEADS = 2
A_RADIUS = 128
B_HEADS = 8
B_DILATIONS = ((128, 1), (512, 4), (2048, 16))
A_WIDTH = A_HEADS * HEAD_DIM
A_KV_WIDTH = A_KV_HEADS * HEAD_DIM
B_WIDTH = B_HEADS * HEAD_DIM
N_EXPERTS = 16
N_GROUPS = 4
EXPERTS_PER_GROUP = N_EXPERTS // N_GROUPS
D_EXPERT = 256
DEEPNORM_ALPHA = (2 * DEPTH) ** 0.25
LN_EPS = 1e-5
RMS_EPS = 1e-6
LOG2E = 1.4426950408889634

LANES = 128
MASK_PENALTY = 1e30
SUBLANES = 8
V7X_VMEM_BYTES = 64 * 1024 * 1024
VMEM_LIMIT = V7X_VMEM_BYTES * 3 // 4

Q_TILE = 128
Q_BLOCK = 1024
TILE_UNROLL = 4
WIDE_TILE_UNROLL = 8
PROJ_TILE = 1024
PROJ_ROWS = 256
OUT_TILE = 1024
MOE_TILE = 1024
MOE_SORT = 512
MOE_CHUNK = 160
SORT_CHUNK = 256
ROW_ALIGN_LOG2 = 4
MOE_VMEM_LIMIT = V7X_VMEM_BYTES * 7 // 8
STRIDED_DILATIONS = tuple(d for _, d in B_DILATIONS if d > 1)
assert all(b % a == 0 for a, b in zip((1,) + STRIDED_DILATIONS, STRIDED_DILATIONS)), "layouts are built level by level"
A_HEAD_ORDER = tuple(c + (A_HEADS // A_KV_HEADS) * half for c in range(A_HEADS // A_KV_HEADS) for half in range(2))
PROJ_WIDTH = A_WIDTH + 2 * A_KV_WIDTH + 3 * B_WIDTH


def _params(*sem):
    return pltpu.CompilerParams(dimension_semantics=sem, vmem_limit_bytes=VMEM_LIMIT)


def _dot_nt(a, b):
    return lax.dot_general(a, b, (((1,), (1,)), ((), ())), preferred_element_type=F32)


def _inproj_kernel(x_ref, w_ref, qa_ref, ka_ref, va_ref, *b_refs):
    q_scale = HEAD_DIM ** -0.5 * LOG2E

    proj = jnp.dot(x_ref[...].astype(BF16), w_ref[...], preferred_element_type=F32)
    qa_ref[...] = (proj[:, :A_WIDTH] * q_scale).astype(BF16)
    ka_ref[...] = proj[:, A_WIDTH:A_WIDTH + A_KV_WIDTH].astype(BF16)
    va_ref[...] = proj[:, A_WIDTH + A_KV_WIDTH:A_WIDTH + 2 * A_KV_WIDTH].astype(BF16)
    lo = A_WIDTH + 2 * A_KV_WIDTH
    n_layouts = 1 + len(STRIDED_DILATIONS)
    for j in range(3):
        outs = b_refs[j * n_layouts:(j + 1) * n_layouts]
        for rb in range(PROJ_TILE // PROJ_ROWS):
            rows_rb = slice(rb * PROJ_ROWS, (rb + 1) * PROJ_ROWS)
            r = proj[rows_rb, lo + j * B_WIDTH:lo + (j + 1) * B_WIDTH]
            if j == 0:
                r = r * q_scale
            outs[0][rows_rb, :] = r.astype(BF16)
            for c in range(B_WIDTH // LANES):
                pieces = [r[:, c * LANES:(c + 1) * LANES]]
                d = 1
                for d_next, out in zip(STRIDED_DILATIONS, outs[1:]):
                    f = d_next // d
                    split = []
                    for res, piece in enumerate(pieces):
                        parts = pltpu.einshape("mrc->rmc", piece.reshape(piece.shape[0] // f, f, LANES))
                        split.append([parts[q] for q in range(f)])
                    pieces = [split[res][q] for q in range(f) for res in range(d)]
                    d = d_next
                    n_rows = PROJ_ROWS // d
                    for res, piece in enumerate(pieces):
                        lo_c = res * B_WIDTH + c * LANES
                        out[rb * n_rows:(rb + 1) * n_rows, lo_c:lo_c + LANES] = piece.astype(BF16)


def _inproj(x, w_in):
    n = x.shape[0]
    shapes = [(n, A_WIDTH), (n, A_KV_WIDTH), (n, A_KV_WIDTH)]
    blocks = [(PROJ_TILE, A_WIDTH), (PROJ_TILE, A_KV_WIDTH), (PROJ_TILE, A_KV_WIDTH)]
    for _ in range(3):
        for d in (1,) + STRIDED_DILATIONS:
            shapes.append((n // d, d * B_WIDTH))
            blocks.append((PROJ_TILE // d, d * B_WIDTH))
    outs = pl.pallas_call(
        _inproj_kernel,
        grid=(n // PROJ_TILE,),
        in_specs=[pl.BlockSpec((PROJ_TILE, D_MODEL), lambda i: (i, 0)),
                  pl.BlockSpec((D_MODEL, PROJ_WIDTH), lambda i: (0, 0), pipeline_mode=pl.Buffered(1))],
        out_specs=[pl.BlockSpec(b, lambda i: (i, 0)) for b in blocks],
        out_shape=[jax.ShapeDtypeStruct(s, BF16) for s in shapes],
        compiler_params=_params("parallel"),
        name="inproj",
    )(x, w_in)
    n_layouts = 1 + len(STRIDED_DILATIONS)
    qa, ka, va = outs[:3]
    qb, kb, vb = (outs[3 + j * n_layouts:3 + (j + 1) * n_layouts] for j in range(3))
    return qa, ka, va, qb, kb, vb


def _band_bias_table(slopes, radius, width, dist_scale):
    qi = jnp.arange(Q_TILE)[:, None]
    kj = jnp.arange(width)[None, :]
    cases = []
    for offset in (0, -radius, Q_TILE - width):
        dist = jnp.abs(kj - qi + offset)
        pen = jnp.where(dist <= radius, dist.astype(F32) * (dist_scale * LOG2E), MASK_PENALTY)
        cases.append(-slopes[:, None, None] * pen[None])
    return jnp.stack(cases, axis=0)


def _tile_case(q0, seq_len):
    return jnp.where(q0 == 0, 0, jnp.where(q0 == seq_len - Q_TILE, 2, 1))


def _with_ones(v):
    return jnp.concatenate([v, jnp.ones(v.shape, v.dtype)], axis=-1)


def _attn_a_kernel(sink_ref, mask_ref, norm_ref, bias_ref, q_ref, k_ref, v_ref, o_ref, acc_ref, *, seq_len):
    width = Q_TILE + 2 * A_RADIUS
    blk = pl.program_id(1)
    lane = lax.broadcasted_iota(jnp.int32, (Q_TILE, LANES), 1)

    def tile(t, carry):
        r0 = pl.multiple_of(t * Q_TILE, Q_TILE)
        q0 = blk * Q_BLOCK + r0
        ks = pl.multiple_of(jnp.clip(q0 - A_RADIUS, 0, seq_len - width), Q_TILE)
        case = _tile_case(q0, seq_len)
        kg = k_ref[0, pl.ds(ks, width), :]
        vg = v_ref[0, pl.ds(ks, width), :]
        for g in range(A_WIDTH // LANES // 2):
            heads = [(c, half) for c in (2 * g, 2 * g + 1) for half in range(2)]
            q4 = jnp.concatenate([q_ref[0, pl.ds(r0, Q_TILE), c * LANES:(c + 1) * LANES] * mask_ref[half:half + 1, :]
                                  for c, half in heads], axis=0)
            s4 = _dot_nt(q4, kg)
            probs, inv = [], []
            for j, (c, half) in enumerate(heads):
                h = 2 * c + half
                sb = s4[j * Q_TILE:(j + 1) * Q_TILE] + bias_ref[case, h]
                sink = sink_ref[h]
                m = jnp.maximum(jnp.max(sb, axis=-1, keepdims=True), sink)
                p = jnp.exp2(sb - m)
                inv.append(1.0 / (jnp.sum(p, axis=-1, keepdims=True) + jnp.exp2(sink - m)))
                probs.append(p.astype(BF16))
            o4 = jnp.dot(jnp.concatenate(probs, axis=0), vg, preferred_element_type=F32)
            for jc, c in enumerate((2 * g, 2 * g + 1)):
                lo = o4[(2 * jc) * Q_TILE:(2 * jc + 1) * Q_TILE] * inv[2 * jc]
                hi = o4[(2 * jc + 1) * Q_TILE:(2 * jc + 2) * Q_TILE] * inv[2 * jc + 1]
                acc_ref[:, c * LANES:(c + 1) * LANES] = jnp.where(lane < HEAD_DIM, lo, hi)
        o = acc_ref[...]
        ms = jnp.mean(o * o, axis=-1, keepdims=True)
        o_ref[0, pl.ds(r0, Q_TILE), :] = (o * lax.rsqrt(ms + RMS_EPS) * norm_ref[...]).astype(BF16)
        return carry

    lax.fori_loop(0, Q_BLOCK // Q_TILE, tile, 0, unroll=WIDE_TILE_UNROLL)


def _attn_a(qa, ka, va, sinks, slopes, lane_masks, norm_a):
    bsz, seq_len, _ = qa.shape
    width = Q_TILE + 2 * A_RADIUS
    bias = _band_bias_table(slopes, A_RADIUS, width, 1.0)
    kv_spec = pl.BlockSpec((1, seq_len, A_KV_WIDTH), lambda b, i: (b, 0, 0))
    return pl.pallas_call(
        functools.partial(_attn_a_kernel, seq_len=seq_len),
        grid=(bsz, seq_len // Q_BLOCK),
        in_specs=[pl.BlockSpec(memory_space=pltpu.SMEM),
                  pl.BlockSpec((2, LANES), lambda b, i: (0, 0)),
                  pl.BlockSpec((1, A_WIDTH), lambda b, i: (0, 0)),
                  pl.BlockSpec(bias.shape, lambda b, i: (0, 0, 0, 0), pipeline_mode=pl.Buffered(1)),
                  pl.BlockSpec((1, Q_BLOCK, A_WIDTH), lambda b, i: (b, i, 0)),
                  kv_spec, kv_spec],
        out_specs=pl.BlockSpec((1, Q_BLOCK, A_WIDTH), lambda b, i: (b, i, 0)),
        out_shape=jax.ShapeDtypeStruct((bsz, seq_len, A_WIDTH), BF16),
        scratch_shapes=[pltpu.VMEM((Q_TILE, A_WIDTH), F32)],
        compiler_params=_params("parallel", "arbitrary"),
        name="attn_window",
    )(sinks, lane_masks, norm_a, bias, qa, ka, va)


def _attn_b_kernel(mask_ref, bias_ref, q_ref, k_ref, v_ref, o_ref, lse_ref, *,
                   seq_len, q_block, res_block, radius):
    width = min(Q_TILE + 2 * radius, seq_len)
    mxu_sums = width >= 2 * LANES
    blk = pl.program_id(2)
    lane = lax.broadcasted_iota(jnp.int32, (Q_TILE, LANES), 1)

    for rr in range(res_block):
        base = rr * B_WIDTH

        def tile(t, carry, base=base, rr=rr):
            r0 = pl.multiple_of(t * Q_TILE, Q_TILE)
            q0 = blk * q_block + r0
            ks = pl.multiple_of(jnp.clip(q0 - radius, 0, seq_len - width), radius)
            case = _tile_case(q0, seq_len)
            max_tile = jnp.zeros((Q_TILE, LANES), F32)
            den_tile = jnp.ones((Q_TILE, LANES), F32)
            for c in range(B_WIDTH // LANES):
                cs = slice(base + c * LANES, base + (c + 1) * LANES)
                qc = q_ref[0, pl.ds(r0, Q_TILE), cs]
                kc = k_ref[0, pl.ds(ks, width), cs]
                vc = v_ref[0, pl.ds(ks, width), cs]
                if mxu_sums:
                    vc = _with_ones(vc)
                q2 = jnp.concatenate([qc * mask_ref[0:1, :], qc * mask_ref[1:2, :]], axis=0)
                s2 = _dot_nt(q2, kc)
                probs, maxes, sums = [], [], []
                for half in range(2):
                    s = s2[half * Q_TILE:(half + 1) * Q_TILE] + bias_ref[case, 2 * c + half]
                    m = jnp.max(s, axis=-1, keepdims=True)
                    p = jnp.exp2(s - m)
                    if not mxu_sums:
                        sums.append(jnp.sum(p, axis=-1, keepdims=True))
                    probs.append(p.astype(BF16))
                    maxes.append(m)
                o2 = jnp.dot(jnp.concatenate(probs, axis=0), vc, preferred_element_type=F32)
                if mxu_sums:
                    sums = [o2[:Q_TILE, LANES:], o2[Q_TILE:, LANES:]]
                low = lane < HEAD_DIM
                num = jnp.where(low, o2[:Q_TILE, :LANES], o2[Q_TILE:, :LANES])
                o_ref[0, pl.ds(r0, Q_TILE), cs] = (num / jnp.where(low, sums[0], sums[1])).astype(BF16)
                for half in range(2):
                    here = lane == 2 * c + half
                    max_tile = jnp.where(here, maxes[half], max_tile)
                    den_tile = jnp.where(here, sums[half], den_tile)
            lse_ref[0, pl.ds(r0, Q_TILE), rr * LANES:(rr + 1) * LANES] = max_tile + jnp.log2(den_tile)
            return carry

        lax.fori_loop(0, q_block // Q_TILE, tile, 0, unroll=min(WIDE_TILE_UNROLL, q_block // Q_TILE))


def _attn_b_branch(qv, kv, vv, slopes, lane_masks, bsz, window, dilation):
    seq_len = qv.shape[0] // bsz
    radius = (window // 2) // dilation
    q_block = min(Q_BLOCK, seq_len)
    width = min(Q_TILE + 2 * radius, seq_len)
    tiles = WIDE_TILE_UNROLL if width >= 2 * LANES else TILE_UNROLL
    res_block = min(dilation, max(1, tiles * Q_TILE // q_block))
    bias = _band_bias_table(slopes, radius, width, float(dilation))
    view = lambda t: t.reshape(bsz, seq_len, t.shape[-1])
    kv_spec = pl.BlockSpec((1, seq_len, res_block * B_WIDTH), lambda b, r, i: (b, 0, r))
    o, lse = pl.pallas_call(
        functools.partial(_attn_b_kernel, seq_len=seq_len, q_block=q_block, res_block=res_block, radius=radius),
        grid=(bsz, dilation // res_block, seq_len // q_block),
        in_specs=[pl.BlockSpec((2, LANES), lambda b, r, i: (0, 0)),
                  pl.BlockSpec(bias.shape, lambda b, r, i: (0, 0, 0, 0), pipeline_mode=pl.Buffered(1)),
                  pl.BlockSpec((1, q_block, res_block * B_WIDTH), lambda b, r, i: (b, i, r)),
                  kv_spec, kv_spec],
        out_specs=[pl.BlockSpec((1, q_block, res_block * B_WIDTH), lambda b, r, i: (b, i, r)),
                   pl.BlockSpec((1, q_block, res_block * LANES), lambda b, r, i: (b, i, r))],
        out_shape=[jax.ShapeDtypeStruct((bsz, seq_len, dilation * B_WIDTH), BF16),
                   jax.ShapeDtypeStruct((bsz, seq_len, dilation * LANES), F32)],
        compiler_params=_params("parallel", "parallel", "arbitrary"),
        name=f"attn_dilated_{dilation}",
    )(lane_masks, bias, view(qv), view(kv), view(vv))
    return o.reshape(bsz * seq_len, dilation * B_WIDTH), lse.reshape(bsz * seq_len, dilation * LANES)


def _layer_norm(v, g, b):
    mu = jnp.mean(v, axis=-1, keepdims=True)
    c = v - mu
    var = jnp.mean(c * c, axis=-1, keepdims=True)
    return c * lax.rsqrt(var + LN_EPS) * g + b


def _mix_out_kernel(x_ref, oa_ref, *rest):
    n_br = len(B_DILATIONS)
    ob_refs, lse_refs = rest[:n_br], rest[n_br:2 * n_br]
    expand_ref, norm_b_ref, w_ref, g_ref, b_ref, y_ref, ob_stage, lse_stage = rest[2 * n_br:]
    obs, lses = [], []
    for idx, ((_, d), o_ref, l_ref) in enumerate(zip(B_DILATIONS, ob_refs, lse_refs)):
        if d == 1:
            obs.append(o_ref[...].astype(F32))
            lses.append(l_ref[...])
            continue
        rows = OUT_TILE // d
        n_cols = B_WIDTH // LANES
        for res in range(d):
            for c in range(n_cols):
                lo_c = res * B_WIDTH + c * LANES
                ob_stage[idx * n_cols + c, pl.ds(res, rows, stride=d), :] = o_ref[:, lo_c:lo_c + LANES].astype(F32)
            lse_stage[idx, pl.ds(res, rows, stride=d), :] = l_ref[:, res * LANES:(res + 1) * LANES]
        obs.append(jnp.concatenate([ob_stage[idx * n_cols + c] for c in range(n_cols)], axis=-1))
        lses.append(lse_stage[idx])
    top = functools.reduce(jnp.maximum, lses)
    es = [jnp.exp2(l - top) for l in lses]
    inv = 1.0 / functools.reduce(lambda a, b: a + b, es)
    ob = None
    for e, o in zip(es, obs):
        wgt = e * inv
        wide = jnp.dot(jnp.concatenate(_split_bf16(wgt), axis=-1), expand_ref[...], preferred_element_type=F32)
        ob = wide * o if ob is None else ob + wide * o
    ms = jnp.mean(ob * ob, axis=-1, keepdims=True)
    ob = (ob * lax.rsqrt(ms + RMS_EPS) * norm_b_ref[...]).astype(BF16)
    mix = jnp.dot(jnp.concatenate([oa_ref[...], ob], axis=-1), w_ref[...], preferred_element_type=F32)
    y_ref[...] = _layer_norm(DEEPNORM_ALPHA * x_ref[...] + mix, g_ref[...], b_ref[...])


def _mix_out(x, oa, obs, lses, expand, norm_b, w_out, ln_g, ln_b):
    n = x.shape[0]
    row = lambda w: pl.BlockSpec((OUT_TILE, w), lambda i: (i, 0))
    full = lambda r, c: pl.BlockSpec((r, c), lambda i: (0, 0), pipeline_mode=pl.Buffered(1))
    strided = lambda w: [pl.BlockSpec((OUT_TILE // d, d * w), lambda i: (i, 0)) for _, d in B_DILATIONS]
    n_br = len(B_DILATIONS)
    return pl.pallas_call(
        _mix_out_kernel,
        grid=(n // OUT_TILE,),
        in_specs=[row(D_MODEL), row(A_WIDTH), *strided(B_WIDTH), *strided(LANES),
                  full(2 * LANES, B_WIDTH), full(1, B_WIDTH), full(D_MODEL, D_MODEL),
                  full(1, D_MODEL), full(1, D_MODEL)],
        out_specs=row(D_MODEL),
        out_shape=jax.ShapeDtypeStruct((n, D_MODEL), F32),
        scratch_shapes=[pltpu.VMEM((n_br * B_WIDTH // LANES, OUT_TILE, LANES), F32),
                        pltpu.VMEM((n_br, OUT_TILE, LANES), F32)],
        compiler_params=_params("parallel"),
        name="mix_out",
    )(x, oa, *obs, *lses, expand, norm_b, w_out, ln_g, ln_b)


def _silu(v):
    return v / (1.0 + jnp.exp(-v))


def _route(scores, biased):
    def top2_sum(c):
        hi01, lo01 = jnp.maximum(c[0], c[1]), jnp.minimum(c[0], c[1])
        hi23, lo23 = jnp.maximum(c[2], c[3]), jnp.minimum(c[2], c[3])
        first = jnp.maximum(hi01, hi23)
        second = jnp.maximum(jnp.minimum(hi01, hi23), jnp.where(hi01 >= hi23, lo01, lo23))
        return first + second

    groups = [biased[g * EXPERTS_PER_GROUP:(g + 1) * EXPERTS_PER_GROUP] for g in range(N_GROUPS)]
    gscore = [top2_sum(c) for c in groups]
    sel = jnp.zeros_like(gscore[0], dtype=jnp.int32)
    best = gscore[0]
    for g in range(1, N_GROUPS):
        better = gscore[g] > best
        sel = jnp.where(better, g, sel)
        best = jnp.where(better, gscore[g], best)

    def pick(cols):
        out = []
        for j in range(EXPERTS_PER_GROUP):
            v = cols[j]
            for g in range(1, N_GROUPS):
                v = jnp.where(sel == g, cols[g * EXPERTS_PER_GROUP + j], v)
            out.append(v)
        return out

    cb, cs = pick(biased), pick(scores)
    i1 = jnp.zeros_like(sel)
    v1 = cb[0]
    for j in range(1, EXPERTS_PER_GROUP):
        better = cb[j] > v1
        i1 = jnp.where(better, j, i1)
        v1 = jnp.where(better, cb[j], v1)
    i2 = jnp.where(i1 == 0, 1, 0)
    v2 = jnp.where(i1 == 0, cb[1], cb[0])
    for j in range(1, EXPERTS_PER_GROUP):
        better = (cb[j] > v2) & (i1 != j) & (i2 != j)
        i2 = jnp.where(better, j, i2)
        v2 = jnp.where(better, cb[j], v2)
    s1 = sum(jnp.where(i1 == j, cs[j], 0.0) for j in range(EXPERTS_PER_GROUP))
    s2 = sum(jnp.where(i2 == j, cs[j], 0.0) for j in range(EXPERTS_PER_GROUP))
    total = s1 + s2
    gates = []
    for e in range(N_EXPERTS):
        g, j = divmod(e, EXPERTS_PER_GROUP)
        in_group = sel == g
        gate = jnp.where(in_group & (i1 == j), s1 / total, 0.0) + jnp.where(in_group & (i2 == j), s2 / total, 0.0)
        gates.append(gate)
    return gates, sel


def _split_bf16(v):
    hi = v.astype(BF16)
    return hi, (v - hi.astype(F32)).astype(BF16)


def _moe_kernel(x_ref, tri_ref, wr_ref, rb_ref, wgu_ref, wd_ref, sgu_ref, sd_ref,
                g_ref, b_ref, y_ref, yb_ref, xb_ref, xs_ref, gs_ref, ys_ref, pos_ref, off_ref):
    grp = pl.program_id(1)
    tile = MOE_SORT
    n_chunks = tile // SORT_CHUNK
    sections = range(MOE_TILE // MOE_SORT)

    def perm_rows(s, c):
        slot = (lax.broadcasted_iota(jnp.int32, (SORT_CHUNK, tile), 0) + c * SORT_CHUNK).astype(F32)
        return jnp.where(pos_ref[s] == slot, 1.0, 0.0).astype(BF16)

    def route_and_sort(s):
        x = x_ref[s * tile:(s + 1) * tile, :]
        xb = x.astype(BF16)
        xb_ref[s * tile:(s + 1) * tile, :] = xb
        x_lo = (x - xb.astype(F32)).astype(BF16)
        w_hi, w_lo = _split_bf16(wr_ref[...])
        logits = (_dot_nt(jnp.concatenate([w_hi, w_lo], axis=0), xb)
                  + jnp.concatenate([_dot_nt(w_hi, x_lo), jnp.zeros((N_EXPERTS, tile), F32)], axis=0))
        logits = logits[:N_EXPERTS] + logits[N_EXPERTS:]
        scores = 1.0 / (1.0 + jnp.exp(-logits))
        biased = scores + rb_ref[...]
        rows = lambda t: [t[j:j + 1, :] for j in range(N_EXPERTS)]
        gates, sel = _route(rows(scores), rows(biased))
        sub = lax.broadcasted_iota(jnp.int32, (SUBLANES, tile), 0)
        onehot = jnp.where(sub == sel, 1.0, 0.0).astype(BF16)
        rank = jnp.dot(onehot, tri_ref[...], preferred_element_type=F32)
        last_lane = lax.broadcasted_iota(jnp.int32, (1, tile), 1) == tile - 1
        pos = jnp.zeros((1, tile), F32)
        start = jnp.float32(0.0)
        off_ref[s, 0] = 0
        for g in range(N_GROUPS):
            pos = jnp.where(sel == g, start + rank[g:g + 1, :] - 1.0, pos)
            start = start + jnp.sum(jnp.where(last_lane, rank[g:g + 1, :], 0.0))
            off_ref[s, g + 1] = start.astype(jnp.int32)
        pos_ref[s] = pos
        gate_rows = jnp.concatenate(gates + [jnp.zeros((LANES - N_EXPERTS, tile), F32)], axis=0)
        g_split = jnp.concatenate(_split_bf16(gate_rows), axis=0)
        for c in range(n_chunks):
            perm = perm_rows(s, c)
            rows_c = slice(c * SORT_CHUNK, (c + 1) * SORT_CHUNK)
            xs_ref[s, rows_c, :] = jnp.dot(perm, xb, preferred_element_type=F32).astype(BF16)
            g2 = _dot_nt(perm, g_split)
            gs_ref[s, rows_c, :] = g2[:, :LANES] + g2[:, LANES:]
        xs_ref[s, tile:, :] = jnp.zeros((MOE_CHUNK, D_MODEL), BF16)
        gs_ref[s, tile:, :] = jnp.zeros((MOE_CHUNK, LANES), F32)
        ys_ref[s] = jnp.zeros(ys_ref.shape[1:], F32)

    def run_group():
        begins, trips = [], []
        for s in sections:
            first, end = off_ref[s, grp], off_ref[s, grp + 1]
            begin = lax.shift_left(lax.shift_right_logical(first, ROW_ALIGN_LOG2), ROW_ALIGN_LOG2)
            begins.append(begin)
            trips.append(jnp.where(end > first, lax.div(end - begin + (MOE_CHUNK - 1), MOE_CHUNK), 0))
        chunk_lane = lax.broadcasted_iota(jnp.int32, (len(sections) * MOE_CHUNK, LANES), 1)

        def chunk(c, carry):
            starts = [pl.multiple_of(jnp.minimum(b + c * MOE_CHUNK, tile), 1 << ROW_ALIGN_LOG2) for b in begins]
            xc = jnp.concatenate([xs_ref[s, pl.ds(r0, MOE_CHUNK), :] for s, r0 in zip(sections, starts)], axis=0)
            gc = jnp.concatenate([gs_ref[s, pl.ds(r0, MOE_CHUNK), :] for s, r0 in zip(sections, starts)], axis=0)
            gu = jnp.dot(xc, wgu_ref[0], preferred_element_type=F32)
            hidden = []
            for j in range(EXPERTS_PER_GROUP):
                gate = jnp.sum(jnp.where(chunk_lane == grp * EXPERTS_PER_GROUP + j, gc, 0.0), axis=-1, keepdims=True)
                lo_j = j * 2 * D_EXPERT
                h = _silu(gu[:, lo_j:lo_j + D_EXPERT]) * gu[:, lo_j + D_EXPERT:lo_j + 2 * D_EXPERT]
                hidden.append((h * gate).astype(BF16))
            yc = jnp.dot(jnp.concatenate(hidden, axis=-1), wd_ref[0], preferred_element_type=F32)
            for s, r0 in zip(sections, starts):
                ys_ref[s, pl.ds(r0, MOE_CHUNK), :] += yc[s * MOE_CHUNK:(s + 1) * MOE_CHUNK]
            return carry

        lax.fori_loop(0, functools.reduce(jnp.maximum, trips), chunk, 0)

    def unsort_and_norm(s):
        rows_s = slice(s * tile, (s + 1) * tile)
        xb = xb_ref[rows_s, :]
        gu = jnp.dot(xb, sgu_ref[...], preferred_element_type=F32)
        hs = _silu(gu[:, :D_EXPERT]) * gu[:, D_EXPERT:]
        acc = jnp.dot(hs.astype(BF16), sd_ref[...], preferred_element_type=F32)
        for c in range(n_chunks):
            acc = acc + lax.dot_general(perm_rows(s, c),
                                        ys_ref[s, c * SORT_CHUNK:(c + 1) * SORT_CHUNK, :].astype(BF16),
                                        (((0,), (0,)), ((), ())), preferred_element_type=F32)
        y = _layer_norm(DEEPNORM_ALPHA * x_ref[rows_s, :] + acc, g_ref[...], b_ref[...])
        y_ref[rows_s, :] = y
        yb_ref[rows_s, :] = y.astype(BF16)

    @pl.when(grp == 0)
    def _():
        for s in sections:
            route_and_sort(s)

    run_group()

    @pl.when(grp == N_GROUPS - 1)
    def _():
        for s in sections:
            unsort_and_norm(s)


def _moe(x, tri, w_router, router_bias, w_gate_up, w_down, ws_gate_up, ws_down, ln_g, ln_b):
    n = x.shape[0]
    row = pl.BlockSpec((MOE_TILE, D_MODEL), lambda i, g: (i, 0))
    full = lambda r, c: pl.BlockSpec((r, c), lambda i, g: (0, 0), pipeline_mode=pl.Buffered(1))
    group_w = lambda r, c: pl.BlockSpec((1, r, c), lambda i, g: (g, 0, 0))
    n_sec, sec_rows = MOE_TILE // MOE_SORT, MOE_SORT + MOE_CHUNK
    return pl.pallas_call(
        _moe_kernel,
        grid=(n // MOE_TILE, N_GROUPS),
        in_specs=[row, full(MOE_SORT, MOE_SORT), full(N_EXPERTS, D_MODEL), full(N_EXPERTS, 1),
                  group_w(D_MODEL, EXPERTS_PER_GROUP * 2 * D_EXPERT), group_w(EXPERTS_PER_GROUP * D_EXPERT, D_MODEL),
                  full(D_MODEL, 2 * D_EXPERT), full(D_EXPERT, D_MODEL),
                  full(1, D_MODEL), full(1, D_MODEL)],
        out_specs=[row, row],
        out_shape=[jax.ShapeDtypeStruct((n, D_MODEL), F32), jax.ShapeDtypeStruct((n, D_MODEL), BF16)],
        scratch_shapes=[pltpu.VMEM((MOE_TILE, D_MODEL), BF16),
                        pltpu.VMEM((n_sec, sec_rows, D_MODEL), BF16),
                        pltpu.VMEM((n_sec, sec_rows, LANES), F32),
                        pltpu.VMEM((n_sec, sec_rows, D_MODEL), F32),
                        pltpu.VMEM((n_sec, 1, MOE_SORT), F32),
                        pltpu.SMEM((n_sec, N_GROUPS + 1), jnp.int32)],
        compiler_params=pltpu.CompilerParams(dimension_semantics=("parallel", "arbitrary"),
                                             vmem_limit_bytes=MOE_VMEM_LIMIT),
        name="moe",
    )(x, tri, w_router, router_bias, w_gate_up, w_down, ws_gate_up, ws_down, ln_g, ln_b)


def _alibi_slopes():
    n_heads = A_HEADS + B_HEADS
    h = jnp.arange(1, n_heads + 1, dtype=F32)
    return jnp.exp2(-8.0 * h / n_heads)


def _layer(x, xb, p, l, shape):
    bsz, seq_len = shape
    tokens = bsz * seq_len
    qa, ka, va, qb, kb, vb = _inproj(xb, p["w_in"][l])
    seq = lambda t: t.reshape(bsz, seq_len, t.shape[-1])
    oa = _attn_a(seq(qa), seq(ka), seq(va), p["sink2"][l], p["slopes_a"], p["lane_masks"], p["norm_a"][l])
    obs, lses = [], []
    for idx, (window, dilation) in enumerate(B_DILATIONS):
        o, lse = _attn_b_branch(qb[idx], kb[idx], vb[idx], p["slopes_b"], p["lane_masks"], bsz, window, dilation)
        obs.append(o)
        lses.append(lse)
    x = _mix_out(x, oa.reshape(tokens, A_WIDTH), obs, lses, p["expand"], p["norm_b"][l],
                 p["w_out"][l], p["ln1_g"][l], p["ln1_b"][l])
    return _moe(x, p["tri"], p["w_router"], p["router_bias"], p["w_gate_up"][l], p["w_down"][l],
                p["ws_gate_up"][l], p["ws_down"][l], p["ln2_g"][l], p["ln2_b"][l])


def _trunk(x, p):
    bsz, seq_len, _ = x.shape
    h = x.reshape(bsz * seq_len, D_MODEL)
    hb = h
    for l in range(DEPTH):
        h, hb = _layer(h, hb, p, l, (bsz, seq_len))
    return h.reshape(bsz, seq_len, D_MODEL)


def _prepare(w_in, sink_a, norm_a, norm_b, w_out, ln1_g, ln1_b, w_router, router_bias,
             w_gate, w_up, w_down, ws_gate, ws_up, ws_down, ln2_g, ln2_b):
    slopes = _alibi_slopes()
    head_lane = jnp.arange(B_WIDTH)[None, :] // HEAD_DIM
    lane_half = jnp.arange(LANES)[None, :] // HEAD_DIM

    order = jnp.array(A_HEAD_ORDER)
    a_cols = (order[:, None] * HEAD_DIM + jnp.arange(HEAD_DIM)[None, :]).reshape(A_WIDTH)
    w_proj = jnp.concatenate([w_in[..., a_cols], w_in[..., A_WIDTH:]], axis=-1)
    w_mix = jnp.concatenate([w_out[:, a_cols, :], w_out[:, A_WIDTH:, :]], axis=1)
    return {
        "w_in": w_proj.astype(BF16),
        "sink2": sink_a.astype(F32)[:, order] * LOG2E,
        "slopes_a": slopes[0::2][order], "slopes_b": slopes[1::2],
        "lane_masks": (lane_half == jnp.arange(2)[:, None]).astype(BF16),
        "expand": (jnp.arange(2 * LANES)[:, None] % LANES == head_lane).astype(BF16),
        "tri": jnp.tri(MOE_SORT, dtype=BF16).T,
        "norm_a": norm_a[:, a_cols].reshape(DEPTH, 1, A_WIDTH),
        "norm_b": norm_b.reshape(DEPTH, 1, B_WIDTH),
        "w_out": w_mix.astype(BF16),
        "ln1_g": ln1_g.reshape(DEPTH, 1, D_MODEL), "ln1_b": ln1_b.reshape(DEPTH, 1, D_MODEL),
        "w_router": w_router.T, "router_bias": router_bias.reshape(N_EXPERTS, 1),
        "w_gate_up": jnp.concatenate([w_gate.astype(BF16), w_up.astype(BF16)], axis=-1).reshape(
            DEPTH, N_GROUPS, EXPERTS_PER_GROUP, D_MODEL, 2 * D_EXPERT).transpose(0, 1, 3, 2, 4).reshape(
            DEPTH, N_GROUPS, D_MODEL, EXPERTS_PER_GROUP * 2 * D_EXPERT),
        "w_down": w_down.astype(BF16).reshape(DEPTH, N_GROUPS, EXPERTS_PER_GROUP * D_EXPERT, D_MODEL),
        "ws_gate_up": jnp.concatenate([ws_gate.astype(BF16), ws_up.astype(BF16)], axis=-1),
        "ws_down": ws_down.astype(BF16),
        "ln2_g": ln2_g.reshape(DEPTH, 1, D_MODEL), "ln2_b": ln2_b.reshape(DEPTH, 1, D_MODEL),
    }


def kernel(x_prompt, x_sample, w_in, sink_a, norm_a, norm_b, w_out, ln1_g, ln1_b, w_router, router_bias,
           w_gate, w_up, w_down, ws_gate, ws_up, ws_down, ln2_g, ln2_b):
    p = _prepare(w_in, sink_a, norm_a, norm_b, w_out, ln1_g, ln1_b, w_router, router_bias,
                 w_gate, w_up, w_down, ws_gate, ws_up, ws_down, ln2_g, ln2_b)
    return _trunk(x_prompt, p), _trunk(x_sample, p)
```

```python
import functools

import jax
import jax.numpy as jnp
from jax import lax
from jax.experimental import pallas as pl
from jax.experimental.pallas import tpu as pltpu

F32 = jnp.float32
BF16 = jnp.bfloat16

D_MODEL = 1024
DEPTH = 4
HEAD_DIM = 64
A_HEADS = 8
A_KV_HEADS = 2
A_RADIUS = 128
B_HEADS = 8
B_DILATIONS = ((128, 1), (512, 4), (2048, 16))
A_WIDTH = A_HEADS * HEAD_DIM
A_KV_WIDTH = A_KV_HEADS * HEAD_DIM
B_WIDTH = B_HEADS * HEAD_DIM
N_EXPERTS = 16
N_GROUPS = 4
EXPERTS_PER_GROUP = N_EXPERTS // N_GROUPS
D_EXPERT = 256
DEEPNORM_ALPHA = (2 * DEPTH) ** 0.25
LN_EPS = 1e-5
RMS_EPS = 1e-6
LOG2E = 1.4426950408889634

LANES = 128
MASK_PENALTY = 1e30
SUBLANES = 8
V7X_VMEM_BYTES = 64 * 1024 * 1024
VMEM_LIMIT = V7X_VMEM_BYTES * 3 // 4

Q_TILE = 128
Q_BLOCK = 1024
TILE_UNROLL = 4
WIDE_TILE_UNROLL = 8
PROJ_TILE = 1024
PROJ_ROWS = 256
OUT_TILE = 1024
MOE_TILE = 1024
MOE_SORT = 512
MOE_CHUNK = 160
SORT_CHUNK = 256
ROW_ALIGN_LOG2 = 4
MOE_VMEM_LIMIT = V7X_VMEM_BYTES * 7 // 8
STRIDED_DILATIONS = tuple(d for _, d in B_DILATIONS if d > 1)
assert all(b % a == 0 for a, b in zip((1,) + STRIDED_DILATIONS, STRIDED_DILATIONS)), "layouts are built level by level"
A_HEAD_ORDER = tuple(c + (A_HEADS // A_KV_HEADS) * half for c in range(A_HEADS // A_KV_HEADS) for half in range(2))
PROJ_WIDTH = A_WIDTH + 2 * A_KV_WIDTH + 3 * B_WIDTH


def _params(*sem):
    return pltpu.CompilerParams(dimension_semantics=sem, vmem_limit_bytes=VMEM_LIMIT)


def _dot_nt(a, b):
    return lax.dot_general(a, b, (((1,), (1,)), ((), ())), preferred_element_type=F32)


def _inproj_kernel(x_ref, w_ref, qa_ref, ka_ref, va_ref, *b_refs):
    q_scale = HEAD_DIM ** -0.5 * LOG2E

    proj = jnp.dot(x_ref[...].astype(BF16), w_ref[...], preferred_element_type=F32)
    qa_ref[...] = (proj[:, :A_WIDTH] * q_scale).astype(BF16)
    ka_ref[...] = proj[:, A_WIDTH:A_WIDTH + A_KV_WIDTH].astype(BF16)
    va_ref[...] = proj[:, A_WIDTH + A_KV_WIDTH:A_WIDTH + 2 * A_KV_WIDTH].astype(BF16)
    lo = A_WIDTH + 2 * A_KV_WIDTH
    n_layouts = 1 + len(STRIDED_DILATIONS)
    for j in range(3):
        outs = b_refs[j * n_layouts:(j + 1) * n_layouts]
        for rb in range(PROJ_TILE // PROJ_ROWS):
            rows_rb = slice(rb * PROJ_ROWS, (rb + 1) * PROJ_ROWS)
            r = proj[rows_rb, lo + j * B_WIDTH:lo + (j + 1) * B_WIDTH]
            if j == 0:
                r = r * q_scale
            outs[0][rows_rb, :] = r.astype(BF16)
            for c in range(B_WIDTH // LANES):
                pieces = [r[:, c * LANES:(c + 1) * LANES]]
                d = 1
                for d_next, out in zip(STRIDED_DILATIONS, outs[1:]):
                    f = d_next // d
                    split = []
                    for res, piece in enumerate(pieces):
                        parts = pltpu.einshape("mrc->rmc", piece.reshape(piece.shape[0] // f, f, LANES))
                        split.append([parts[q] for q in range(f)])
                    pieces = [split[res][q] for q in range(f) for res in range(d)]
                    d = d_next
                    n_rows = PROJ_ROWS // d
                    for res, piece in enumerate(pieces):
                        lo_c = res * B_WIDTH + c * LANES
                        out[rb * n_rows:(rb + 1) * n_rows, lo_c:lo_c + LANES] = piece.astype(BF16)


def _inproj(x, w_in):
    n = x.shape[0]
    shapes = [(n, A_WIDTH), (n, A_KV_WIDTH), (n, A_KV_WIDTH)]
    blocks = [(PROJ_TILE, A_WIDTH), (PROJ_TILE, A_KV_WIDTH), (PROJ_TILE, A_KV_WIDTH)]
    for _ in range(3):
        for d in (1,) + STRIDED_DILATIONS:
            shapes.append((n // d, d * B_WIDTH))
            blocks.append((PROJ_TILE // d, d * B_WIDTH))
    outs = pl.pallas_call(
        _inproj_kernel,
        grid=(n // PROJ_TILE,),
        in_specs=[pl.BlockSpec((PROJ_TILE, D_MODEL), lambda i: (i, 0)),
                  pl.BlockSpec((D_MODEL, PROJ_WIDTH), lambda i: (0, 0), pipeline_mode=pl.Buffered(1))],
        out_specs=[pl.BlockSpec(b, lambda i: (i, 0)) for b in blocks],
        out_shape=[jax.ShapeDtypeStruct(s, BF16) for s in shapes],
        compiler_params=_params("parallel"),
        name="inproj",
    )(x, w_in)
    n_layouts = 1 + len(STRIDED_DILATIONS)
    qa, ka, va = outs[:3]
    qb, kb, vb = (outs[3 + j * n_layouts:3 + (j + 1) * n_layouts] for j in range(3))
    return qa, ka, va, qb, kb, vb


def _band_bias_table(slopes, radius, width, dist_scale):
    qi = jnp.arange(Q_TILE)[:, None]
    kj = jnp.arange(width)[None, :]
    cases = []
    for offset in (0, -radius, Q_TILE - width):
        dist = jnp.abs(kj - qi + offset)
        pen = jnp.where(dist <= radius, dist.astype(F32) * (dist_scale * LOG2E), MASK_PENALTY)
        cases.append(-slopes[:, None, None] * pen[None])
    return jnp.stack(cases, axis=0)


def _tile_case(q0, seq_len):
    return jnp.where(q0 == 0, 0, jnp.where(q0 == seq_len - Q_TILE, 2, 1))


def _with_ones(v):
    return jnp.concatenate([v, jnp.ones(v.shape, v.dtype)], axis=-1)


def _attn_a_kernel(sink_ref, mask_ref, norm_ref, bias_ref, q_ref, k_ref, v_ref, o_ref, acc_ref, *, seq_len):
    width = Q_TILE + 2 * A_RADIUS
    blk = pl.program_id(1)
    lane = lax.broadcasted_iota(jnp.int32, (Q_TILE, LANES), 1)

    def tile(t, carry):
        r0 = pl.multiple_of(t * Q_TILE, Q_TILE)
        q0 = blk * Q_BLOCK + r0
        ks = pl.multiple_of(jnp.clip(q0 - A_RADIUS, 0, seq_len - width), Q_TILE)
        case = _tile_case(q0, seq_len)
        kg = k_ref[0, pl.ds(ks, width), :]
        vg = v_ref[0, pl.ds(ks, width), :]
        for g in range(A_WIDTH // LANES // 2):
            heads = [(c, half) for c in (2 * g, 2 * g + 1) for half in range(2)]
            q4 = jnp.concatenate([q_ref[0, pl.ds(r0, Q_TILE), c * LANES:(c + 1) * LANES] * mask_ref[half:half + 1, :]
                                  for c, half in heads], axis=0)
            s4 = _dot_nt(q4, kg)
            probs, inv = [], []
            for j, (c, half) in enumerate(heads):
                h = 2 * c + half
                sb = s4[j * Q_TILE:(j + 1) * Q_TILE] + bias_ref[case, h]
                sink = sink_ref[h]
                m = jnp.maximum(jnp.max(sb, axis=-1, keepdims=True), sink)
                p = jnp.exp2(sb - m)
                inv.append(1.0 / (jnp.sum(p, axis=-1, keepdims=True) + jnp.exp2(sink - m)))
                probs.append(p.astype(BF16))
            o4 = jnp.dot(jnp.concatenate(probs, axis=0), vg, preferred_element_type=F32)
            for jc, c in enumerate((2 * g, 2 * g + 1)):
                lo = o4[(2 * jc) * Q_TILE:(2 * jc + 1) * Q_TILE] * inv[2 * jc]
                hi = o4[(2 * jc + 1) * Q_TILE:(2 * jc + 2) * Q_TILE] * inv[2 * jc + 1]
                acc_ref[:, c * LANES:(c + 1) * LANES] = jnp.where(lane < HEAD_DIM, lo, hi)
        o = acc_ref[...]
        ms = jnp.mean(o * o, axis=-1, keepdims=True)
        o_ref[0, pl.ds(r0, Q_TILE), :] = (o * lax.rsqrt(ms + RMS_EPS) * norm_ref[...]).astype(BF16)
        return carry

    lax.fori_loop(0, Q_BLOCK // Q_TILE, tile, 0, unroll=WIDE_TILE_UNROLL)


def _attn_a(qa, ka, va, sinks, slopes, lane_masks, norm_a):
    bsz, seq_len, _ = qa.shape
    width = Q_TILE + 2 * A_RADIUS
    bias = _band_bias_table(slopes, A_RADIUS, width, 1.0)
    kv_spec = pl.BlockSpec((1, seq_len, A_KV_WIDTH), lambda b, i: (b, 0, 0))
    return pl.pallas_call(
        functools.partial(_attn_a_kernel, seq_len=seq_len),
        grid=(bsz, seq_len // Q_BLOCK),
        in_specs=[pl.BlockSpec(memory_space=pltpu.SMEM),
                  pl.BlockSpec((2, LANES), lambda b, i: (0, 0)),
                  pl.BlockSpec((1, A_WIDTH), lambda b, i: (0, 0)),
                  pl.BlockSpec(bias.shape, lambda b, i: (0, 0, 0, 0), pipeline_mode=pl.Buffered(1)),
                  pl.BlockSpec((1, Q_BLOCK, A_WIDTH), lambda b, i: (b, i, 0)),
                  kv_spec, kv_spec],
        out_specs=pl.BlockSpec((1, Q_BLOCK, A_WIDTH), lambda b, i: (b, i, 0)),
        out_shape=jax.ShapeDtypeStruct((bsz, seq_len, A_WIDTH), BF16),
        scratch_shapes=[pltpu.VMEM((Q_TILE, A_WIDTH), F32)],
        compiler_params=_params("parallel", "arbitrary"),
        name="attn_window",
    )(sinks, lane_masks, norm_a, bias, qa, ka, va)


def _attn_b_kernel(mask_ref, bias_ref, q_ref, k_ref, v_ref, o_ref, lse_ref, *,
                   seq_len, q_block, res_block, radius):
    width = min(Q_TILE + 2 * radius, seq_len)
    mxu_sums = width >= 2 * LANES
    blk = pl.program_id(2)
    lane = lax.broadcasted_iota(jnp.int32, (Q_TILE, LANES), 1)

    for rr in range(res_block):
        base = rr * B_WIDTH

        def tile(t, carry, base=base, rr=rr):
            r0 = pl.multiple_of(t * Q_TILE, Q_TILE)
            q0 = blk * q_block + r0
            ks = pl.multiple_of(jnp.clip(q0 - radius, 0, seq_len - width), radius)
            case = _tile_case(q0, seq_len)
            max_tile = jnp.zeros((Q_TILE, LANES), F32)
            den_tile = jnp.ones((Q_TILE, LANES), F32)
            for c in range(B_WIDTH // LANES):
                cs = slice(base + c * LANES, base + (c + 1) * LANES)
                qc = q_ref[0, pl.ds(r0, Q_TILE), cs]
                kc = k_ref[0, pl.ds(ks, width), cs]
                vc = v_ref[0, pl.ds(ks, width), cs]
                if mxu_sums:
                    vc = _with_ones(vc)
                q2 = jnp.concatenate([qc * mask_ref[0:1, :], qc * mask_ref[1:2, :]], axis=0)
                s2 = _dot_nt(q2, kc)
                probs, maxes, sums = [], [], []
                for half in range(2):
                    s = s2[half * Q_TILE:(half + 1) * Q_TILE] + bias_ref[case, 2 * c + half]
                    m = jnp.max(s, axis=-1, keepdims=True)
                    p = jnp.exp2(s - m)
                    if not mxu_sums:
                        sums.append(jnp.sum(p, axis=-1, keepdims=True))
                    probs.append(p.astype(BF16))
                    maxes.append(m)
                o2 = jnp.dot(jnp.concatenate(probs, axis=0), vc, preferred_element_type=F32)
                if mxu_sums:
                    sums = [o2[:Q_TILE, LANES:], o2[Q_TILE:, LANES:]]
                low = lane < HEAD_DIM
                num = jnp.where(low, o2[:Q_TILE, :LANES], o2[Q_TILE:, :LANES])
                o_ref[0, pl.ds(r0, Q_TILE), cs] = (num / jnp.where(low, sums[0], sums[1])).astype(BF16)
                for half in range(2):
                    here = lane == 2 * c + half
                    max_tile = jnp.where(here, maxes[half], max_tile)
                    den_tile = jnp.where(here, sums[half], den_tile)
            lse_ref[0, pl.ds(r0, Q_TILE), rr * LANES:(rr + 1) * LANES] = max_tile + jnp.log2(den_tile)
            return carry

        lax.fori_loop(0, q_block // Q_TILE, tile, 0, unroll=min(WIDE_TILE_UNROLL, q_block // Q_TILE))


def _attn_b_branch(qv, kv, vv, slopes, lane_masks, bsz, window, dilation):
    seq_len = qv.shape[0] // bsz
    radius = (window // 2) // dilation
    q_block = min(Q_BLOCK, seq_len)
    width = min(Q_TILE + 2 * radius, seq_len)
    tiles = WIDE_TILE_UNROLL if width >= 2 * LANES else TILE_UNROLL
    res_block = min(dilation, max(1, tiles * Q_TILE // q_block))
    bias = _band_bias_table(slopes, radius, width, float(dilation))
    view = lambda t: t.reshape(bsz, seq_len, t.shape[-1])
    kv_spec = pl.BlockSpec((1, seq_len, res_block * B_WIDTH), lambda b, r, i: (b, 0, r))
    o, lse = pl.pallas_call(
        functools.partial(_attn_b_kernel, seq_len=seq_len, q_block=q_block, res_block=res_block, radius=radius),
        grid=(bsz, dilation // res_block, seq_len // q_block),
        in_specs=[pl.BlockSpec((2, LANES), lambda b, r, i: (0, 0)),
                  pl.BlockSpec(bias.shape, lambda b, r, i: (0, 0, 0, 0), pipeline_mode=pl.Buffered(1)),
                  pl.BlockSpec((1, q_block, res_block * B_WIDTH), lambda b, r, i: (b, i, r)),
                  kv_spec, kv_spec],
        out_specs=[pl.BlockSpec((1, q_block, res_block * B_WIDTH), lambda b, r, i: (b, i, r)),
                   pl.BlockSpec((1, q_block, res_block * LANES), lambda b, r, i: (b, i, r))],
        out_shape=[jax.ShapeDtypeStruct((bsz, seq_len, dilation * B_WIDTH), BF16),
                   jax.ShapeDtypeStruct((bsz, seq_len, dilation * LANES), F32)],
        compiler_params=_params("parallel", "parallel", "arbitrary"),
        name=f"attn_dilated_{dilation}",
    )(lane_masks, bias, view(qv), view(kv), view(vv))
    return o.reshape(bsz * seq_len, dilation * B_WIDTH), lse.reshape(bsz * seq_len, dilation * LANES)


def _layer_norm(v, g, b):
    mu = jnp.mean(v, axis=-1, keepdims=True)
    c = v - mu
    var = jnp.mean(c * c, axis=-1, keepdims=True)
    return c * lax.rsqrt(var + LN_EPS) * g + b


def _mix_out_kernel(x_ref, oa_ref, *rest):
    n_br = len(B_DILATIONS)
    ob_refs, lse_refs = rest[:n_br], rest[n_br:2 * n_br]
    expand_ref, norm_b_ref, w_ref, g_ref, b_ref, y_ref, ob_stage, lse_stage = rest[2 * n_br:]
    obs, lses = [], []
    for idx, ((_, d), o_ref, l_ref) in enumerate(zip(B_DILATIONS, ob_refs, lse_refs)):
        if d == 1:
            obs.append(o_ref[...].astype(F32))
            lses.append(l_ref[...])
            continue
        rows = OUT_TILE // d
        n_cols = B_WIDTH // LANES
        for res in range(d):
            for c in range(n_cols):
                lo_c = res * B_WIDTH + c * LANES
                ob_stage[idx * n_cols + c, pl.ds(res, rows, stride=d), :] = o_ref[:, lo_c:lo_c + LANES].astype(F32)
            lse_stage[idx, pl.ds(res, rows, stride=d), :] = l_ref[:, res * LANES:(res + 1) * LANES]
        obs.append(jnp.concatenate([ob_stage[idx * n_cols + c] for c in range(n_cols)], axis=-1))
        lses.append(lse_stage[idx])
    top = functools.reduce(jnp.maximum, lses)
    es = [jnp.exp2(l - top) for l in lses]
    inv = 1.0 / functools.reduce(lambda a, b: a + b, es)
    ob = None
    for e, o in zip(es, obs):
        wgt = e * inv
        wide = jnp.dot(jnp.concatenate(_split_bf16(wgt), axis=-1), expand_ref[...], preferred_element_type=F32)
        ob = wide * o if ob is None else ob + wide * o
    ms = jnp.mean(ob * ob, axis=-1, keepdims=True)
    ob = (ob * lax.rsqrt(ms + RMS_EPS) * norm_b_ref[...]).astype(BF16)
    mix = jnp.dot(jnp.concatenate([oa_ref[...], ob], axis=-1), w_ref[...], preferred_element_type=F32)
    y_ref[...] = _layer_norm(DEEPNORM_ALPHA * x_ref[...] + mix, g_ref[...], b_ref[...])


def _mix_out(x, oa, obs, lses, expand, norm_b, w_out, ln_g, ln_b):
    n = x.shape[0]
    row = lambda w: pl.BlockSpec((OUT_TILE, w), lambda i: (i, 0))
    full = lambda r, c: pl.BlockSpec((r, c), lambda i: (0, 0), pipeline_mode=pl.Buffered(1))
    strided = lambda w: [pl.BlockSpec((OUT_TILE // d, d * w), lambda i: (i, 0)) for _, d in B_DILATIONS]
    n_br = len(B_DILATIONS)
    return pl.pallas_call(
        _mix_out_kernel,
        grid=(n // OUT_TILE,),
        in_specs=[row(D_MODEL), row(A_WIDTH), *strided(B_WIDTH), *strided(LANES),
                  full(2 * LANES, B_WIDTH), full(1, B_WIDTH), full(D_MODEL, D_MODEL),
                  full(1, D_MODEL), full(1, D_MODEL)],
        out_specs=row(D_MODEL),
        out_shape=jax.ShapeDtypeStruct((n, D_MODEL), F32),
        scratch_shapes=[pltpu.VMEM((n_br * B_WIDTH // LANES, OUT_TILE, LANES), F32),
                        pltpu.VMEM((n_br, OUT_TILE, LANES), F32)],
        compiler_params=_params("parallel"),
        name="mix_out",
    )(x, oa, *obs, *lses, expand, norm_b, w_out, ln_g, ln_b)


def _silu(v):
    return v / (1.0 + jnp.exp(-v))


def _route(scores, biased):
    def top2_sum(c):
        hi01, lo01 = jnp.maximum(c[0], c[1]), jnp.minimum(c[0], c[1])
        hi23, lo23 = jnp.maximum(c[2], c[3]), jnp.minimum(c[2], c[3])
        first = jnp.maximum(hi01, hi23)
        second = jnp.maximum(jnp.minimum(hi01, hi23), jnp.where(hi01 >= hi23, lo01, lo23))
        return first + second

    groups = [biased[g * EXPERTS_PER_GROUP:(g + 1) * EXPERTS_PER_GROUP] for g in range(N_GROUPS)]
    gscore = [top2_sum(c) for c in groups]
    sel = jnp.zeros_like(gscore[0], dtype=jnp.int32)
    best = gscore[0]
    for g in range(1, N_GROUPS):
        better = gscore[g] > best
        sel = jnp.where(better, g, sel)
        best = jnp.where(better, gscore[g], best)

    def pick(cols):
        out = []
        for j in range(EXPERTS_PER_GROUP):
            v = cols[j]
            for g in range(1, N_GROUPS):
                v = jnp.where(sel == g, cols[g * EXPERTS_PER_GROUP + j], v)
            out.append(v)
        return out

    cb, cs = pick(biased), pick(scores)
    i1 = jnp.zeros_like(sel)
    v1 = cb[0]
    for j in range(1, EXPERTS_PER_GROUP):
        better = cb[j] > v1
        i1 = jnp.where(better, j, i1)
        v1 = jnp.where(better, cb[j], v1)
    i2 = jnp.where(i1 == 0, 1, 0)
    v2 = jnp.where(i1 == 0, cb[1], cb[0])
    for j in range(1, EXPERTS_PER_GROUP):
        better = (cb[j] > v2) & (i1 != j) & (i2 != j)
        i2 = jnp.where(better, j, i2)
        v2 = jnp.where(better, cb[j], v2)
    s1 = sum(jnp.where(i1 == j, cs[j], 0.0) for j in range(EXPERTS_PER_GROUP))
    s2 = sum(jnp.where(i2 == j, cs[j], 0.0) for j in range(EXPERTS_PER_GROUP))
    total = s1 + s2
    gates = []
    for e in range(N_EXPERTS):
        g, j = divmod(e, EXPERTS_PER_GROUP)
        in_group = sel == g
        gate = jnp.where(in_group & (i1 == j), s1 / total, 0.0) + jnp.where(in_group & (i2 == j), s2 / total, 0.0)
        gates.append(gate)
    return gates, sel


def _split_bf16(v):
    hi = v.astype(BF16)
    return hi, (v - hi.astype(F32)).astype(BF16)


def _moe_kernel(x_ref, tri_ref, wr_ref, rb_ref, wgu_ref, wd_ref, sgu_ref, sd_ref,
                g_ref, b_ref, y_ref, yb_ref, xb_ref, xs_ref, gs_ref, ys_ref, pos_ref, off_ref):
    grp = pl.program_id(1)
    tile = MOE_SORT
    n_chunks = tile // SORT_CHUNK
    sections = range(MOE_TILE // MOE_SORT)

    def perm_rows(s, c):
        slot = (lax.broadcasted_iota(jnp.int32, (SORT_CHUNK, tile), 0) + c * SORT_CHUNK).astype(F32)
        return jnp.where(pos_ref[s] == slot, 1.0, 0.0).astype(BF16)

    def route_and_sort(s):
        x = x_ref[s * tile:(s + 1) * tile, :]
        xb = x.astype(BF16)
        xb_ref[s * tile:(s + 1) * tile, :] = xb
        x_lo = (x - xb.astype(F32)).astype(BF16)
        w_hi, w_lo = _split_bf16(wr_ref[...])
        logits = (_dot_nt(jnp.concatenate([w_hi, w_lo], axis=0), xb)
                  + jnp.concatenate([_dot_nt(w_hi, x_lo), jnp.zeros((N_EXPERTS, tile), F32)], axis=0))
        logits = logits[:N_EXPERTS] + logits[N_EXPERTS:]
        scores = 1.0 / (1.0 + jnp.exp(-logits))
        biased = scores + rb_ref[...]
        rows = lambda t: [t[j:j + 1, :] for j in range(N_EXPERTS)]
        gates, sel = _route(rows(scores), rows(biased))
        sub = lax.broadcasted_iota(jnp.int32, (SUBLANES, tile), 0)
        onehot = jnp.where(sub == sel, 1.0, 0.0).astype(BF16)
        rank = jnp.dot(onehot, tri_ref[...], preferred_element_type=F32)
        last_lane = lax.broadcasted_iota(jnp.int32, (1, tile), 1) == tile - 1
        pos = jnp.zeros((1, tile), F32)
        start = jnp.float32(0.0)
        off_ref[s, 0] = 0
        for g in range(N_GROUPS):
            pos = jnp.where(sel == g, start + rank[g:g + 1, :] - 1.0, pos)
            start = start + jnp.sum(jnp.where(last_lane, rank[g:g + 1, :], 0.0))
            off_ref[s, g + 1] = start.astype(jnp.int32)
        pos_ref[s] = pos
        gate_rows = jnp.concatenate(gates + [jnp.zeros((LANES - N_EXPERTS, tile), F32)], axis=0)
        g_split = jnp.concatenate(_split_bf16(gate_rows), axis=0)
        for c in range(n_chunks):
            perm = perm_rows(s, c)
            rows_c = slice(c * SORT_CHUNK, (c + 1) * SORT_CHUNK)
            xs_ref[s, rows_c, :] = jnp.dot(perm, xb, preferred_element_type=F32).astype(BF16)
            g2 = _dot_nt(perm, g_split)
            gs_ref[s, rows_c, :] = g2[:, :LANES] + g2[:, LANES:]
        xs_ref[s, tile:, :] = jnp.zeros((MOE_CHUNK, D_MODEL), BF16)
        gs_ref[s, tile:, :] = jnp.zeros((MOE_CHUNK, LANES), F32)
        ys_ref[s] = jnp.zeros(ys_ref.shape[1:], F32)

    def run_group():
        begins, trips = [], []
        for s in sections:
            first, end = off_ref[s, grp], off_ref[s, grp + 1]
            begin = lax.shift_left(lax.shift_right_logical(first, ROW_ALIGN_LOG2), ROW_ALIGN_LOG2)
            begins.append(begin)
            trips.append(jnp.where(end > first, lax.div(end - begin + (MOE_CHUNK - 1), MOE_CHUNK), 0))
        chunk_lane = lax.broadcasted_iota(jnp.int32, (len(sections) * MOE_CHUNK, LANES), 1)

        def chunk(c, carry):
            starts = [pl.multiple_of(jnp.minimum(b + c * MOE_CHUNK, tile), 1 << ROW_ALIGN_LOG2) for b in begins]
            xc = jnp.concatenate([xs_ref[s, pl.ds(r0, MOE_CHUNK), :] for s, r0 in zip(sections, starts)], axis=0)
            gc = jnp.concatenate([gs_ref[s, pl.ds(r0, MOE_CHUNK), :] for s, r0 in zip(sections, starts)], axis=0)
            hidden = []
            for j in range(EXPERTS_PER_GROUP):
                gate = jnp.sum(jnp.where(chunk_lane == grp * EXPERTS_PER_GROUP + j, gc, 0.0), axis=-1, keepdims=True)
                gu = jnp.dot(xc, wgu_ref[j], preferred_element_type=F32)
                h = _silu(gu[:, :D_EXPERT]) * gu[:, D_EXPERT:]
                hidden.append((h * gate).astype(BF16))
            yc = jnp.dot(jnp.concatenate(hidden, axis=-1), wd_ref[0], preferred_element_type=F32)
            for s, r0 in zip(sections, starts):
                ys_ref[s, pl.ds(r0, MOE_CHUNK), :] += yc[s * MOE_CHUNK:(s + 1) * MOE_CHUNK]
            return carry

        lax.fori_loop(0, functools.reduce(jnp.maximum, trips), chunk, 0)

    def unsort_and_norm(s):
        rows_s = slice(s * tile, (s + 1) * tile)
        xb = xb_ref[rows_s, :]
        gu = jnp.dot(xb, sgu_ref[...], preferred_element_type=F32)
        hs = _silu(gu[:, :D_EXPERT]) * gu[:, D_EXPERT:]
        acc = jnp.dot(hs.astype(BF16), sd_ref[...], preferred_element_type=F32)
        for c in range(n_chunks):
            acc = acc + lax.dot_general(perm_rows(s, c),
                                        ys_ref[s, c * SORT_CHUNK:(c + 1) * SORT_CHUNK, :].astype(BF16),
                                        (((0,), (0,)), ((), ())), preferred_element_type=F32)
        y = _layer_norm(DEEPNORM_ALPHA * x_ref[rows_s, :] + acc, g_ref[...], b_ref[...])
        y_ref[rows_s, :] = y
        yb_ref[rows_s, :] = y.astype(BF16)

    @pl.when(grp == 0)
    def _():
        for s in sections:
            route_and_sort(s)

    run_group()

    @pl.when(grp == N_GROUPS - 1)
    def _():
        for s in sections:
            unsort_and_norm(s)


def _moe(x, tri, w_router, router_bias, w_gate_up, w_down, ws_gate_up, ws_down, ln_g, ln_b):
    n = x.shape[0]
    row = pl.BlockSpec((MOE_TILE, D_MODEL), lambda i, g: (i, 0))
    full = lambda r, c: pl.BlockSpec((r, c), lambda i, g: (0, 0), pipeline_mode=pl.Buffered(1))
    group_w = lambda n_mat, r, c: pl.BlockSpec((n_mat, r, c), lambda i, g: (g, 0, 0))
    n_sec, sec_rows = MOE_TILE // MOE_SORT, MOE_SORT + MOE_CHUNK
    return pl.pallas_call(
        _moe_kernel,
        grid=(n // MOE_TILE, N_GROUPS),
        in_specs=[row, full(MOE_SORT, MOE_SORT), full(N_EXPERTS, D_MODEL), full(N_EXPERTS, 1),
                  group_w(EXPERTS_PER_GROUP, D_MODEL, 2 * D_EXPERT), group_w(1, EXPERTS_PER_GROUP * D_EXPERT, D_MODEL),
                  full(D_MODEL, 2 * D_EXPERT), full(D_EXPERT, D_MODEL),
                  full(1, D_MODEL), full(1, D_MODEL)],
        out_specs=[row, row],
        out_shape=[jax.ShapeDtypeStruct((n, D_MODEL), F32), jax.ShapeDtypeStruct((n, D_MODEL), BF16)],
        scratch_shapes=[pltpu.VMEM((MOE_TILE, D_MODEL), BF16),
                        pltpu.VMEM((n_sec, sec_rows, D_MODEL), BF16),
                        pltpu.VMEM((n_sec, sec_rows, LANES), F32),
                        pltpu.VMEM((n_sec, sec_rows, D_MODEL), F32),
                        pltpu.VMEM((n_sec, 1, MOE_SORT), F32),
                        pltpu.SMEM((n_sec, N_GROUPS + 1), jnp.int32)],
        compiler_params=pltpu.CompilerParams(dimension_semantics=("parallel", "arbitrary"),
                                             vmem_limit_bytes=MOE_VMEM_LIMIT),
        name="moe",
    )(x, tri, w_router, router_bias, w_gate_up, w_down, ws_gate_up, ws_down, ln_g, ln_b)


def _alibi_slopes():
    n_heads = A_HEADS + B_HEADS
    h = jnp.arange(1, n_heads + 1, dtype=F32)
    return jnp.exp2(-8.0 * h / n_heads)


def _layer(x, xb, p, l, shape):
    bsz, seq_len = shape
    tokens = bsz * seq_len
    qa, ka, va, qb, kb, vb = _inproj(xb, p["w_in"][l])
    seq = lambda t: t.reshape(bsz, seq_len, t.shape[-1])
    oa = _attn_a(seq(qa), seq(ka), seq(va), p["sink2"][l], p["slopes_a"], p["lane_masks"], p["norm_a"][l])
    obs, lses = [], []
    for idx, (window, dilation) in enumerate(B_DILATIONS):
        o, lse = _attn_b_branch(qb[idx], kb[idx], vb[idx], p["slopes_b"], p["lane_masks"], bsz, window, dilation)
        obs.append(o)
        lses.append(lse)
    x = _mix_out(x, oa.reshape(tokens, A_WIDTH), obs, lses, p["expand"], p["norm_b"][l],
                 p["w_out"][l], p["ln1_g"][l], p["ln1_b"][l])
    return _moe(x, p["tri"], p["w_router"], p["router_bias"], p["w_gate_up"][l], p["w_down"][l],
                p["ws_gate_up"][l], p["ws_down"][l], p["ln2_g"][l], p["ln2_b"][l])


def _trunk(x, p):
    bsz, seq_len, _ = x.shape
    h = x.reshape(bsz * seq_len, D_MODEL)
    hb = h
    for l in range(DEPTH):
        h, hb = _layer(h, hb, p, l, (bsz, seq_len))
    return h.reshape(bsz, seq_len, D_MODEL)


def _prepare(w_in, sink_a, norm_a, norm_b, w_out, ln1_g, ln1_b, w_router, router_bias,
             w_gate, w_up, w_down, ws_gate, ws_up, ws_down, ln2_g, ln2_b):
    slopes = _alibi_slopes()
    head_lane = jnp.arange(B_WIDTH)[None, :] // HEAD_DIM
    lane_half = jnp.arange(LANES)[None, :] // HEAD_DIM

    order = jnp.array(A_HEAD_ORDER)
    a_cols = (order[:, None] * HEAD_DIM + jnp.arange(HEAD_DIM)[None, :]).reshape(A_WIDTH)
    w_proj = jnp.concatenate([w_in[..., a_cols], w_in[..., A_WIDTH:]], axis=-1)
    w_mix = jnp.concatenate([w_out[:, a_cols, :], w_out[:, A_WIDTH:, :]], axis=1)
    return {
        "w_in": w_proj.astype(BF16),
        "sink2": sink_a.astype(F32)[:, order] * LOG2E,
        "slopes_a": slopes[0::2][order], "slopes_b": slopes[1::2],
        "lane_masks": (lane_half == jnp.arange(2)[:, None]).astype(BF16),
        "expand": (jnp.arange(2 * LANES)[:, None] % LANES == head_lane).astype(BF16),
        "tri": jnp.tri(MOE_SORT, dtype=BF16).T,
        "norm_a": norm_a[:, a_cols].reshape(DEPTH, 1, A_WIDTH),
        "norm_b": norm_b.reshape(DEPTH, 1, B_WIDTH),
        "w_out": w_mix.astype(BF16),
        "ln1_g": ln1_g.reshape(DEPTH, 1, D_MODEL), "ln1_b": ln1_b.reshape(DEPTH, 1, D_MODEL),
        "w_router": w_router.T, "router_bias": router_bias.reshape(N_EXPERTS, 1),
        "w_gate_up": jnp.concatenate([w_gate.astype(BF16), w_up.astype(BF16)], axis=-1),
        "w_down": w_down.astype(BF16).reshape(DEPTH, N_GROUPS, EXPERTS_PER_GROUP * D_EXPERT, D_MODEL),
        "ws_gate_up": jnp.concatenate([ws_gate.astype(BF16), ws_up.astype(BF16)], axis=-1),
        "ws_down": ws_down.astype(BF16),
        "ln2_g": ln2_g.reshape(DEPTH, 1, D_MODEL), "ln2_b": ln2_b.reshape(DEPTH, 1, D_MODEL),
    }


def kernel(x_prompt, x_sample, w_in, sink_a, norm_a, norm_b, w_out, ln1_g, ln1_b, w_router, router_bias,
           w_gate, w_up, w_down, ws_gate, ws_up, ws_down, ln2_g, ln2_b):
    p = _prepare(w_in, sink_a, norm_a, norm_b, w_out, ln1_g, ln1_b, w_router, router_bias,
                 w_gate, w_up, w_down, ws_gate, ws_up, ws_down, ln2_g, ln2_b)
    return _trunk(x_prompt, p), _trunk(x_sample, p)
```

```python
import functools

import jax
import jax.numpy as jnp
from jax import lax
from jax.experimental import pallas as pl
from jax.experimental.pallas import tpu as pltpu

F32 = jnp.float32
BF16 = jnp.bfloat16

D_MODEL = 1024
DEPTH = 4
HEAD_DIM = 64
A_HEADS = 8
A_KV_HEADS = 2
A_RADIUS = 128
B_HEADS = 8
B_DILATIONS = ((128, 1), (512, 4), (2048, 16))
A_WIDTH = A_HEADS * HEAD_DIM
A_KV_WIDTH = A_KV_HEADS * HEAD_DIM
B_WIDTH = B_HEADS * HEAD_DIM
N_EXPERTS = 16
N_GROUPS = 4
EXPERTS_PER_GROUP = N_EXPERTS // N_GROUPS
D_EXPERT = 256
DEEPNORM_ALPHA = (2 * DEPTH) ** 0.25
LN_EPS = 1e-5
RMS_EPS = 1e-6
LOG2E = 1.4426950408889634

LANES = 128
MASK_PENALTY = 1e30
SUBLANES = 8
V7X_VMEM_BYTES = 64 * 1024 * 1024
VMEM_LIMIT = V7X_VMEM_BYTES * 3 // 4

Q_TILE = 128
Q_BLOCK = 1024
TILE_UNROLL = 4
WIDE_TILE_UNROLL = 8
A_COLS_PER_DOT = 4
PROJ_TILE = 1024
PROJ_ROWS = 256
OUT_TILE = 1024
MOE_TILE = 1024
MOE_SORT = 512
MOE_CHUNK = 160
SORT_CHUNK = 256
ROW_ALIGN_LOG2 = 4
MOE_VMEM_LIMIT = V7X_VMEM_BYTES * 7 // 8
STRIDED_DILATIONS = tuple(d for _, d in B_DILATIONS if d > 1)
assert all(b % a == 0 for a, b in zip((1,) + STRIDED_DILATIONS, STRIDED_DILATIONS)), "layouts are built level by level"
A_HEAD_ORDER = tuple(c + (A_HEADS // A_KV_HEADS) * half for c in range(A_HEADS // A_KV_HEADS) for half in range(2))
PROJ_WIDTH = A_WIDTH + 2 * A_KV_WIDTH + 3 * B_WIDTH


def _params(*sem):
    return pltpu.CompilerParams(dimension_semantics=sem, vmem_limit_bytes=VMEM_LIMIT)


def _dot_nt(a, b):
    return lax.dot_general(a, b, (((1,), (1,)), ((), ())), preferred_element_type=F32)


def _inproj_kernel(x_ref, w_ref, qa_ref, ka_ref, va_ref, *b_refs):
    q_scale = HEAD_DIM ** -0.5 * LOG2E

    proj = jnp.dot(x_ref[...].astype(BF16), w_ref[...], preferred_element_type=F32)
    qa_ref[...] = (proj[:, :A_WIDTH] * q_scale).astype(BF16)
    ka_ref[...] = proj[:, A_WIDTH:A_WIDTH + A_KV_WIDTH].astype(BF16)
    va_ref[...] = proj[:, A_WIDTH + A_KV_WIDTH:A_WIDTH + 2 * A_KV_WIDTH].astype(BF16)
    lo = A_WIDTH + 2 * A_KV_WIDTH
    n_layouts = 1 + len(STRIDED_DILATIONS)
    for j in range(3):
        outs = b_refs[j * n_layouts:(j + 1) * n_layouts]
        for rb in range(PROJ_TILE // PROJ_ROWS):
            rows_rb = slice(rb * PROJ_ROWS, (rb + 1) * PROJ_ROWS)
            r = proj[rows_rb, lo + j * B_WIDTH:lo + (j + 1) * B_WIDTH]
            if j == 0:
                r = r * q_scale
            outs[0][rows_rb, :] = r.astype(BF16)
            for c in range(B_WIDTH // LANES):
                pieces = [r[:, c * LANES:(c + 1) * LANES]]
                d = 1
                for d_next, out in zip(STRIDED_DILATIONS, outs[1:]):
                    f = d_next // d
                    split = []
                    for res, piece in enumerate(pieces):
                        parts = pltpu.einshape("mrc->rmc", piece.reshape(piece.shape[0] // f, f, LANES))
                        split.append([parts[q] for q in range(f)])
                    pieces = [split[res][q] for q in range(f) for res in range(d)]
                    d = d_next
                    n_rows = PROJ_ROWS // d
                    for res, piece in enumerate(pieces):
                        lo_c = res * B_WIDTH + c * LANES
                        out[rb * n_rows:(rb + 1) * n_rows, lo_c:lo_c + LANES] = piece.astype(BF16)


def _inproj(x, w_in):
    n = x.shape[0]
    shapes = [(n, A_WIDTH), (n, A_KV_WIDTH), (n, A_KV_WIDTH)]
    blocks = [(PROJ_TILE, A_WIDTH), (PROJ_TILE, A_KV_WIDTH), (PROJ_TILE, A_KV_WIDTH)]
    for _ in range(3):
        for d in (1,) + STRIDED_DILATIONS:
            shapes.append((n // d, d * B_WIDTH))
            blocks.append((PROJ_TILE // d, d * B_WIDTH))
    outs = pl.pallas_call(
        _inproj_kernel,
        grid=(n // PROJ_TILE,),
        in_specs=[pl.BlockSpec((PROJ_TILE, D_MODEL), lambda i: (i, 0)),
                  pl.BlockSpec((D_MODEL, PROJ_WIDTH), lambda i: (0, 0), pipeline_mode=pl.Buffered(1))],
        out_specs=[pl.BlockSpec(b, lambda i: (i, 0)) for b in blocks],
        out_shape=[jax.ShapeDtypeStruct(s, BF16) for s in shapes],
        compiler_params=_params("parallel"),
        name="inproj",
    )(x, w_in)
    n_layouts = 1 + len(STRIDED_DILATIONS)
    qa, ka, va = outs[:3]
    qb, kb, vb = (outs[3 + j * n_layouts:3 + (j + 1) * n_layouts] for j in range(3))
    return qa, ka, va, qb, kb, vb


def _band_bias_table(slopes, radius, width, dist_scale):
    qi = jnp.arange(Q_TILE)[:, None]
    kj = jnp.arange(width)[None, :]
    cases = []
    for offset in (0, -radius, Q_TILE - width):
        dist = jnp.abs(kj - qi + offset)
        pen = jnp.where(dist <= radius, dist.astype(F32) * (dist_scale * LOG2E), MASK_PENALTY)
        cases.append(-slopes[:, None, None] * pen[None])
    return jnp.stack(cases, axis=0)


def _tile_case(q0, seq_len):
    return jnp.where(q0 == 0, 0, jnp.where(q0 == seq_len - Q_TILE, 2, 1))


def _with_ones(v):
    return jnp.concatenate([v, jnp.ones(v.shape, v.dtype)], axis=-1)


def _attn_a_kernel(sink_ref, mask_ref, norm_ref, bias_ref, q_ref, k_ref, v_ref, o_ref, acc_ref, *, seq_len):
    width = Q_TILE + 2 * A_RADIUS
    blk = pl.program_id(1)
    lane = lax.broadcasted_iota(jnp.int32, (Q_TILE, LANES), 1)

    def tile(t, carry):
        r0 = pl.multiple_of(t * Q_TILE, Q_TILE)
        q0 = blk * Q_BLOCK + r0
        ks = pl.multiple_of(jnp.clip(q0 - A_RADIUS, 0, seq_len - width), Q_TILE)
        case = _tile_case(q0, seq_len)
        kg = k_ref[0, pl.ds(ks, width), :]
        vg = v_ref[0, pl.ds(ks, width), :]
        for g in range(A_WIDTH // LANES // A_COLS_PER_DOT):
            cols = range(g * A_COLS_PER_DOT, (g + 1) * A_COLS_PER_DOT)
            heads = [(c, half) for c in cols for half in range(2)]
            q_rows = jnp.concatenate([q_ref[0, pl.ds(r0, Q_TILE), c * LANES:(c + 1) * LANES] * mask_ref[half:half + 1, :]
                                      for c, half in heads], axis=0)
            s_rows = _dot_nt(q_rows, kg)
            probs, inv = [], []
            for j, (c, half) in enumerate(heads):
                h = 2 * c + half
                sb = s_rows[j * Q_TILE:(j + 1) * Q_TILE] + bias_ref[case, h]
                sink = sink_ref[h]
                m = jnp.maximum(jnp.max(sb, axis=-1, keepdims=True), sink)
                p = jnp.exp2(sb - m)
                inv.append(1.0 / (jnp.sum(p, axis=-1, keepdims=True) + jnp.exp2(sink - m)))
                probs.append(p.astype(BF16))
            o_rows = jnp.dot(jnp.concatenate(probs, axis=0), vg, preferred_element_type=F32)
            for jc, c in enumerate(cols):
                lo = o_rows[(2 * jc) * Q_TILE:(2 * jc + 1) * Q_TILE] * inv[2 * jc]
                hi = o_rows[(2 * jc + 1) * Q_TILE:(2 * jc + 2) * Q_TILE] * inv[2 * jc + 1]
                acc_ref[:, c * LANES:(c + 1) * LANES] = jnp.where(lane < HEAD_DIM, lo, hi)
        o = acc_ref[...]
        ms = jnp.mean(o * o, axis=-1, keepdims=True)
        o_ref[0, pl.ds(r0, Q_TILE), :] = (o * lax.rsqrt(ms + RMS_EPS) * norm_ref[...]).astype(BF16)
        return carry

    lax.fori_loop(0, Q_BLOCK // Q_TILE, tile, 0, unroll=WIDE_TILE_UNROLL)


def _attn_a(qa, ka, va, sinks, slopes, lane_masks, norm_a):
    bsz, seq_len, _ = qa.shape
    width = Q_TILE + 2 * A_RADIUS
    bias = _band_bias_table(slopes, A_RADIUS, width, 1.0)
    kv_spec = pl.BlockSpec((1, seq_len, A_KV_WIDTH), lambda b, i: (b, 0, 0))
    return pl.pallas_call(
        functools.partial(_attn_a_kernel, seq_len=seq_len),
        grid=(bsz, seq_len // Q_BLOCK),
        in_specs=[pl.BlockSpec(memory_space=pltpu.SMEM),
                  pl.BlockSpec((2, LANES), lambda b, i: (0, 0)),
                  pl.BlockSpec((1, A_WIDTH), lambda b, i: (0, 0)),
                  pl.BlockSpec(bias.shape, lambda b, i: (0, 0, 0, 0), pipeline_mode=pl.Buffered(1)),
                  pl.BlockSpec((1, Q_BLOCK, A_WIDTH), lambda b, i: (b, i, 0)),
                  kv_spec, kv_spec],
        out_specs=pl.BlockSpec((1, Q_BLOCK, A_WIDTH), lambda b, i: (b, i, 0)),
        out_shape=jax.ShapeDtypeStruct((bsz, seq_len, A_WIDTH), BF16),
        scratch_shapes=[pltpu.VMEM((Q_TILE, A_WIDTH), F32)],
        compiler_params=_params("parallel", "arbitrary"),
        name="attn_window",
    )(sinks, lane_masks, norm_a, bias, qa, ka, va)


def _attn_b_kernel(mask_ref, bias_ref, q_ref, k_ref, v_ref, o_ref, lse_ref, *,
                   seq_len, q_block, res_block, radius):
    width = min(Q_TILE + 2 * radius, seq_len)
    mxu_sums = width >= 2 * LANES
    blk = pl.program_id(2)
    lane = lax.broadcasted_iota(jnp.int32, (Q_TILE, LANES), 1)

    for rr in range(res_block):
        base = rr * B_WIDTH

        def tile(t, carry, base=base, rr=rr):
            r0 = pl.multiple_of(t * Q_TILE, Q_TILE)
            q0 = blk * q_block + r0
            ks = pl.multiple_of(jnp.clip(q0 - radius, 0, seq_len - width), radius)
            case = _tile_case(q0, seq_len)
            max_tile = jnp.zeros((Q_TILE, LANES), F32)
            den_tile = jnp.ones((Q_TILE, LANES), F32)
            for c in range(B_WIDTH // LANES):
                cs = slice(base + c * LANES, base + (c + 1) * LANES)
                qc = q_ref[0, pl.ds(r0, Q_TILE), cs]
                kc = k_ref[0, pl.ds(ks, width), cs]
                vc = v_ref[0, pl.ds(ks, width), cs]
                if mxu_sums:
                    vc = _with_ones(vc)
                q2 = jnp.concatenate([qc * mask_ref[0:1, :], qc * mask_ref[1:2, :]], axis=0)
                s2 = _dot_nt(q2, kc)
                probs, maxes, sums = [], [], []
                for half in range(2):
                    s = s2[half * Q_TILE:(half + 1) * Q_TILE] + bias_ref[case, 2 * c + half]
                    m = jnp.max(s, axis=-1, keepdims=True)
                    p = jnp.exp2(s - m)
                    if not mxu_sums:
                        sums.append(jnp.sum(p, axis=-1, keepdims=True))
                    probs.append(p.astype(BF16))
                    maxes.append(m)
                o2 = jnp.dot(jnp.concatenate(probs, axis=0), vc, preferred_element_type=F32)
                if mxu_sums:
                    sums = [o2[:Q_TILE, LANES:], o2[Q_TILE:, LANES:]]
                low = lane < HEAD_DIM
                num = jnp.where(low, o2[:Q_TILE, :LANES], o2[Q_TILE:, :LANES])
                o_ref[0, pl.ds(r0, Q_TILE), cs] = (num / jnp.where(low, sums[0], sums[1])).astype(BF16)
                for half in range(2):
                    here = lane == 2 * c + half
                    max_tile = jnp.where(here, maxes[half], max_tile)
                    den_tile = jnp.where(here, sums[half], den_tile)
            lse_ref[0, pl.ds(r0, Q_TILE), rr * LANES:(rr + 1) * LANES] = max_tile + jnp.log2(den_tile)
            return carry

        lax.fori_loop(0, q_block // Q_TILE, tile, 0, unroll=min(WIDE_TILE_UNROLL, q_block // Q_TILE))


def _attn_b_branch(qv, kv, vv, slopes, lane_masks, bsz, window, dilation):
    seq_len = qv.shape[0] // bsz
    radius = (window // 2) // dilation
    q_block = min(Q_BLOCK, seq_len)
    width = min(Q_TILE + 2 * radius, seq_len)
    tiles = WIDE_TILE_UNROLL if width >= 2 * LANES else TILE_UNROLL
    res_block = min(dilation, max(1, tiles * Q_TILE // q_block))
    bias = _band_bias_table(slopes, radius, width, float(dilation))
    view = lambda t: t.reshape(bsz, seq_len, t.shape[-1])
    kv_spec = pl.BlockSpec((1, seq_len, res_block * B_WIDTH), lambda b, r, i: (b, 0, r))
    o, lse = pl.pallas_call(
        functools.partial(_attn_b_kernel, seq_len=seq_len, q_block=q_block, res_block=res_block, radius=radius),
        grid=(bsz, dilation // res_block, seq_len // q_block),
        in_specs=[pl.BlockSpec((2, LANES), lambda b, r, i: (0, 0)),
                  pl.BlockSpec(bias.shape, lambda b, r, i: (0, 0, 0, 0), pipeline_mode=pl.Buffered(1)),
                  pl.BlockSpec((1, q_block, res_block * B_WIDTH), lambda b, r, i: (b, i, r)),
                  kv_spec, kv_spec],
        out_specs=[pl.BlockSpec((1, q_block, res_block * B_WIDTH), lambda b, r, i: (b, i, r)),
                   pl.BlockSpec((1, q_block, res_block * LANES), lambda b, r, i: (b, i, r))],
        out_shape=[jax.ShapeDtypeStruct((bsz, seq_len, dilation * B_WIDTH), BF16),
                   jax.ShapeDtypeStruct((bsz, seq_len, dilation * LANES), F32)],
        compiler_params=_params("parallel", "parallel", "arbitrary"),
        name=f"attn_dilated_{dilation}",
    )(lane_masks, bias, view(qv), view(kv), view(vv))
    return o.reshape(bsz * seq_len, dilation * B_WIDTH), lse.reshape(bsz * seq_len, dilation * LANES)


def _layer_norm(v, g, b):
    mu = jnp.mean(v, axis=-1, keepdims=True)
    c = v - mu
    var = jnp.mean(c * c, axis=-1, keepdims=True)
    return c * lax.rsqrt(var + LN_EPS) * g + b


def _mix_out_kernel(x_ref, oa_ref, *rest):
    n_br = len(B_DILATIONS)
    ob_refs, lse_refs = rest[:n_br], rest[n_br:2 * n_br]
    expand_ref, norm_b_ref, w_ref, g_ref, b_ref, y_ref, ob_stage, lse_stage = rest[2 * n_br:]
    obs, lses = [], []
    for idx, ((_, d), o_ref, l_ref) in enumerate(zip(B_DILATIONS, ob_refs, lse_refs)):
        if d == 1:
            obs.append(o_ref[...].astype(F32))
            lses.append(l_ref[...])
            continue
        rows = OUT_TILE // d
        n_cols = B_WIDTH // LANES
        for res in range(d):
            for c in range(n_cols):
                lo_c = res * B_WIDTH + c * LANES
                ob_stage[idx * n_cols + c, pl.ds(res, rows, stride=d), :] = o_ref[:, lo_c:lo_c + LANES].astype(F32)
            lse_stage[idx, pl.ds(res, rows, stride=d), :] = l_ref[:, res * LANES:(res + 1) * LANES]
        obs.append(jnp.concatenate([ob_stage[idx * n_cols + c] for c in range(n_cols)], axis=-1))
        lses.append(lse_stage[idx])
    top = functools.reduce(jnp.maximum, lses)
    es = [jnp.exp2(l - top) for l in lses]
    inv = 1.0 / functools.reduce(lambda a, b: a + b, es)
    ob = None
    for e, o in zip(es, obs):
        wgt = e * inv
        wide = jnp.dot(jnp.concatenate(_split_bf16(wgt), axis=-1), expand_ref[...], preferred_element_type=F32)
        ob = wide * o if ob is None else ob + wide * o
    ms = jnp.mean(ob * ob, axis=-1, keepdims=True)
    ob = (ob * lax.rsqrt(ms + RMS_EPS) * norm_b_ref[...]).astype(BF16)
    mix = jnp.dot(jnp.concatenate([oa_ref[...], ob], axis=-1), w_ref[...], preferred_element_type=F32)
    y_ref[...] = _layer_norm(DEEPNORM_ALPHA * x_ref[...] + mix, g_ref[...], b_ref[...])


def _mix_out(x, oa, obs, lses, expand, norm_b, w_out, ln_g, ln_b):
    n = x.shape[0]
    row = lambda w: pl.BlockSpec((OUT_TILE, w), lambda i: (i, 0))
    full = lambda r, c: pl.BlockSpec((r, c), lambda i: (0, 0), pipeline_mode=pl.Buffered(1))
    strided = lambda w: [pl.BlockSpec((OUT_TILE // d, d * w), lambda i: (i, 0)) for _, d in B_DILATIONS]
    n_br = len(B_DILATIONS)
    return pl.pallas_call(
        _mix_out_kernel,
        grid=(n // OUT_TILE,),
        in_specs=[row(D_MODEL), row(A_WIDTH), *strided(B_WIDTH), *strided(LANES),
                  full(2 * LANES, B_WIDTH), full(1, B_WIDTH), full(D_MODEL, D_MODEL),
                  full(1, D_MODEL), full(1, D_MODEL)],
        out_specs=row(D_MODEL),
        out_shape=jax.ShapeDtypeStruct((n, D_MODEL), F32),
        scratch_shapes=[pltpu.VMEM((n_br * B_WIDTH // LANES, OUT_TILE, LANES), F32),
                        pltpu.VMEM((n_br, OUT_TILE, LANES), F32)],
        compiler_params=_params("parallel"),
        name="mix_out",
    )(x, oa, *obs, *lses, expand, norm_b, w_out, ln_g, ln_b)


def _silu(v):
    return v / (1.0 + jnp.exp(-v))


def _route(scores, biased):
    def top2_sum(c):
        hi01, lo01 = jnp.maximum(c[0], c[1]), jnp.minimum(c[0], c[1])
        hi23, lo23 = jnp.maximum(c[2], c[3]), jnp.minimum(c[2], c[3])
        first = jnp.maximum(hi01, hi23)
        second = jnp.maximum(jnp.minimum(hi01, hi23), jnp.where(hi01 >= hi23, lo01, lo23))
        return first + second

    groups = [biased[g * EXPERTS_PER_GROUP:(g + 1) * EXPERTS_PER_GROUP] for g in range(N_GROUPS)]
    gscore = [top2_sum(c) for c in groups]
    sel = jnp.zeros_like(gscore[0], dtype=jnp.int32)
    best = gscore[0]
    for g in range(1, N_GROUPS):
        better = gscore[g] > best
        sel = jnp.where(better, g, sel)
        best = jnp.where(better, gscore[g], best)

    def pick(cols):
        out = []
        for j in range(EXPERTS_PER_GROUP):
            v = cols[j]
            for g in range(1, N_GROUPS):
                v = jnp.where(sel == g, cols[g * EXPERTS_PER_GROUP + j], v)
            out.append(v)
        return out

    cb, cs = pick(biased), pick(scores)
    i1 = jnp.zeros_like(sel)
    v1 = cb[0]
    for j in range(1, EXPERTS_PER_GROUP):
        better = cb[j] > v1
        i1 = jnp.where(better, j, i1)
        v1 = jnp.where(better, cb[j], v1)
    i2 = jnp.where(i1 == 0, 1, 0)
    v2 = jnp.where(i1 == 0, cb[1], cb[0])
    for j in range(1, EXPERTS_PER_GROUP):
        better = (cb[j] > v2) & (i1 != j) & (i2 != j)
        i2 = jnp.where(better, j, i2)
        v2 = jnp.where(better, cb[j], v2)
    s1 = sum(jnp.where(i1 == j, cs[j], 0.0) for j in range(EXPERTS_PER_GROUP))
    s2 = sum(jnp.where(i2 == j, cs[j], 0.0) for j in range(EXPERTS_PER_GROUP))
    total = s1 + s2
    gates = []
    for e in range(N_EXPERTS):
        g, j = divmod(e, EXPERTS_PER_GROUP)
        in_group = sel == g
        gate = jnp.where(in_group & (i1 == j), s1 / total, 0.0) + jnp.where(in_group & (i2 == j), s2 / total, 0.0)
        gates.append(gate)
    return gates, sel


def _split_bf16(v):
    hi = v.astype(BF16)
    return hi, (v - hi.astype(F32)).astype(BF16)


def _moe_kernel(x_ref, tri_ref, wr_ref, rb_ref, wgu_ref, wd_ref, sgu_ref, sd_ref,
                g_ref, b_ref, y_ref, yb_ref, xb_ref, xs_ref, gs_ref, ys_ref, pos_ref, off_ref):
    grp = pl.program_id(1)
    tile = MOE_SORT
    n_chunks = tile // SORT_CHUNK
    sections = range(MOE_TILE // MOE_SORT)

    def perm_rows(s, c):
        slot = (lax.broadcasted_iota(jnp.int32, (SORT_CHUNK, tile), 0) + c * SORT_CHUNK).astype(F32)
        return jnp.where(pos_ref[s] == slot, 1.0, 0.0).astype(BF16)

    def route_and_sort(s):
        x = x_ref[s * tile:(s + 1) * tile, :]
        xb = x.astype(BF16)
        xb_ref[s * tile:(s + 1) * tile, :] = xb
        x_lo = (x - xb.astype(F32)).astype(BF16)
        w_hi, w_lo = _split_bf16(wr_ref[...])
        logits = (_dot_nt(jnp.concatenate([w_hi, w_lo], axis=0), xb)
                  + jnp.concatenate([_dot_nt(w_hi, x_lo), jnp.zeros((N_EXPERTS, tile), F32)], axis=0))
        logits = logits[:N_EXPERTS] + logits[N_EXPERTS:]
        scores = 1.0 / (1.0 + jnp.exp(-logits))
        biased = scores + rb_ref[...]
        rows = lambda t: [t[j:j + 1, :] for j in range(N_EXPERTS)]
        gates, sel = _route(rows(scores), rows(biased))
        sub = lax.broadcasted_iota(jnp.int32, (SUBLANES, tile), 0)
        onehot = jnp.where(sub == sel, 1.0, 0.0).astype(BF16)
        rank = jnp.dot(onehot, tri_ref[...], preferred_element_type=F32)
        last_lane = lax.broadcasted_iota(jnp.int32, (1, tile), 1) == tile - 1
        pos = jnp.zeros((1, tile), F32)
        start = jnp.float32(0.0)
        off_ref[s, 0] = 0
        for g in range(N_GROUPS):
            pos = jnp.where(sel == g, start + rank[g:g + 1, :] - 1.0, pos)
            start = start + jnp.sum(jnp.where(last_lane, rank[g:g + 1, :], 0.0))
            off_ref[s, g + 1] = start.astype(jnp.int32)
        pos_ref[s] = pos
        gate_rows = jnp.concatenate(gates + [jnp.zeros((LANES - N_EXPERTS, tile), F32)], axis=0)
        g_split = jnp.concatenate(_split_bf16(gate_rows), axis=0)
        for c in range(n_chunks):
            perm = perm_rows(s, c)
            rows_c = slice(c * SORT_CHUNK, (c + 1) * SORT_CHUNK)
            xs_ref[s, rows_c, :] = jnp.dot(perm, xb, preferred_element_type=F32).astype(BF16)
            g2 = _dot_nt(perm, g_split)
            gs_ref[s, rows_c, :] = g2[:, :LANES] + g2[:, LANES:]
        xs_ref[s, tile:, :] = jnp.zeros((MOE_CHUNK, D_MODEL), BF16)
        gs_ref[s, tile:, :] = jnp.zeros((MOE_CHUNK, LANES), F32)
        ys_ref[s] = jnp.zeros(ys_ref.shape[1:], F32)

    def run_group():
        begins, trips = [], []
        for s in sections:
            first, end = off_ref[s, grp], off_ref[s, grp + 1]
            begin = lax.shift_left(lax.shift_right_logical(first, ROW_ALIGN_LOG2), ROW_ALIGN_LOG2)
            begins.append(begin)
            trips.append(jnp.where(end > first, lax.div(end - begin + (MOE_CHUNK - 1), MOE_CHUNK), 0))
        chunk_lane = lax.broadcasted_iota(jnp.int32, (len(sections) * MOE_CHUNK, LANES), 1)

        def chunk(c, carry):
            starts = [pl.multiple_of(jnp.minimum(b + c * MOE_CHUNK, tile), 1 << ROW_ALIGN_LOG2) for b in begins]
            xc = jnp.concatenate([xs_ref[s, pl.ds(r0, MOE_CHUNK), :] for s, r0 in zip(sections, starts)], axis=0)
            gc = jnp.concatenate([gs_ref[s, pl.ds(r0, MOE_CHUNK), :] for s, r0 in zip(sections, starts)], axis=0)
            hidden = []
            for j in range(EXPERTS_PER_GROUP):
                gate = jnp.sum(jnp.where(chunk_lane == grp * EXPERTS_PER_GROUP + j, gc, 0.0), axis=-1, keepdims=True)
                gu = jnp.dot(xc, wgu_ref[j], preferred_element_type=F32)
                h = _silu(gu[:, :D_EXPERT]) * gu[:, D_EXPERT:]
                hidden.append((h * gate).astype(BF16))
            yc = jnp.dot(jnp.concatenate(hidden, axis=-1), wd_ref[0], preferred_element_type=F32)
            for s, r0 in zip(sections, starts):
                ys_ref[s, pl.ds(r0, MOE_CHUNK), :] += yc[s * MOE_CHUNK:(s + 1) * MOE_CHUNK]
            return carry

        lax.fori_loop(0, functools.reduce(jnp.maximum, trips), chunk, 0)

    def unsort_and_norm(s):
        rows_s = slice(s * tile, (s + 1) * tile)
        xb = xb_ref[rows_s, :]
        gu = jnp.dot(xb, sgu_ref[...], preferred_element_type=F32)
        hs = _silu(gu[:, :D_EXPERT]) * gu[:, D_EXPERT:]
        acc = jnp.dot(hs.astype(BF16), sd_ref[...], preferred_element_type=F32)
        for c in range(n_chunks):
            acc = acc + lax.dot_general(perm_rows(s, c),
                                        ys_ref[s, c * SORT_CHUNK:(c + 1) * SORT_CHUNK, :].astype(BF16),
                                        (((0,), (0,)), ((), ())), preferred_element_type=F32)
        y = _layer_norm(DEEPNORM_ALPHA * x_ref[rows_s, :] + acc, g_ref[...], b_ref[...])
        y_ref[rows_s, :] = y
        yb_ref[rows_s, :] = y.astype(BF16)

    @pl.when(grp == 0)
    def _():
        for s in sections:
            route_and_sort(s)

    run_group()

    @pl.when(grp == N_GROUPS - 1)
    def _():
        for s in sections:
            unsort_and_norm(s)


def _moe(x, tri, w_router, router_bias, w_gate_up, w_down, ws_gate_up, ws_down, ln_g, ln_b):
    n = x.shape[0]
    row = pl.BlockSpec((MOE_TILE, D_MODEL), lambda i, g: (i, 0))
    full = lambda r, c: pl.BlockSpec((r, c), lambda i, g: (0, 0), pipeline_mode=pl.Buffered(1))
    group_w = lambda n_mat, r, c: pl.BlockSpec((n_mat, r, c), lambda i, g: (g, 0, 0))
    n_sec, sec_rows = MOE_TILE // MOE_SORT, MOE_SORT + MOE_CHUNK
    return pl.pallas_call(
        _moe_kernel,
        grid=(n // MOE_TILE, N_GROUPS),
        in_specs=[row, full(MOE_SORT, MOE_SORT), full(N_EXPERTS, D_MODEL), full(N_EXPERTS, 1),
                  group_w(EXPERTS_PER_GROUP, D_MODEL, 2 * D_EXPERT), group_w(1, EXPERTS_PER_GROUP * D_EXPERT, D_MODEL),
                  full(D_MODEL, 2 * D_EXPERT), full(D_EXPERT, D_MODEL),
                  full(1, D_MODEL), full(1, D_MODEL)],
        out_specs=[row, row],
        out_shape=[jax.ShapeDtypeStruct((n, D_MODEL), F32), jax.ShapeDtypeStruct((n, D_MODEL), BF16)],
        scratch_shapes=[pltpu.VMEM((MOE_TILE, D_MODEL), BF16),
                        pltpu.VMEM((n_sec, sec_rows, D_MODEL), BF16),
                        pltpu.VMEM((n_sec, sec_rows, LANES), F32),
                        pltpu.VMEM((n_sec, sec_rows, D_MODEL), F32),
                        pltpu.VMEM((n_sec, 1, MOE_SORT), F32),
                        pltpu.SMEM((n_sec, N_GROUPS + 1), jnp.int32)],
        compiler_params=pltpu.CompilerParams(dimension_semantics=("parallel", "arbitrary"),
                                             vmem_limit_bytes=MOE_VMEM_LIMIT),
        name="moe",
    )(x, tri, w_router, router_bias, w_gate_up, w_down, ws_gate_up, ws_down, ln_g, ln_b)


def _alibi_slopes():
    n_heads = A_HEADS + B_HEADS
    h = jnp.arange(1, n_heads + 1, dtype=F32)
    return jnp.exp2(-8.0 * h / n_heads)


def _layer(x, xb, p, l, shape):
    bsz, seq_len = shape
    tokens = bsz * seq_len
    qa, ka, va, qb, kb, vb = _inproj(xb, p["w_in"][l])
    seq = lambda t: t.reshape(bsz, seq_len, t.shape[-1])
    oa = _attn_a(seq(qa), seq(ka), seq(va), p["sink2"][l], p["slopes_a"], p["lane_masks"], p["norm_a"][l])
    obs, lses = [], []
    for idx, (window, dilation) in enumerate(B_DILATIONS):
        o, lse = _attn_b_branch(qb[idx], kb[idx], vb[idx], p["slopes_b"], p["lane_masks"], bsz, window, dilation)
        obs.append(o)
        lses.append(lse)
    x = _mix_out(x, oa.reshape(tokens, A_WIDTH), obs, lses, p["expand"], p["norm_b"][l],
                 p["w_out"][l], p["ln1_g"][l], p["ln1_b"][l])
    return _moe(x, p["tri"], p["w_router"], p["router_bias"], p["w_gate_up"][l], p["w_down"][l],
                p["ws_gate_up"][l], p["ws_down"][l], p["ln2_g"][l], p["ln2_b"][l])


def _trunk(x, p):
    bsz, seq_len, _ = x.shape
    h = x.reshape(bsz * seq_len, D_MODEL)
    hb = h
    for l in range(DEPTH):
        h, hb = _layer(h, hb, p, l, (bsz, seq_len))
    return h.reshape(bsz, seq_len, D_MODEL)


def _prepare(w_in, sink_a, norm_a, norm_b, w_out, ln1_g, ln1_b, w_router, router_bias,
             w_gate, w_up, w_down, ws_gate, ws_up, ws_down, ln2_g, ln2_b):
    slopes = _alibi_slopes()
    head_lane = jnp.arange(B_WIDTH)[None, :] // HEAD_DIM
    lane_half = jnp.arange(LANES)[None, :] // HEAD_DIM

    order = jnp.array(A_HEAD_ORDER)
    a_cols = (order[:, None] * HEAD_DIM + jnp.arange(HEAD_DIM)[None, :]).reshape(A_WIDTH)
    w_proj = jnp.concatenate([w_in[..., a_cols], w_in[..., A_WIDTH:]], axis=-1)
    w_mix = jnp.concatenate([w_out[:, a_cols, :], w_out[:, A_WIDTH:, :]], axis=1)
    return {
        "w_in": w_proj.astype(BF16),
        "sink2": sink_a.astype(F32)[:, order] * LOG2E,
        "slopes_a": slopes[0::2][order], "slopes_b": slopes[1::2],
        "lane_masks": (lane_half == jnp.arange(2)[:, None]).astype(BF16),
        "expand": (jnp.arange(2 * LANES)[:, None] % LANES == head_lane).astype(BF16),
        "tri": jnp.tri(MOE_SORT, dtype=BF16).T,
        "norm_a": norm_a[:, a_cols].reshape(DEPTH, 1, A_WIDTH),
        "norm_b": norm_b.reshape(DEPTH, 1, B_WIDTH),
        "w_out": w_mix.astype(BF16),
        "ln1_g": ln1_g.reshape(DEPTH, 1, D_MODEL), "ln1_b": ln1_b.reshape(DEPTH, 1, D_MODEL),
        "w_router": w_router.T, "router_bias": router_bias.reshape(N_EXPERTS, 1),
        "w_gate_up": jnp.concatenate([w_gate.astype(BF16), w_up.astype(BF16)], axis=-1),
        "w_down": w_down.astype(BF16).reshape(DEPTH, N_GROUPS, EXPERTS_PER_GROUP * D_EXPERT, D_MODEL),
        "ws_gate_up": jnp.concatenate([ws_gate.astype(BF16), ws_up.astype(BF16)], axis=-1),
        "ws_down": ws_down.astype(BF16),
        "ln2_g": ln2_g.reshape(DEPTH, 1, D_MODEL), "ln2_b": ln2_b.reshape(DEPTH, 1, D_MODEL),
    }


def kernel(x_prompt, x_sample, w_in, sink_a, norm_a, norm_b, w_out, ln1_g, ln1_b, w_router, router_bias,
           w_gate, w_up, w_down, ws_gate, ws_up, ws_down, ln2_g, ln2_b):
    p = _prepare(w_in, sink_a, norm_a, norm_b, w_out, ln1_g, ln1_b, w_router, router_bias,
                 w_gate, w_up, w_down, ws_gate, ws_up, ws_down, ln2_g, ln2_b)
    return _trunk(x_prompt, p), _trunk(x_sample, p)
```

```python
import functools

import jax
import jax.numpy as jnp
from jax import lax
from jax.experimental import pallas as pl
from jax.experimental.pallas import tpu as pltpu

F32 = jnp.float32
BF16 = jnp.bfloat16

D_MODEL = 1024
DEPTH = 4
HEAD_DIM = 64
A_HEADS = 8
A_KV_HEADS = 2
A_RADIUS = 128
B_HEADS = 8
B_DILATIONS = ((128, 1), (512, 4), (2048, 16))
A_WIDTH = A_HEADS * HEAD_DIM
A_KV_WIDTH = A_KV_HEADS * HEAD_DIM
B_WIDTH = B_HEADS * HEAD_DIM
N_EXPERTS = 16
N_GROUPS = 4
EXPERTS_PER_GROUP = N_EXPERTS // N_GROUPS
D_EXPERT = 256
DEEPNORM_ALPHA = (2 * DEPTH) ** 0.25
LN_EPS = 1e-5
RMS_EPS = 1e-6
LOG2E = 1.4426950408889634

LANES = 128
MASK_PENALTY = 1e30
SUBLANES = 8
V7X_VMEM_BYTES = 64 * 1024 * 1024
VMEM_LIMIT = V7X_VMEM_BYTES * 3 // 4

Q_TILE = 128
Q_BLOCK = 2048
TILE_UNROLL = 4
WIDE_TILE_UNROLL = 16
A_COLS_PER_DOT = 4
PROJ_TILE = 1024
PROJ_ROWS = 256
OUT_TILE = 1024
MOE_TILE = 1024
MOE_SORT = 512
MOE_CHUNK = 160
SORT_CHUNK = 256
ROW_ALIGN_LOG2 = 4
MOE_VMEM_LIMIT = V7X_VMEM_BYTES * 7 // 8
STRIDED_DILATIONS = tuple(d for _, d in B_DILATIONS if d > 1)
assert all(b % a == 0 for a, b in zip((1,) + STRIDED_DILATIONS, STRIDED_DILATIONS)), "layouts are built level by level"
A_HEAD_ORDER = tuple(c + (A_HEADS // A_KV_HEADS) * half for c in range(A_HEADS // A_KV_HEADS) for half in range(2))
PROJ_WIDTH = A_WIDTH + 2 * A_KV_WIDTH + 3 * B_WIDTH


def _params(*sem):
    return pltpu.CompilerParams(dimension_semantics=sem, vmem_limit_bytes=VMEM_LIMIT)


def _dot_nt(a, b):
    return lax.dot_general(a, b, (((1,), (1,)), ((), ())), preferred_element_type=F32)


def _inproj_kernel(x_ref, w_ref, qa_ref, ka_ref, va_ref, *b_refs):
    q_scale = HEAD_DIM ** -0.5 * LOG2E

    proj = jnp.dot(x_ref[...].astype(BF16), w_ref[...], preferred_element_type=F32)
    qa_ref[...] = (proj[:, :A_WIDTH] * q_scale).astype(BF16)
    ka_ref[...] = proj[:, A_WIDTH:A_WIDTH + A_KV_WIDTH].astype(BF16)
    va_ref[...] = proj[:, A_WIDTH + A_KV_WIDTH:A_WIDTH + 2 * A_KV_WIDTH].astype(BF16)
    lo = A_WIDTH + 2 * A_KV_WIDTH
    n_layouts = 1 + len(STRIDED_DILATIONS)
    for j in range(3):
        outs = b_refs[j * n_layouts:(j + 1) * n_layouts]
        for rb in range(PROJ_TILE // PROJ_ROWS):
            rows_rb = slice(rb * PROJ_ROWS, (rb + 1) * PROJ_ROWS)
            r = proj[rows_rb, lo + j * B_WIDTH:lo + (j + 1) * B_WIDTH]
            if j == 0:
                r = r * q_scale
            outs[0][rows_rb, :] = r.astype(BF16)
            for c in range(B_WIDTH // LANES):
                pieces = [r[:, c * LANES:(c + 1) * LANES]]
                d = 1
                for d_next, out in zip(STRIDED_DILATIONS, outs[1:]):
                    f = d_next // d
                    split = []
                    for res, piece in enumerate(pieces):
                        parts = pltpu.einshape("mrc->rmc", piece.reshape(piece.shape[0] // f, f, LANES))
                        split.append([parts[q] for q in range(f)])
                    pieces = [split[res][q] for q in range(f) for res in range(d)]
                    d = d_next
                    n_rows = PROJ_ROWS // d
                    for res, piece in enumerate(pieces):
                        lo_c = res * B_WIDTH + c * LANES
                        out[rb * n_rows:(rb + 1) * n_rows, lo_c:lo_c + LANES] = piece.astype(BF16)


def _inproj(x, w_in):
    n = x.shape[0]
    shapes = [(n, A_WIDTH), (n, A_KV_WIDTH), (n, A_KV_WIDTH)]
    blocks = [(PROJ_TILE, A_WIDTH), (PROJ_TILE, A_KV_WIDTH), (PROJ_TILE, A_KV_WIDTH)]
    for _ in range(3):
        for d in (1,) + STRIDED_DILATIONS:
            shapes.append((n // d, d * B_WIDTH))
            blocks.append((PROJ_TILE // d, d * B_WIDTH))
    outs = pl.pallas_call(
        _inproj_kernel,
        grid=(n // PROJ_TILE,),
        in_specs=[pl.BlockSpec((PROJ_TILE, D_MODEL), lambda i: (i, 0)),
                  pl.BlockSpec((D_MODEL, PROJ_WIDTH), lambda i: (0, 0), pipeline_mode=pl.Buffered(1))],
        out_specs=[pl.BlockSpec(b, lambda i: (i, 0)) for b in blocks],
        out_shape=[jax.ShapeDtypeStruct(s, BF16) for s in shapes],
        compiler_params=_params("parallel"),
        name="inproj",
    )(x, w_in)
    n_layouts = 1 + len(STRIDED_DILATIONS)
    qa, ka, va = outs[:3]
    qb, kb, vb = (outs[3 + j * n_layouts:3 + (j + 1) * n_layouts] for j in range(3))
    return qa, ka, va, qb, kb, vb


def _band_bias_table(slopes, radius, width, dist_scale):
    qi = jnp.arange(Q_TILE)[:, None]
    kj = jnp.arange(width)[None, :]
    cases = []
    for offset in (0, -radius, Q_TILE - width):
        dist = jnp.abs(kj - qi + offset)
        pen = jnp.where(dist <= radius, dist.astype(F32) * (dist_scale * LOG2E), MASK_PENALTY)
        cases.append(-slopes[:, None, None] * pen[None])
    return jnp.stack(cases, axis=0)


def _tile_case(q0, seq_len):
    return jnp.where(q0 == 0, 0, jnp.where(q0 == seq_len - Q_TILE, 2, 1))


def _with_ones(v):
    return jnp.concatenate([v, jnp.ones(v.shape, v.dtype)], axis=-1)


def _attn_a_kernel(sink_ref, mask_ref, norm_ref, bias_ref, q_ref, k_ref, v_ref, o_ref, acc_ref, *, seq_len):
    width = Q_TILE + 2 * A_RADIUS
    blk = pl.program_id(1)
    lane = lax.broadcasted_iota(jnp.int32, (Q_TILE, LANES), 1)

    def tile(t, carry):
        r0 = pl.multiple_of(t * Q_TILE, Q_TILE)
        q0 = blk * Q_BLOCK + r0
        ks = pl.multiple_of(jnp.clip(q0 - A_RADIUS, 0, seq_len - width), Q_TILE)
        case = _tile_case(q0, seq_len)
        kg = k_ref[0, pl.ds(ks, width), :]
        vg = v_ref[0, pl.ds(ks, width), :]
        for g in range(A_WIDTH // LANES // A_COLS_PER_DOT):
            cols = range(g * A_COLS_PER_DOT, (g + 1) * A_COLS_PER_DOT)
            heads = [(c, half) for c in cols for half in range(2)]
            q_rows = jnp.concatenate([q_ref[0, pl.ds(r0, Q_TILE), c * LANES:(c + 1) * LANES] * mask_ref[half:half + 1, :]
                                      for c, half in heads], axis=0)
            s_rows = _dot_nt(q_rows, kg)
            probs, inv = [], []
            for j, (c, half) in enumerate(heads):
                h = 2 * c + half
                sb = s_rows[j * Q_TILE:(j + 1) * Q_TILE] + bias_ref[case, h]
                sink = sink_ref[h]
                m = jnp.maximum(jnp.max(sb, axis=-1, keepdims=True), sink)
                p = jnp.exp2(sb - m)
                inv.append(1.0 / (jnp.sum(p, axis=-1, keepdims=True) + jnp.exp2(sink - m)))
                probs.append(p.astype(BF16))
            o_rows = jnp.dot(jnp.concatenate(probs, axis=0), vg, preferred_element_type=F32)
            for jc, c in enumerate(cols):
                lo = o_rows[(2 * jc) * Q_TILE:(2 * jc + 1) * Q_TILE] * inv[2 * jc]
                hi = o_rows[(2 * jc + 1) * Q_TILE:(2 * jc + 2) * Q_TILE] * inv[2 * jc + 1]
                acc_ref[:, c * LANES:(c + 1) * LANES] = jnp.where(lane < HEAD_DIM, lo, hi)
        o = acc_ref[...]
        ms = jnp.mean(o * o, axis=-1, keepdims=True)
        o_ref[0, pl.ds(r0, Q_TILE), :] = (o * lax.rsqrt(ms + RMS_EPS) * norm_ref[...]).astype(BF16)
        return carry

    lax.fori_loop(0, Q_BLOCK // Q_TILE, tile, 0, unroll=WIDE_TILE_UNROLL)


def _attn_a(qa, ka, va, sinks, slopes, lane_masks, norm_a):
    bsz, seq_len, _ = qa.shape
    width = Q_TILE + 2 * A_RADIUS
    bias = _band_bias_table(slopes, A_RADIUS, width, 1.0)
    kv_spec = pl.BlockSpec((1, seq_len, A_KV_WIDTH), lambda b, i: (b, 0, 0))
    return pl.pallas_call(
        functools.partial(_attn_a_kernel, seq_len=seq_len),
        grid=(bsz, seq_len // Q_BLOCK),
        in_specs=[pl.BlockSpec(memory_space=pltpu.SMEM),
                  pl.BlockSpec((2, LANES), lambda b, i: (0, 0)),
                  pl.BlockSpec((1, A_WIDTH), lambda b, i: (0, 0)),
                  pl.BlockSpec(bias.shape, lambda b, i: (0, 0, 0, 0), pipeline_mode=pl.Buffered(1)),
                  pl.BlockSpec((1, Q_BLOCK, A_WIDTH), lambda b, i: (b, i, 0)),
                  kv_spec, kv_spec],
        out_specs=pl.BlockSpec((1, Q_BLOCK, A_WIDTH), lambda b, i: (b, i, 0)),
        out_shape=jax.ShapeDtypeStruct((bsz, seq_len, A_WIDTH), BF16),
        scratch_shapes=[pltpu.VMEM((Q_TILE, A_WIDTH), F32)],
        compiler_params=_params("parallel", "arbitrary"),
        name="attn_window",
    )(sinks, lane_masks, norm_a, bias, qa, ka, va)


def _attn_b_kernel(mask_ref, bias_ref, q_ref, k_ref, v_ref, o_ref, lse_ref, *,
                   seq_len, q_block, res_block, radius):
    width = min(Q_TILE + 2 * radius, seq_len)
    mxu_sums = width >= 2 * LANES
    blk = pl.program_id(2)
    lane = lax.broadcasted_iota(jnp.int32, (Q_TILE, LANES), 1)

    for rr in range(res_block):
        base = rr * B_WIDTH

        def tile(t, carry, base=base, rr=rr):
            r0 = pl.multiple_of(t * Q_TILE, Q_TILE)
            q0 = blk * q_block + r0
            ks = pl.multiple_of(jnp.clip(q0 - radius, 0, seq_len - width), radius)
            case = _tile_case(q0, seq_len)
            max_tile = jnp.zeros((Q_TILE, LANES), F32)
            den_tile = jnp.ones((Q_TILE, LANES), F32)
            for c in range(B_WIDTH // LANES):
                cs = slice(base + c * LANES, base + (c + 1) * LANES)
                qc = q_ref[0, pl.ds(r0, Q_TILE), cs]
                kc = k_ref[0, pl.ds(ks, width), cs]
                vc = v_ref[0, pl.ds(ks, width), cs]
                if mxu_sums:
                    vc = _with_ones(vc)
                q2 = jnp.concatenate([qc * mask_ref[0:1, :], qc * mask_ref[1:2, :]], axis=0)
                s2 = _dot_nt(q2, kc)
                probs, maxes, sums = [], [], []
                for half in range(2):
                    s = s2[half * Q_TILE:(half + 1) * Q_TILE] + bias_ref[case, 2 * c + half]
                    m = jnp.max(s, axis=-1, keepdims=True)
                    p = jnp.exp2(s - m)
                    if not mxu_sums:
                        sums.append(jnp.sum(p, axis=-1, keepdims=True))
                    probs.append(p.astype(BF16))
                    maxes.append(m)
                o2 = jnp.dot(jnp.concatenate(probs, axis=0), vc, preferred_element_type=F32)
                if mxu_sums:
                    sums = [o2[:Q_TILE, LANES:], o2[Q_TILE:, LANES:]]
                low = lane < HEAD_DIM
                num = jnp.where(low, o2[:Q_TILE, :LANES], o2[Q_TILE:, :LANES])
                o_ref[0, pl.ds(r0, Q_TILE), cs] = (num / jnp.where(low, sums[0], sums[1])).astype(BF16)
                for half in range(2):
                    here = lane == 2 * c + half
                    max_tile = jnp.where(here, maxes[half], max_tile)
                    den_tile = jnp.where(here, sums[half], den_tile)
            lse_ref[0, pl.ds(r0, Q_TILE), rr * LANES:(rr + 1) * LANES] = max_tile + jnp.log2(den_tile)
            return carry

        lax.fori_loop(0, q_block // Q_TILE, tile, 0, unroll=min(WIDE_TILE_UNROLL, q_block // Q_TILE))


def _attn_b_branch(qv, kv, vv, slopes, lane_masks, bsz, window, dilation):
    seq_len = qv.shape[0] // bsz
    radius = (window // 2) // dilation
    q_block = min(Q_BLOCK, seq_len)
    width = min(Q_TILE + 2 * radius, seq_len)
    tiles = WIDE_TILE_UNROLL if width >= 2 * LANES else TILE_UNROLL
    res_block = min(dilation, max(1, tiles * Q_TILE // q_block))
    bias = _band_bias_table(slopes, radius, width, float(dilation))
    view = lambda t: t.reshape(bsz, seq_len, t.shape[-1])
    kv_spec = pl.BlockSpec((1, seq_len, res_block * B_WIDTH), lambda b, r, i: (b, 0, r))
    o, lse = pl.pallas_call(
        functools.partial(_attn_b_kernel, seq_len=seq_len, q_block=q_block, res_block=res_block, radius=radius),
        grid=(bsz, dilation // res_block, seq_len // q_block),
        in_specs=[pl.BlockSpec((2, LANES), lambda b, r, i: (0, 0)),
                  pl.BlockSpec(bias.shape, lambda b, r, i: (0, 0, 0, 0), pipeline_mode=pl.Buffered(1)),
                  pl.BlockSpec((1, q_block, res_block * B_WIDTH), lambda b, r, i: (b, i, r)),
                  kv_spec, kv_spec],
        out_specs=[pl.BlockSpec((1, q_block, res_block * B_WIDTH), lambda b, r, i: (b, i, r)),
                   pl.BlockSpec((1, q_block, res_block * LANES), lambda b, r, i: (b, i, r))],
        out_shape=[jax.ShapeDtypeStruct((bsz, seq_len, dilation * B_WIDTH), BF16),
                   jax.ShapeDtypeStruct((bsz, seq_len, dilation * LANES), F32)],
        compiler_params=_params("parallel", "parallel", "arbitrary"),
        name=f"attn_dilated_{dilation}",
    )(lane_masks, bias, view(qv), view(kv), view(vv))
    return o.reshape(bsz * seq_len, dilation * B_WIDTH), lse.reshape(bsz * seq_len, dilation * LANES)


def _layer_norm(v, g, b):
    mu = jnp.mean(v, axis=-1, keepdims=True)
    c = v - mu
    var = jnp.mean(c * c, axis=-1, keepdims=True)
    return c * lax.rsqrt(var + LN_EPS) * g + b


def _mix_out_kernel(x_ref, oa_ref, *rest):
    n_br = len(B_DILATIONS)
    ob_refs, lse_refs = rest[:n_br], rest[n_br:2 * n_br]
    expand_ref, norm_b_ref, w_ref, g_ref, b_ref, y_ref, ob_stage, lse_stage = rest[2 * n_br:]
    obs, lses = [], []
    for idx, ((_, d), o_ref, l_ref) in enumerate(zip(B_DILATIONS, ob_refs, lse_refs)):
        if d == 1:
            obs.append(o_ref[...].astype(F32))
            lses.append(l_ref[...])
            continue
        rows = OUT_TILE // d
        n_cols = B_WIDTH // LANES
        for res in range(d):
            for c in range(n_cols):
                lo_c = res * B_WIDTH + c * LANES
                ob_stage[idx * n_cols + c, pl.ds(res, rows, stride=d), :] = o_ref[:, lo_c:lo_c + LANES].astype(F32)
            lse_stage[idx, pl.ds(res, rows, stride=d), :] = l_ref[:, res * LANES:(res + 1) * LANES]
        obs.append(jnp.concatenate([ob_stage[idx * n_cols + c] for c in range(n_cols)], axis=-1))
        lses.append(lse_stage[idx])
    top = functools.reduce(jnp.maximum, lses)
    es = [jnp.exp2(l - top) for l in lses]
    inv = 1.0 / functools.reduce(lambda a, b: a + b, es)
    ob = None
    for e, o in zip(es, obs):
        wgt = e * inv
        wide = jnp.dot(jnp.concatenate(_split_bf16(wgt), axis=-1), expand_ref[...], preferred_element_type=F32)
        ob = wide * o if ob is None else ob + wide * o
    ms = jnp.mean(ob * ob, axis=-1, keepdims=True)
    ob = (ob * lax.rsqrt(ms + RMS_EPS) * norm_b_ref[...]).astype(BF16)
    mix = jnp.dot(jnp.concatenate([oa_ref[...], ob], axis=-1), w_ref[...], preferred_element_type=F32)
    y_ref[...] = _layer_norm(DEEPNORM_ALPHA * x_ref[...] + mix, g_ref[...], b_ref[...])


def _mix_out(x, oa, obs, lses, expand, norm_b, w_out, ln_g, ln_b):
    n = x.shape[0]
    row = lambda w: pl.BlockSpec((OUT_TILE, w), lambda i: (i, 0))
    full = lambda r, c: pl.BlockSpec((r, c), lambda i: (0, 0), pipeline_mode=pl.Buffered(1))
    strided = lambda w: [pl.BlockSpec((OUT_TILE // d, d * w), lambda i: (i, 0)) for _, d in B_DILATIONS]
    n_br = len(B_DILATIONS)
    return pl.pallas_call(
        _mix_out_kernel,
        grid=(n // OUT_TILE,),
        in_specs=[row(D_MODEL), row(A_WIDTH), *strided(B_WIDTH), *strided(LANES),
                  full(2 * LANES, B_WIDTH), full(1, B_WIDTH), full(D_MODEL, D_MODEL),
                  full(1, D_MODEL), full(1, D_MODEL)],
        out_specs=row(D_MODEL),
        out_shape=jax.ShapeDtypeStruct((n, D_MODEL), F32),
        scratch_shapes=[pltpu.VMEM((n_br * B_WIDTH // LANES, OUT_TILE, LANES), F32),
                        pltpu.VMEM((n_br, OUT_TILE, LANES), F32)],
        compiler_params=_params("parallel"),
        name="mix_out",
    )(x, oa, *obs, *lses, expand, norm_b, w_out, ln_g, ln_b)


def _silu(v):
    return v / (1.0 + jnp.exp(-v))


def _route(scores, biased):
    def top2_sum(c):
        hi01, lo01 = jnp.maximum(c[0], c[1]), jnp.minimum(c[0], c[1])
        hi23, lo23 = jnp.maximum(c[2], c[3]), jnp.minimum(c[2], c[3])
        first = jnp.maximum(hi01, hi23)
        second = jnp.maximum(jnp.minimum(hi01, hi23), jnp.where(hi01 >= hi23, lo01, lo23))
        return first + second

    groups = [biased[g * EXPERTS_PER_GROUP:(g + 1) * EXPERTS_PER_GROUP] for g in range(N_GROUPS)]
    gscore = [top2_sum(c) for c in groups]
    sel = jnp.zeros_like(gscore[0], dtype=jnp.int32)
    best = gscore[0]
    for g in range(1, N_GROUPS):
        better = gscore[g] > best
        sel = jnp.where(better, g, sel)
        best = jnp.where(better, gscore[g], best)

    def pick(cols):
        out = []
        for j in range(EXPERTS_PER_GROUP):
            v = cols[j]
            for g in range(1, N_GROUPS):
                v = jnp.where(sel == g, cols[g * EXPERTS_PER_GROUP + j], v)
            out.append(v)
        return out

    cb, cs = pick(biased), pick(scores)
    i1 = jnp.zeros_like(sel)
    v1 = cb[0]
    for j in range(1, EXPERTS_PER_GROUP):
        better = cb[j] > v1
        i1 = jnp.where(better, j, i1)
        v1 = jnp.where(better, cb[j], v1)
    i2 = jnp.where(i1 == 0, 1, 0)
    v2 = jnp.where(i1 == 0, cb[1], cb[0])
    for j in range(1, EXPERTS_PER_GROUP):
        better = (cb[j] > v2) & (i1 != j) & (i2 != j)
        i2 = jnp.where(better, j, i2)
        v2 = jnp.where(better, cb[j], v2)
    s1 = sum(jnp.where(i1 == j, cs[j], 0.0) for j in range(EXPERTS_PER_GROUP))
    s2 = sum(jnp.where(i2 == j, cs[j], 0.0) for j in range(EXPERTS_PER_GROUP))
    total = s1 + s2
    gates = []
    for e in range(N_EXPERTS):
        g, j = divmod(e, EXPERTS_PER_GROUP)
        in_group = sel == g
        gate = jnp.where(in_group & (i1 == j), s1 / total, 0.0) + jnp.where(in_group & (i2 == j), s2 / total, 0.0)
        gates.append(gate)
    return gates, sel


def _split_bf16(v):
    hi = v.astype(BF16)
    return hi, (v - hi.astype(F32)).astype(BF16)


def _moe_kernel(x_ref, tri_ref, wr_ref, rb_ref, wgu_ref, wd_ref, sgu_ref, sd_ref,
                g_ref, b_ref, y_ref, yb_ref, xb_ref, xs_ref, gs_ref, ys_ref, pos_ref, off_ref):
    grp = pl.program_id(1)
    tile = MOE_SORT
    n_chunks = tile // SORT_CHUNK
    sections = range(MOE_TILE // MOE_SORT)

    def perm_rows(s, c):
        slot = (lax.broadcasted_iota(jnp.int32, (SORT_CHUNK, tile), 0) + c * SORT_CHUNK).astype(F32)
        return jnp.where(pos_ref[s] == slot, 1.0, 0.0).astype(BF16)

    def route_and_sort(s):
        x = x_ref[s * tile:(s + 1) * tile, :]
        xb = x.astype(BF16)
        xb_ref[s * tile:(s + 1) * tile, :] = xb
        x_lo = (x - xb.astype(F32)).astype(BF16)
        w_hi, w_lo = _split_bf16(wr_ref[...])
        logits = (_dot_nt(jnp.concatenate([w_hi, w_lo], axis=0), xb)
                  + jnp.concatenate([_dot_nt(w_hi, x_lo), jnp.zeros((N_EXPERTS, tile), F32)], axis=0))
        logits = logits[:N_EXPERTS] + logits[N_EXPERTS:]
        scores = 1.0 / (1.0 + jnp.exp(-logits))
        biased = scores + rb_ref[...]
        rows = lambda t: [t[j:j + 1, :] for j in range(N_EXPERTS)]
        gates, sel = _route(rows(scores), rows(biased))
        sub = lax.broadcasted_iota(jnp.int32, (SUBLANES, tile), 0)
        onehot = jnp.where(sub == sel, 1.0, 0.0).astype(BF16)
        rank = jnp.dot(onehot, tri_ref[...], preferred_element_type=F32)
        last_lane = lax.broadcasted_iota(jnp.int32, (1, tile), 1) == tile - 1
        pos = jnp.zeros((1, tile), F32)
        start = jnp.float32(0.0)
        off_ref[s, 0] = 0
        for g in range(N_GROUPS):
            pos = jnp.where(sel == g, start + rank[g:g + 1, :] - 1.0, pos)
            start = start + jnp.sum(jnp.where(last_lane, rank[g:g + 1, :], 0.0))
            off_ref[s, g + 1] = start.astype(jnp.int32)
        pos_ref[s] = pos
        gate_rows = jnp.concatenate(gates + [jnp.zeros((LANES - N_EXPERTS, tile), F32)], axis=0)
        g_split = jnp.concatenate(_split_bf16(gate_rows), axis=0)
        for c in range(n_chunks):
            perm = perm_rows(s, c)
            rows_c = slice(c * SORT_CHUNK, (c + 1) * SORT_CHUNK)
            xs_ref[s, rows_c, :] = jnp.dot(perm, xb, preferred_element_type=F32).astype(BF16)
            g2 = _dot_nt(perm, g_split)
            gs_ref[s, rows_c, :] = g2[:, :LANES] + g2[:, LANES:]
        xs_ref[s, tile:, :] = jnp.zeros((MOE_CHUNK, D_MODEL), BF16)
        gs_ref[s, tile:, :] = jnp.zeros((MOE_CHUNK, LANES), F32)
        ys_ref[s] = jnp.zeros(ys_ref.shape[1:], F32)

    def run_group():
        begins, trips = [], []
        for s in sections:
            first, end = off_ref[s, grp], off_ref[s, grp + 1]
            begin = lax.shift_left(lax.shift_right_logical(first, ROW_ALIGN_LOG2), ROW_ALIGN_LOG2)
            begins.append(begin)
            trips.append(jnp.where(end > first, lax.div(end - begin + (MOE_CHUNK - 1), MOE_CHUNK), 0))
        chunk_lane = lax.broadcasted_iota(jnp.int32, (len(sections) * MOE_CHUNK, LANES), 1)

        def chunk(c, carry):
            starts = [pl.multiple_of(jnp.minimum(b + c * MOE_CHUNK, tile), 1 << ROW_ALIGN_LOG2) for b in begins]
            xc = jnp.concatenate([xs_ref[s, pl.ds(r0, MOE_CHUNK), :] for s, r0 in zip(sections, starts)], axis=0)
            gc = jnp.concatenate([gs_ref[s, pl.ds(r0, MOE_CHUNK), :] for s, r0 in zip(sections, starts)], axis=0)
            hidden = []
            for j in range(EXPERTS_PER_GROUP):
                gate = jnp.sum(jnp.where(chunk_lane == grp * EXPERTS_PER_GROUP + j, gc, 0.0), axis=-1, keepdims=True)
                gu = jnp.dot(xc, wgu_ref[j], preferred_element_type=F32)
                h = _silu(gu[:, :D_EXPERT]) * gu[:, D_EXPERT:]
                hidden.append((h * gate).astype(BF16))
            yc = jnp.dot(jnp.concatenate(hidden, axis=-1), wd_ref[0], preferred_element_type=F32)
            for s, r0 in zip(sections, starts):
                ys_ref[s, pl.ds(r0, MOE_CHUNK), :] += yc[s * MOE_CHUNK:(s + 1) * MOE_CHUNK]
            return carry

        lax.fori_loop(0, functools.reduce(jnp.maximum, trips), chunk, 0)

    def unsort_and_norm(s):
        rows_s = slice(s * tile, (s + 1) * tile)
        xb = xb_ref[rows_s, :]
        gu = jnp.dot(xb, sgu_ref[...], preferred_element_type=F32)
        hs = _silu(gu[:, :D_EXPERT]) * gu[:, D_EXPERT:]
        acc = jnp.dot(hs.astype(BF16), sd_ref[...], preferred_element_type=F32)
        for c in range(n_chunks):
            acc = acc + lax.dot_general(perm_rows(s, c),
                                        ys_ref[s, c * SORT_CHUNK:(c + 1) * SORT_CHUNK, :].astype(BF16),
                                        (((0,), (0,)), ((), ())), preferred_element_type=F32)
        y = _layer_norm(DEEPNORM_ALPHA * x_ref[rows_s, :] + acc, g_ref[...], b_ref[...])
        y_ref[rows_s, :] = y
        yb_ref[rows_s, :] = y.astype(BF16)

    @pl.when(grp == 0)
    def _():
        for s in sections:
            route_and_sort(s)

    run_group()

    @pl.when(grp == N_GROUPS - 1)
    def _():
        for s in sections:
            unsort_and_norm(s)


def _moe(x, tri, w_router, router_bias, w_gate_up, w_down, ws_gate_up, ws_down, ln_g, ln_b):
    n = x.shape[0]
    row = pl.BlockSpec((MOE_TILE, D_MODEL), lambda i, g: (i, 0))
    full = lambda r, c: pl.BlockSpec((r, c), lambda i, g: (0, 0), pipeline_mode=pl.Buffered(1))
    group_w = lambda n_mat, r, c: pl.BlockSpec((n_mat, r, c), lambda i, g: (g, 0, 0))
    n_sec, sec_rows = MOE_TILE // MOE_SORT, MOE_SORT + MOE_CHUNK
    return pl.pallas_call(
        _moe_kernel,
        grid=(n // MOE_TILE, N_GROUPS),
        in_specs=[row, full(MOE_SORT, MOE_SORT), full(N_EXPERTS, D_MODEL), full(N_EXPERTS, 1),
                  group_w(EXPERTS_PER_GROUP, D_MODEL, 2 * D_EXPERT), group_w(1, EXPERTS_PER_GROUP * D_EXPERT, D_MODEL),
                  full(D_MODEL, 2 * D_EXPERT), full(D_EXPERT, D_MODEL),
                  full(1, D_MODEL), full(1, D_MODEL)],
        out_specs=[row, row],
        out_shape=[jax.ShapeDtypeStruct((n, D_MODEL), F32), jax.ShapeDtypeStruct((n, D_MODEL), BF16)],
        scratch_shapes=[pltpu.VMEM((MOE_TILE, D_MODEL), BF16),
                        pltpu.VMEM((n_sec, sec_rows, D_MODEL), BF16),
                        pltpu.VMEM((n_sec, sec_rows, LANES), F32),
                        pltpu.VMEM((n_sec, sec_rows, D_MODEL), F32),
                        pltpu.VMEM((n_sec, 1, MOE_SORT), F32),
                        pltpu.SMEM((n_sec, N_GROUPS + 1), jnp.int32)],
        compiler_params=pltpu.CompilerParams(dimension_semantics=("parallel", "arbitrary"),
                                             vmem_limit_bytes=MOE_VMEM_LIMIT),
        name="moe",
    )(x, tri, w_router, router_bias, w_gate_up, w_down, ws_gate_up, ws_down, ln_g, ln_b)


def _alibi_slopes():
    n_heads = A_HEADS + B_HEADS
    h = jnp.arange(1, n_heads + 1, dtype=F32)
    return jnp.exp2(-8.0 * h / n_heads)


def _layer(x, xb, p, l, shape):
    bsz, seq_len = shape
    tokens = bsz * seq_len
    qa, ka, va, qb, kb, vb = _inproj(xb, p["w_in"][l])
    seq = lambda t: t.reshape(bsz, seq_len, t.shape[-1])
    oa = _attn_a(seq(qa), seq(ka), seq(va), p["sink2"][l], p["slopes_a"], p["lane_masks"], p["norm_a"][l])
    obs, lses = [], []
    for idx, (window, dilation) in enumerate(B_DILATIONS):
        o, lse = _attn_b_branch(qb[idx], kb[idx], vb[idx], p["slopes_b"], p["lane_masks"], bsz, window, dilation)
        obs.append(o)
        lses.append(lse)
    x = _mix_out(x, oa.reshape(tokens, A_WIDTH), obs, lses, p["expand"], p["norm_b"][l],
                 p["w_out"][l], p["ln1_g"][l], p["ln1_b"][l])
    return _moe(x, p["tri"], p["w_router"], p["router_bias"], p["w_gate_up"][l], p["w_down"][l],
                p["ws_gate_up"][l], p["ws_down"][l], p["ln2_g"][l], p["ln2_b"][l])


def _trunk(x, p):
    bsz, seq_len, _ = x.shape
    h = x.reshape(bsz * seq_len, D_MODEL)
    hb = h
    for l in range(DEPTH):
        h, hb = _layer(h, hb, p, l, (bsz, seq_len))
    return h.reshape(bsz, seq_len, D_MODEL)


def _prepare(w_in, sink_a, norm_a, norm_b, w_out, ln1_g, ln1_b, w_router, router_bias,
             w_gate, w_up, w_down, ws_gate, ws_up, ws_down, ln2_g, ln2_b):
    slopes = _alibi_slopes()
    head_lane = jnp.arange(B_WIDTH)[None, :] // HEAD_DIM
    lane_half = jnp.arange(LANES)[None, :] // HEAD_DIM

    order = jnp.array(A_HEAD_ORDER)
    a_cols = (order[:, None] * HEAD_DIM + jnp.arange(HEAD_DIM)[None, :]).reshape(A_WIDTH)
    w_proj = jnp.concatenate([w_in[..., a_cols], w_in[..., A_WIDTH:]], axis=-1)
    w_mix = jnp.concatenate([w_out[:, a_cols, :], w_out[:, A_WIDTH:, :]], axis=1)
    return {
        "w_in": w_proj.astype(BF16),
        "sink2": sink_a.astype(F32)[:, order] * LOG2E,
        "slopes_a": slopes[0::2][order], "slopes_b": slopes[1::2],
        "lane_masks": (lane_half == jnp.arange(2)[:, None]).astype(BF16),
        "expand": (jnp.arange(2 * LANES)[:, None] % LANES == head_lane).astype(BF16),
        "tri": jnp.tri(MOE_SORT, dtype=BF16).T,
        "norm_a": norm_a[:, a_cols].reshape(DEPTH, 1, A_WIDTH),
        "norm_b": norm_b.reshape(DEPTH, 1, B_WIDTH),
        "w_out": w_mix.astype(BF16),
        "ln1_g": ln1_g.reshape(DEPTH, 1, D_MODEL), "ln1_b": ln1_b.reshape(DEPTH, 1, D_MODEL),
        "w_router": w_router.T, "router_bias": router_bias.reshape(N_EXPERTS, 1),
        "w_gate_up": jnp.concatenate([w_gate.astype(BF16), w_up.astype(BF16)], axis=-1),
        "w_down": w_down.astype(BF16).reshape(DEPTH, N_GROUPS, EXPERTS_PER_GROUP * D_EXPERT, D_MODEL),
        "ws_gate_up": jnp.concatenate([ws_gate.astype(BF16), ws_up.astype(BF16)], axis=-1),
        "ws_down": ws_down.astype(BF16),
        "ln2_g": ln2_g.reshape(DEPTH, 1, D_MODEL), "ln2_b": ln2_b.reshape(DEPTH, 1, D_MODEL),
    }


def kernel(x_prompt, x_sample, w_in, sink_a, norm_a, norm_b, w_out, ln1_g, ln1_b, w_router, router_bias,
           w_gate, w_up, w_down, ws_gate, ws_up, ws_down, ln2_g, ln2_b):
    p = _prepare(w_in, sink_a, norm_a, norm_b, w_out, ln1_g, ln1_b, w_router, router_bias,
                 w_gate, w_up, w_down, ws_gate, ws_up, ws_down, ln2_g, ln2_b)
    return _trunk(x_prompt, p), _trunk(x_sample, p)
```
